```python
import math
import jax, jax.numpy as jnp
from jax import lax
import numpy as np

D_MODEL = 1024
BATCH = 8
SEQ = 2048
DEPTH = 1
DEC_BATCH = 32
DEC_SEQ = 8
PAST_LEN = 8192
PAGE_SIZE = 128

MIX_WIDTH = D_MODEL
ATT_HEADS = 8
HEAD_DIM = 64
ATT_WIDTH = ATT_HEADS * HEAD_DIM
BLOCK = 256
TOPK = 3
QUERY_ROWS = 128
ROPE_THETA = 10000.0
SSM_HEADS = 8
SSM_HEAD_DIM = 64
SSM_INNER = SSM_HEADS * SSM_HEAD_DIM
SSM_GROUPS = 2
SSM_STATE = 128
CONV_W = 4
CONV_DIM = SSM_INNER + 2 * SSM_GROUPS * SSM_STATE
SSD_CHUNK = 128
IN_COLS = 3 * ATT_WIDTH + SSM_INNER + CONV_DIM + SSM_HEADS
MEM_LEN = 256
X_HEADS = 4
X_HEAD_DIM = 128
X_WIDTH = X_HEADS * X_HEAD_DIM
D_FF = 4 * D_MODEL
EPS = 1e-6

kernel_name = 'moba_ssd_parallel_hybrid_step'


def rmsnorm(x, g):
    xf = x.astype(jnp.float32)
    y = xf * lax.rsqrt(jnp.mean(xf * xf, axis=-1, keepdims=True) + EPS)
    return (y * g.astype(jnp.float32)).astype(x.dtype)


def rotary(x, pos):
    half = x.shape[-1] // 2
    inv_freq = ROPE_THETA ** (-jnp.arange(half, dtype=jnp.float32) / half)
    ang = pos.astype(jnp.float32)[:, None] * inv_freq[None, :]
    cos = jnp.cos(ang)[None, :, None, :]
    sin = jnp.sin(ang)[None, :, None, :]
    xf = x.astype(jnp.float32)
    x1, x2 = xf[..., :half], xf[..., half:]
    return jnp.concatenate([x1 * cos - x2 * sin, x2 * cos + x1 * sin], axis=-1).astype(x.dtype)


def moba_attention(q, k, v, q_pos):
    b, t, h, dh = q.shape
    l = k.shape[1]
    nbp = -(-l // BLOCK)
    nb_full = l // BLOCK
    n_score = max(nb_full, TOPK)
    kpad = ((0, 0), (0, nbp * BLOCK - l), (0, 0), (0, 0))
    kb = jnp.pad(k, kpad).reshape(b, nbp, BLOCK, h, dh)
    vb = jnp.pad(v, kpad).reshape(b, nbp, BLOCK, h, dh)
    kmean = jnp.mean(kb[:, :nb_full].astype(jnp.float32), axis=2)
    kmean = jnp.pad(kmean, ((0, 0), (0, n_score - nb_full), (0, 0), (0, 0)))
    own = q_pos // BLOCK
    gate = jnp.einsum('bthd,bjhd->bhtj', q.astype(jnp.float32), kmean)
    gate = jnp.where(jnp.arange(n_score)[None, :] < own[:, None], gate, -jnp.inf)
    _, sel = lax.top_k(gate, TOPK)
    sel_ok = sel < own[:, None]
    sel = jnp.where(sel_ok, sel, 0)
    own_b = jnp.broadcast_to(own[:, None], (b, h, t, 1)).astype(sel.dtype)
    blocks = jnp.concatenate([sel, own_b], axis=-1)
    block_ok = jnp.concatenate([sel_ok, jnp.ones((b, h, t, 1), bool)], axis=-1)
    n_sel = TOPK + 1

    qc = max(1, min(t, QUERY_ROWS // b))
    nq = -(-t // qc)
    padq = nq * qc - t
    qs = jnp.pad(q.astype(jnp.float32) * (dh ** -0.5), ((0, 0), (0, padq), (0, 0), (0, 0)))
    qs = qs.reshape(b, nq, qc, h, dh).transpose(1, 0, 3, 2, 4)
    blk = jnp.pad(blocks, ((0, 0), (0, 0), (0, padq), (0, 0)))
    blk = blk.reshape(b, h, nq, qc, n_sel).transpose(2, 0, 1, 3, 4)
    okc = jnp.pad(block_ok, ((0, 0), (0, 0), (0, padq), (0, 0)), constant_values=True)
    okc = okc.reshape(b, h, nq, qc, n_sel).transpose(2, 0, 1, 3, 4)
    posc = jnp.pad(q_pos, (0, padq), mode='edge').reshape(nq, qc)
    bi = jnp.arange(b)[:, None, None, None]
    hi = jnp.arange(h)[None, :, None, None]
    offs = jnp.arange(BLOCK)

    def chunk(args):
        qq, bc, oc, pc = args
        kg = kb[bi, bc, :, hi].astype(jnp.float32)
        vg = vb[bi, bc, :, hi].astype(jnp.float32)
        kpos = bc[..., None] * BLOCK + offs
        mask = oc[..., None] & (kpos <= pc[None, None, :, None, None])
        s = jnp.einsum('bhqd,bhqnkd->bhqnk', qq, kg)
        s = jnp.where(mask, s, -jnp.inf).reshape(b, h, qc, n_sel * BLOCK)
        p = jax.nn.softmax(s, axis=-1).reshape(b, h, qc, n_sel, BLOCK)
        return jnp.einsum('bhqnk,bhqnkd->bhqd', p, vg)

    o = lax.map(chunk, (qs, blk, okc, posc))
    o = o.transpose(1, 0, 3, 2, 4).reshape(b, nq * qc, h, dh)[:, :t]
    return o.astype(q.dtype)


def causal_conv(xbc, buf, w, bias):
    t = xbc.shape[1]
    full = jnp.concatenate([buf.astype(xbc.dtype), xbc], axis=1)
    y = bias
    for i in range(CONV_W):
        y = y + full[:, i:i + t] * w[i]
    return y, full[:, full.shape[1] - (CONV_W - 1):]


def ssd_scan(x, dt, a, bm, cm, h0):
    b, t = x.shape[:2]
    cl = min(SSD_CHUNK, t)
    nc = -(-t // cl)
    pad = nc * cl - t
    rep = SSM_HEADS // SSM_GROUPS
    bh = jnp.repeat(bm.astype(jnp.float32), rep, axis=2)
    ch = jnp.repeat(cm.astype(jnp.float32), rep, axis=2)
    xdt = x.astype(jnp.float32) * dt[..., None]
    da = dt * a.astype(jnp.float32)

    def chunked(z):
        z = jnp.pad(z, ((0, 0), (0, pad)) + ((0, 0),) * (z.ndim - 2))
        return jnp.moveaxis(z.reshape((b, nc, cl) + z.shape[2:]), 1, 0)

    causal = jnp.tril(jnp.ones((cl, cl), bool))[None, :, :, None]

    def step(hprev, inp):
        xc, dac, bc, cc = inp
        cum = jnp.cumsum(dac, axis=1)
        seg = cum[:, :, None, :] - cum[:, None, :, :]
        decay = jnp.exp(jnp.where(causal, seg, -jnp.inf))
        scores = jnp.einsum('blhn,bshn->blsh', cc, bc) * decay
        y = jnp.einsum('blsh,bshp->blhp', scores, xc)
        y = y + jnp.einsum('blhn,bhpn->blhp', cc, hprev) * jnp.exp(cum)[..., None]
        tail = jnp.exp(cum[:, -1:, :] - cum)
        hnew = hprev * jnp.exp(cum[:, -1])[:, :, None, None] + jnp.einsum('bshn,bsh,bshp->bhpn', bc, tail, xc)
        return hnew, y

    h_last, ys = lax.scan(step, h0.astype(jnp.float32), (chunked(xdt), chunked(da), chunked(bh), chunked(ch)))
    y = jnp.moveaxis(ys, 0, 1).reshape(b, nc * cl, SSM_HEADS, SSM_HEAD_DIM)[:, :t]
    return y, h_last


def gated_group_norm(y, z, g):
    yz = y * jax.nn.silu(z.astype(jnp.float32))
    shp = yz.shape
    yg = yz.reshape(shp[:-1] + (SSM_GROUPS, SSM_INNER // SSM_GROUPS))
    yg = yg * lax.rsqrt(jnp.mean(yg * yg, axis=-1, keepdims=True) + EPS)
    return yg.reshape(shp) * g.astype(jnp.float32)


def memory_kv(mem, lp):
    b, m, _ = mem.shape
    mn = rmsnorm(mem, lp['ln_mem_g'])
    mk = rmsnorm((mn @ lp['wk_x']).reshape(b, m, X_HEADS, X_HEAD_DIM), lp['kx_norm_g'])
    mv = (mn @ lp['wv_x']).reshape(b, m, X_HEADS, X_HEAD_DIM)
    return mk, mv


def cross_attn(h, mem_k, mem_v, lp):
    b, t, _ = h.shape
    q = rmsnorm((h @ lp['wq_x']).reshape(b, t, X_HEADS, X_HEAD_DIM), lp['qx_norm_g'])
    s = jnp.einsum('bthd,bmhd->bhtm', q.astype(jnp.float32), mem_k.astype(jnp.float32)) * (X_HEAD_DIM ** -0.5)
    p = jax.nn.softmax(s, axis=-1)
    o = jnp.einsum('bhtm,bmhd->bthd', p, mem_v.astype(jnp.float32)).reshape(b, t, X_WIDTH)
    return o.astype(h.dtype) @ lp['wo_x']


def layer(x, pos, k_past, v_past, conv_buf, h0, mem_k, mem_v, lp):
    b, t, _ = x.shape
    hn = rmsnorm(x, lp['ln_mix_g'])
    proj = hn @ lp['w_in']
    q, k, v, z, xbc, dt_raw = jnp.split(
        proj, [ATT_WIDTH, 2 * ATT_WIDTH, 3 * ATT_WIDTH, 3 * ATT_WIDTH + SSM_INNER,
               3 * ATT_WIDTH + SSM_INNER + CONV_DIM], axis=-1)
    q = rotary(rmsnorm(q.reshape(b, t, ATT_HEADS, HEAD_DIM), lp['q_norm_g']), pos)
    k = rotary(rmsnorm(k.reshape(b, t, ATT_HEADS, HEAD_DIM), lp['k_norm_g']), pos)
    v = v.reshape(b, t, ATT_HEADS, HEAD_DIM)
    if k_past is None:
        k_all, v_all = k, v
    else:
        k_all = jnp.concatenate([k_past.astype(k.dtype), k], axis=1)
        v_all = jnp.concatenate([v_past.astype(v.dtype), v], axis=1)
    att = moba_attention(q, k_all, v_all, pos).reshape(b, t, ATT_WIDTH)
    xbc_c, new_conv = causal_conv(xbc, conv_buf, lp['conv_w'], lp['conv_b'])
    xbc_c = jax.nn.silu(xbc_c)
    xs, bm, cm = jnp.split(xbc_c, [SSM_INNER, SSM_INNER + SSM_GROUPS * SSM_STATE], axis=-1)
    xs = xs.reshape(b, t, SSM_HEADS, SSM_HEAD_DIM)
    dt = jax.nn.softplus(dt_raw.astype(jnp.float32) + lp['dt_bias'].astype(jnp.float32))
    a = -jnp.exp(lp['a_log'].astype(jnp.float32))
    y, h_new = ssd_scan(xs, dt, a, bm.reshape(b, t, SSM_GROUPS, SSM_STATE),
                        cm.reshape(b, t, SSM_GROUPS, SSM_STATE), h0)
    y = y + lp['d_skip'].astype(jnp.float32)[:, None] * xs.astype(jnp.float32)
    y = gated_group_norm(y.reshape(b, t, SSM_INNER), z, lp['ssm_norm_g'])
    x = x + jnp.concatenate([att, y.astype(x.dtype)], axis=-1) @ lp['w_out']
    x = x + cross_attn(rmsnorm(x, lp['ln_x_g']), mem_k, mem_v, lp)
    hm = rmsnorm(x, lp['ln_mlp_g'])
    x = x + jnp.square(jax.nn.relu(hm @ lp['w_up'])) @ lp['w_down']
    return x, k, v, new_conv, h_new


def setup_inputs(seed: int = 0) -> dict:
    key = jax.random.key(seed)
    ks = jax.random.split(key, 40)
    f32 = jnp.float32
    n_pages = PAST_LEN // PAGE_SIZE
    used = DEC_BATCH * n_pages
    n_pool = used + max(1, used // 4)

    def nrm(k, shape, scale):
        return scale * jax.random.normal(k, shape, f32)

    def gain(k, n):
        return 1.0 + 0.01 * jax.random.normal(k, (DEPTH, n), f32)

    page_table = jax.random.permutation(ks[0], n_pool)[:used].reshape(DEC_BATCH, n_pages).astype(jnp.int32)
    dt0 = jnp.exp(jax.random.uniform(ks[1], (DEPTH, SSM_HEADS), f32, math.log(1e-3), math.log(1e-1)))
    return {
        'x_prompt': nrm(ks[2], (BATCH, SEQ, D_MODEL), 1.0),
        'x_sample': nrm(ks[3], (DEC_BATCH, DEC_SEQ, D_MODEL), 1.0),
        'mem_prompt': nrm(ks[4], (BATCH, MEM_LEN, D_MODEL), 1.0),
        'cache_k': nrm(ks[5], (DEPTH, n_pool, PAGE_SIZE, ATT_HEADS, HEAD_DIM), 1.0),
        'cache_v': nrm(ks[6], (DEPTH, n_pool, PAGE_SIZE, ATT_HEADS, HEAD_DIM), 1.0),
        'page_table': page_table,
        'state_conv': nrm(ks[7], (DEPTH, DEC_BATCH, CONV_W - 1, CONV_DIM), 1.0),
        'state_ssm': nrm(ks[8], (DEPTH, DEC_BATCH, SSM_HEADS, SSM_HEAD_DIM, SSM_STATE), 0.5),
        'cache_mem_k': nrm(ks[9], (DEPTH, DEC_BATCH, MEM_LEN, X_HEADS, X_HEAD_DIM), 1.0),
        'cache_mem_v': nrm(ks[10], (DEPTH, DEC_BATCH, MEM_LEN, X_HEADS, X_HEAD_DIM), 1.0),
        'ln_mix_g': gain(ks[11], D_MODEL),
        'w_in': nrm(ks[12], (DEPTH, D_MODEL, IN_COLS), D_MODEL ** -0.5),
        'q_norm_g': gain(ks[13], HEAD_DIM),
        'k_norm_g': gain(ks[14], HEAD_DIM),
        'conv_w': nrm(ks[15], (DEPTH, CONV_W, CONV_DIM), CONV_W ** -0.5),
        'conv_b': nrm(ks[16], (DEPTH, CONV_DIM), 0.01),
        'dt_bias': dt0 + jnp.log(-jnp.expm1(-dt0)),
        'a_log': jnp.log(jax.random.uniform(ks[17], (DEPTH, SSM_HEADS), f32, 1.0, 16.0)),
        'd_skip': gain(ks[18], SSM_HEADS),
        'ssm_norm_g': gain(ks[19], SSM_INNER),
        'w_out': nrm(ks[20], (DEPTH, MIX_WIDTH, D_MODEL), MIX_WIDTH ** -0.5),
        'ln_x_g': gain(ks[21], D_MODEL),
        'ln_mem_g': gain(ks[22], D_MODEL),
        'wq_x': nrm(ks[23], (DEPTH, D_MODEL, X_WIDTH), D_MODEL ** -0.5),
        'wk_x': nrm(ks[24], (DEPTH, D_MODEL, X_WIDTH), D_MODEL ** -0.5),
        'wv_x': nrm(ks[25], (DEPTH, D_MODEL, X_WIDTH), D_MODEL ** -0.5),
        'qx_norm_g': gain(ks[26], X_HEAD_DIM),
        'kx_norm_g': gain(ks[27], X_HEAD_DIM),
        'wo_x': nrm(ks[28], (DEPTH, X_WIDTH, D_MODEL), X_WIDTH ** -0.5),
        'ln_mlp_g': gain(ks[29], D_MODEL),
        'w_up': nrm(ks[30], (DEPTH, D_MODEL, D_FF), D_MODEL ** -0.5),
        'w_down': nrm(ks[31], (DEPTH, D_FF, D_MODEL), D_FF ** -0.5),
    }


def reference(x_prompt, x_sample, mem_prompt, cache_k, cache_v, page_table, state_conv, state_ssm,
              cache_mem_k, cache_mem_v, ln_mix_g, w_in, q_norm_g, k_norm_g, conv_w, conv_b, dt_bias,
              a_log, d_skip, ssm_norm_g, w_out, ln_x_g, ln_mem_g, wq_x, wk_x, wv_x, qx_norm_g,
              kx_norm_g, wo_x, ln_mlp_g, w_up, w_down):
    bp, tp, _ = x_prompt.shape
    bs, ts, _ = x_sample.shape
    past_len = page_table.shape[1] * cache_k.shape[2]
    pos_p = jnp.arange(tp, dtype=jnp.int32)
    pos_s = past_len + jnp.arange(ts, dtype=jnp.int32)
    hp, hs = x_prompt, x_sample
    kp_l, vp_l, cp_l, sp_l, mkp_l, mvp_l = [], [], [], [], [], []
    ks_l, vs_l, cs_l, ss_l = [], [], [], []
    for l in range(DEPTH):
        lp = {'ln_mix_g': ln_mix_g[l], 'w_in': w_in[l], 'q_norm_g': q_norm_g[l], 'k_norm_g': k_norm_g[l],
              'conv_w': conv_w[l], 'conv_b': conv_b[l], 'dt_bias': dt_bias[l], 'a_log': a_log[l],
              'd_skip': d_skip[l], 'ssm_norm_g': ssm_norm_g[l], 'w_out': w_out[l], 'ln_x_g': ln_x_g[l],
              'ln_mem_g': ln_mem_g[l], 'wq_x': wq_x[l], 'wk_x': wk_x[l], 'wv_x': wv_x[l],
              'qx_norm_g': qx_norm_g[l], 'kx_norm_g': kx_norm_g[l], 'wo_x': wo_x[l],
              'ln_mlp_g': ln_mlp_g[l], 'w_up': w_up[l], 'w_down': w_down[l]}
        mk_p, mv_p = memory_kv(mem_prompt, lp)
        conv0 = jnp.zeros((bp, CONV_W - 1, CONV_DIM), x_prompt.dtype)
        h0 = jnp.zeros((bp, SSM_HEADS, SSM_HEAD_DIM, SSM_STATE), jnp.float32)
        hp, kp, vp, cp, sp = layer(hp, pos_p, None, None, conv0, h0, mk_p, mv_p, lp)
        k_past = cache_k[l][page_table].reshape(bs, -1, ATT_HEADS, HEAD_DIM)
        v_past = cache_v[l][page_table].reshape(bs, -1, ATT_HEADS, HEAD_DIM)
        hs, kn, vn, cn, sn = layer(hs, pos_s, k_past, v_past, state_conv[l], state_ssm[l],
                                   cache_mem_k[l], cache_mem_v[l], lp)
        kp_l.append(kp); vp_l.append(vp); cp_l.append(cp); sp_l.append(sp)
        mkp_l.append(mk_p); mvp_l.append(mv_p)
        ks_l.append(kn); vs_l.append(vn); cs_l.append(cn); ss_l.append(sn)
    new_k_prompt = jnp.stack(kp_l)
    new_v_prompt = jnp.stack(vp_l)
    new_conv_prompt = jnp.stack(cp_l)
    new_ssm_prompt = jnp.stack(sp_l)
    new_mem_k_prompt = jnp.stack(mkp_l)
    new_mem_v_prompt = jnp.stack(mvp_l)
    new_k_sample = jnp.stack(ks_l)
    new_v_sample = jnp.stack(vs_l)
    new_conv_sample = jnp.stack(cs_l)
    new_ssm_sample = jnp.stack(ss_l)
    return (hp, hs, new_k_prompt, new_v_prompt, new_conv_prompt, new_ssm_prompt, new_mem_k_prompt,
            new_mem_v_prompt, new_k_sample, new_v_sample, new_conv_sample, new_ssm_sample)
```

```python
import functools

import jax
import jax.numpy as jnp
from jax import lax
from jax.experimental import pallas as pl
from jax.experimental.pallas import tpu as pltpu

F32 = jnp.float32
BF16 = jnp.bfloat16

D_MODEL = 1024
ATT_HEADS = 8
HEAD_DIM = 64
ATT_WIDTH = ATT_HEADS * HEAD_DIM
MOBA_BLOCK = 256
TOPK = 3
ROPE_THETA = 10000.0
SSM_HEADS = 8
SSM_HEAD_DIM = 64
SSM_INNER = SSM_HEADS * SSM_HEAD_DIM
SSM_GROUPS = 2
SSM_STATE = 128
CONV_W = 4
CONV_DIM = SSM_INNER + 2 * SSM_GROUPS * SSM_STATE
SSD_CHUNK = 128
X_HEADS = 4
X_HEAD_DIM = 128
X_WIDTH = X_HEADS * X_HEAD_DIM
D_FF = 4 * D_MODEL
EPS = 1e-6

LANES = 128
CONV_PAD = 8
VMEM_LIMIT = 56 * 1024 * 1024
NEG = -1e30
PAGES_PER_STEP = 8


def _dot(a, b):
    return jnp.dot(a, b, preferred_element_type=F32)


def _dot_nt(a, b):
    return lax.dot_general(a, b, (((1,), (1,)), ((), ())), preferred_element_type=F32)


def _dot_tn(a, b):
    return lax.dot_general(a, b, (((0,), (0,)), ((), ())), preferred_element_type=F32)


def _split3(x):
    hi = x.astype(BF16)
    r = x - hi.astype(F32)
    mid = r.astype(BF16)
    lo = (r - mid.astype(F32)).astype(BF16)
    return hi, mid, lo


def _dot_nt_f32(a, b):
    a0, a1, a2 = _split3(a)
    b0, b1, b2 = _split3(b)
    return (_dot_nt(a0, b0) + (_dot_nt(a0, b1) + _dot_nt(a1, b0))
            + (_dot_nt(a1, b1) + _dot_nt(a0, b2) + _dot_nt(a2, b0)))


def _rms(x, g):
    ms = jnp.mean(x * x, axis=-1, keepdims=True)
    return x * lax.rsqrt(ms + EPS) * g


def _sigmoid(x):
    return 1.0 / (1.0 + jnp.exp(-x))


def _params(*sem):
    return pltpu.CompilerParams(dimension_semantics=sem, vmem_limit_bytes=VMEM_LIMIT)


def _const(shape):
    return pl.BlockSpec(shape, lambda *_: (0,) * len(shape))


def _const_single(shape):
    return pl.BlockSpec(shape, lambda *_: (0,) * len(shape), pipeline_mode=pl.Buffered(1))


def _inproj_kernel(x_ref, g_ref, w_ref, wdt_ref, gq_ref, gk_ref, cos_ref, sin_ref, seg_ref,
                   q_ref, k_ref, v_ref, z_ref, xbc_ref, dt_ref):
    hn = _rms(x_ref[...], g_ref[...]).astype(BF16)
    cos = cos_ref[...]
    sin = sin_ref[...]
    lane = lax.broadcasted_iota(jnp.int32, cos.shape, 1)
    first_half = (lane % HEAD_DIM) < (HEAD_DIM // 2)

    def head_norm_rot(p, g):
        ms = _dot((p * p).astype(BF16), seg_ref[...])
        pn = p * lax.rsqrt(ms + EPS) * g
        swapped = jnp.where(first_half, pltpu.roll(pn, ATT_WIDTH - HEAD_DIM // 2, 1),
                            pltpu.roll(pn, HEAD_DIM // 2, 1))
        return pn * cos + swapped * sin

    a = ATT_WIDTH
    q_ref[...] = head_norm_rot(_dot(hn, w_ref[:, 0:a]), gq_ref[...])
    k_ref[...] = head_norm_rot(_dot(hn, w_ref[:, a:2 * a]), gk_ref[...])
    v_ref[...] = _dot(hn, w_ref[:, 2 * a:3 * a])
    z_ref[...] = _dot(hn, w_ref[:, 3 * a:3 * a + SSM_INNER])
    xbc_ref[...] = _dot(hn, w_ref[:, 3 * a + SSM_INNER:3 * a + SSM_INNER + CONV_DIM])
    dt_ref[...] = _dot(hn, wdt_ref[...])


def _inproj(x, g, w_main, w_dt, gq, gk, cos, sin, seg, tm):
    n = x.shape[0]
    tab_blocks = cos.shape[0] // tm
    row = lambda w: pl.BlockSpec((tm, w), lambda i: (i, 0))
    tab = pl.BlockSpec((tm, ATT_WIDTH), lambda i: (i % tab_blocks, 0))
    outs = [jax.ShapeDtypeStruct((n, w), F32)
            for w in (ATT_WIDTH, ATT_WIDTH, ATT_WIDTH, SSM_INNER, CONV_DIM, LANES)]
    return pl.pallas_call(
        _inproj_kernel,
        grid=(n // tm,),
        in_specs=[row(D_MODEL), _const(g.shape), _const_single(w_main.shape), _const(w_dt.shape),
                  _const(gq.shape), _const(gk.shape), tab, tab, _const(seg.shape)],
        out_specs=[row(ATT_WIDTH), row(ATT_WIDTH), row(ATT_WIDTH), row(SSM_INNER), row(CONV_DIM),
                   row(LANES)],
        out_shape=outs,
        compiler_params=_params("parallel"),
        name="inproj",
    )(x, g, w_main, w_dt, gq, gk, cos, sin, seg)


def _topk_mask(gate, n_valid):
    ncol = gate.shape[1]
    colj = lax.broadcasted_iota(jnp.int32, gate.shape, 1)
    cnt = jnp.zeros(gate.shape, jnp.int32)
    for jp in range(ncol):
        gj = gate[:, jp:jp + 1]
        beats = (gj > gate) | ((gj == gate) & (jp < colj))
        cnt = cnt + jnp.where(beats, jnp.where(jp < n_valid, 1, 0), 0)
    return jnp.where((colj < n_valid) & (cnt < TOPK), 1.0, 0.0)


def _moba_kernel(q_ref, k_ref, v_ref, o_ref, *, nblk):
    i = pl.program_id(2)
    bq = MOBA_BLOCK
    q = q_ref[0]
    lane = lax.broadcasted_iota(jnp.int32, q.shape, 1)
    lo = lane < HEAD_DIM
    kmean = jnp.mean(k_ref[0].reshape(nblk, bq, LANES), axis=1)
    npad = max(16, nblk)
    if npad > nblk:
        kmean = jnp.concatenate([kmean, jnp.zeros((npad - nblk, LANES), F32)], axis=0)
    heads_q = (jnp.where(lo, q, 0.0), jnp.where(lo, 0.0, q))
    sel = [_topk_mask(_dot_nt_f32(qh, kmean), i) for qh in heads_q]
    colj = lax.broadcasted_iota(jnp.int32, (bq, npad), 1)
    qs = [(qh * (HEAD_DIM ** -0.5)).astype(BF16) for qh in heads_q]

    def update(carry, kj, vj, bias):
        out = []
        for h in range(2):
            m, l, acc = carry[h]
            s = _dot_nt(qs[h], kj) + bias[h]
            m_new = jnp.maximum(m, jnp.max(s, axis=-1, keepdims=True))
            alpha = jnp.exp(m - m_new)
            p = jnp.exp(s - m_new)
            l = alpha * l + jnp.sum(p, axis=-1, keepdims=True)
            acc = alpha * acc + _dot(p.astype(BF16), vj)
            out.append((m_new, l, acc))
        return tuple(out)

    def kv_block(j):
        start = pl.multiple_of(j * bq, bq)
        return (k_ref[0, pl.ds(start, bq), :].astype(BF16), v_ref[0, pl.ds(start, bq), :].astype(BF16))

    rowi = lax.broadcasted_iota(jnp.int32, (bq, bq), 0)
    coli = lax.broadcasted_iota(jnp.int32, (bq, bq), 1)
    causal = jnp.where(coli <= rowi, 0.0, NEG)
    init = tuple((jnp.full((bq, 1), NEG, F32), jnp.zeros((bq, 1), F32), jnp.zeros((bq, LANES), F32))
                 for _ in range(2))
    kd, vd = kv_block(i)
    carry = update(init, kd, vd, (causal, causal))

    def body(j, carry):
        kj, vj = kv_block(j)
        bias = [(jnp.sum(jnp.where(colj == j, sel[h], 0.0), axis=1, keepdims=True) - 1.0) * (-NEG)
                for h in range(2)]
        return update(carry, kj, vj, bias)

    carry = lax.fori_loop(0, i, body, carry)
    (_, l0, a0), (_, l1, a1) = carry
    o_ref[0] = jnp.where(lo, a0 / l0, a1 / l1).astype(o_ref.dtype)


def _moba_prompt(q, k, v):
    b, t, _ = q.shape
    nblk = t // MOBA_BLOCK
    npair = ATT_WIDTH // LANES
    qspec = pl.BlockSpec((1, MOBA_BLOCK, LANES), lambda bi, p, i: (bi, i, p))
    kvspec = pl.BlockSpec((1, t, LANES), lambda bi, p, i: (bi, 0, p))
    return pl.pallas_call(
        functools.partial(_moba_kernel, nblk=nblk),
        grid=(b, npair, nblk),
        in_specs=[qspec, kvspec, kvspec],
        out_specs=qspec,
        out_shape=jax.ShapeDtypeStruct((b, t, ATT_WIDTH), BF16),
        compiler_params=_params("parallel", "parallel", "arbitrary"),
        name="moba_prompt",
    )(q, k, v)


def _expand_heads(q):
    t = q.shape[0]
    tiled = jnp.concatenate([q] * ATT_HEADS, axis=0)
    rowh = lax.broadcasted_iota(jnp.int32, tiled.shape, 0) // t
    laneh = lax.broadcasted_iota(jnp.int32, tiled.shape, 1) // HEAD_DIM
    return jnp.where(rowh == laneh, tiled, 0.0)


def _sample_scores_kernel(pt_ref, q_ref, *rest, page, npg):
    del pt_ref
    k_refs = rest[:npg]
    s_ref, gate_ref, kmean_ref = rest[npg:]
    c = pl.program_id(1)
    qe = _expand_heads(q_ref[0])
    qb = qe.astype(BF16)
    pages_per_block = MOBA_BLOCK // page
    sums = []
    for pi in range(npg):
        kp = k_refs[pi][0]
        s_ref[0, :, pi * page:(pi + 1) * page] = _dot_nt(qb, kp.astype(BF16))
        sums.append(jnp.sum(kp, axis=0, keepdims=True))
    blocks_per_step = npg // pages_per_block
    for jj in range(blocks_per_step):
        tot = sums[jj * pages_per_block]
        for u in range(1, pages_per_block):
            tot = tot + sums[jj * pages_per_block + u]
        kmean_ref[pl.ds(c * blocks_per_step + jj, 1), :] = tot * (1.0 / MOBA_BLOCK)

    @pl.when(c == pl.num_programs(1) - 1)
    def _():
        gate_ref[0] = _dot_nt_f32(qe, kmean_ref[...])


def _sample_scores(page_table, q, cache_k):
    b, t, _ = q.shape
    n_pool, page, _ = cache_k.shape
    n_pages = page_table.shape[1]
    npg = PAGES_PER_STEP
    nsteps = n_pages // npg
    nblk = n_pages * page // MOBA_BLOCK
    rows = ATT_HEADS * t

    def kspec(pi):
        return pl.BlockSpec((1, page, ATT_WIDTH), lambda bi, c, pt: (pt[bi, c * npg + pi], 0, 0))

    grid_spec = pltpu.PrefetchScalarGridSpec(
        num_scalar_prefetch=1,
        grid=(b, nsteps),
        in_specs=[pl.BlockSpec((1, t, ATT_WIDTH), lambda bi, c, pt: (bi, 0, 0))]
                 + [kspec(pi) for pi in range(npg)],
        out_specs=[pl.BlockSpec((1, rows, npg * page), lambda bi, c, pt: (bi, 0, c)),
                   pl.BlockSpec((1, rows, nblk), lambda bi, c, pt: (bi, 0, 0))],
        scratch_shapes=[pltpu.VMEM((nblk, ATT_WIDTH), F32)],
    )
    return pl.pallas_call(
        functools.partial(_sample_scores_kernel, page=page, npg=npg),
        grid_spec=grid_spec,
        out_shape=[jax.ShapeDtypeStruct((b, rows, n_pages * page), F32),
                   jax.ShapeDtypeStruct((b, rows, nblk), F32)],
        compiler_params=_params("parallel", "arbitrary"),
        name="sample_scores",
    )(page_table, q, *([cache_k] * npg))


def _sample_attend_kernel(pt_ref, s_ref, gate_ref, q_ref, kn_ref, vn_ref, *rest, page, npg, nsteps):
    del pt_ref
    v_refs = rest[:npg]
    o_ref, p_ref, acc_ref, linv_ref = rest[npg:]
    c = pl.program_id(1)
    t = q_ref.shape[1]
    rows = ATT_HEADS * t
    step_keys = npg * page
    nblk = gate_ref.shape[2]
    scale = HEAD_DIM ** -0.5

    @pl.when(c == 0)
    def _():
        sel = _topk_mask(gate_ref[0], nblk).astype(BF16)
        blk_of_key = lax.broadcasted_iota(jnp.int32, (nblk, step_keys), 1) // MOBA_BLOCK
        blk_row = lax.broadcasted_iota(jnp.int32, (nblk, step_keys), 0)
        qe = _expand_heads(q_ref[0])
        s_own = _dot_nt(qe.astype(BF16), kn_ref[0].astype(BF16)) * scale
        qpos = lax.broadcasted_iota(jnp.int32, s_own.shape, 0) % t
        kpos = lax.broadcasted_iota(jnp.int32, s_own.shape, 1)
        s_own = jnp.where(kpos <= qpos, s_own, NEG)
        m = jnp.max(s_own, axis=-1, keepdims=True)
        masked = []
        for cc in range(nsteps):
            expand = jnp.where(blk_of_key + cc * (step_keys // MOBA_BLOCK) == blk_row, 1.0, 0.0)
            keep = _dot(sel, expand.astype(BF16)) > 0.5
            s = jnp.where(keep, s_ref[0, :, cc * step_keys:(cc + 1) * step_keys] * scale, NEG)
            m = jnp.maximum(m, jnp.max(s, axis=-1, keepdims=True))
            masked.append(s)
        p_own = jnp.exp(s_own - m)
        l = jnp.sum(p_own, axis=-1, keepdims=True)
        for cc in range(nsteps):
            p = jnp.exp(masked[cc] - m)
            l = l + jnp.sum(p, axis=-1, keepdims=True)
            p_ref[cc] = p.astype(BF16)
        linv_ref[...] = 1.0 / l
        vn = vn_ref[0]
        acc = jnp.zeros((rows, ATT_WIDTH), F32)
        for tt in range(t):
            acc = acc + p_own[:, tt:tt + 1] * vn[tt:tt + 1, :]
        acc_ref[...] = acc

    p = p_ref[c]
    acc = acc_ref[...]
    for pi in range(npg):
        acc = acc + _dot(p[:, pi * page:(pi + 1) * page], v_refs[pi][0].astype(BF16))
    acc_ref[...] = acc

    @pl.when(c == nsteps - 1)
    def _():
        full = acc_ref[...] * linv_ref[...]
        laneh = lax.broadcasted_iota(jnp.int32, (t, ATT_WIDTH), 1) // HEAD_DIM
        out = jnp.zeros((t, ATT_WIDTH), F32)
        for h in range(ATT_HEADS):
            out = out + jnp.where(laneh == h, full[h * t:(h + 1) * t, :], 0.0)
        o_ref[0] = out.astype(o_ref.dtype)


def _sample_attend(page_table, scores, gate, q, k_new, v_new, cache_v):
    b, t, _ = q.shape
    n_pool, page, _ = cache_v.shape
    n_pages = page_table.shape[1]
    npg = PAGES_PER_STEP
    nsteps = n_pages // npg
    rows = ATT_HEADS * t
    nblk = gate.shape[2]
    per_b = lambda shape: pl.BlockSpec((1,) + shape, lambda bi, c, pt: (bi, 0, 0))

    def vspec(pi):
        return pl.BlockSpec((1, page, ATT_WIDTH), lambda bi, c, pt: (pt[bi, c * npg + pi], 0, 0))

    grid_spec = pltpu.PrefetchScalarGridSpec(
        num_scalar_prefetch=1,
        grid=(b, nsteps),
        in_specs=[per_b((rows, n_pages * page)), per_b((rows, nblk)), per_b((t, ATT_WIDTH)),
                  per_b((t, ATT_WIDTH)), per_b((t, ATT_WIDTH))] + [vspec(pi) for pi in range(npg)],
        out_specs=per_b((t, ATT_WIDTH)),
        scratch_shapes=[pltpu.VMEM((nsteps, rows, npg * page), BF16),
                        pltpu.VMEM((rows, ATT_WIDTH), F32),
                        pltpu.VMEM((rows, 1), F32)],
    )
    return pl.pallas_call(
        functools.partial(_sample_attend_kernel, page=page, npg=npg, nsteps=nsteps),
        grid_spec=grid_spec,
        out_shape=jax.ShapeDtypeStruct((b, t, ATT_WIDTH), F32),
        compiler_params=_params("parallel", "arbitrary"),
        name="sample_attend",
    )(page_table, scores, gate, q, k_new, v_new, *([cache_v] * npg))


def _cumsum_rows(x):
    n = x.shape[0]
    row = lax.broadcasted_iota(jnp.int32, x.shape, 0)
    step = 1
    while step < n:
        x = x + jnp.where(row >= step, pltpu.roll(x, step, 0), 0.0)
        step *= 2
    return x


def _ssd_kernel(xbc_ref, dtr_ref, z_ref, conv0_ref, h0_ref, cw_ref, cb_ref, dtb_ref, alog_ref,
                dsk_ref, gn_ref, y_ref, convo_ref, ho_ref, xpad_ref, h_ref, *, cl, valid):
    c = pl.program_id(1)
    nc = pl.num_programs(1)

    @pl.when(c == 0)
    def _():
        xpad_ref[0:CONV_PAD, :] = conv0_ref[0]
        h_ref[...] = h0_ref[0]

    x = xbc_ref[0]
    xpad_ref[CONV_PAD:CONV_PAD + cl, :] = x
    w = cw_ref[...]
    conv = cb_ref[...] + x * w[CONV_W - 1:CONV_W, :]
    for back in range(1, CONV_W):
        conv = conv + xpad_ref[pl.ds(CONV_PAD - back, cl), :] * w[CONV_W - 1 - back:CONV_W - back, :]

    @pl.when(c == nc - 1)
    def _():
        convo_ref[0] = xpad_ref[CONV_PAD + valid - (CONV_W - 1):CONV_PAD + valid, :]

    xpad_ref[0:CONV_PAD, :] = xpad_ref[cl:cl + CONV_PAD, :]
    xc = conv * _sigmoid(conv)
    xs = xc[:, :SSM_INNER]
    gw = SSM_STATE
    bm = [xc[:, SSM_INNER + g * gw:SSM_INNER + (g + 1) * gw].astype(BF16) for g in range(SSM_GROUPS)]
    cm = [xc[:, SSM_INNER + (SSM_GROUPS + g) * gw:SSM_INNER + (SSM_GROUPS + g + 1) * gw].astype(BF16)
          for g in range(SSM_GROUPS)]

    pre = dtr_ref[0] + dtb_ref[...]
    dt = jnp.maximum(pre, 0.0) + jnp.log1p(jnp.exp(-jnp.abs(pre)))
    rowl = lax.broadcasted_iota(jnp.int32, dt.shape, 0)
    if valid < cl:
        dt = jnp.where(rowl < valid, dt, 0.0)
    da = dt * (-jnp.exp(alog_ref[...]))
    cum = _cumsum_rows(da)
    cum_t = cum.T
    cum_last = cum[cl - 1:cl, :]
    ecum = jnp.exp(cum)
    tail = jnp.exp(cum_last - cum)
    elast = jnp.exp(cum_last)

    ri = lax.broadcasted_iota(jnp.int32, (cl, cl), 0)
    ci = lax.broadcasted_iota(jnp.int32, (cl, cl), 1)
    causal = ci <= ri
    lane = lax.broadcasted_iota(jnp.int32, (cl, LANES), 1)
    lo = lane < SSM_HEAD_DIM
    row_lo = lax.broadcasted_iota(jnp.int32, (LANES, 1), 0) < SSM_HEAD_DIM
    heads_per_group = SSM_HEADS // SSM_GROUPS
    gmat = [_dot_nt(cm[g], bm[g]) for g in range(SSM_GROUPS)]

    def pick(cols, h0):
        return jnp.where(lo, cols[:, h0:h0 + 1], cols[:, h0 + 1:h0 + 2])

    ys = []
    for p in range(SSM_HEADS // 2):
        h0 = 2 * p
        g = h0 // heads_per_group
        xs_p = xs[:, p * LANES:(p + 1) * LANES]
        xdt = xs_p * pick(dt, h0)
        xdt_b = xdt.astype(BF16)
        parts = []
        for h in (h0, h0 + 1):
            seg = cum[:, h:h + 1] - cum_t[h:h + 1, :]
            dec = jnp.where(causal, jnp.exp(jnp.minimum(seg, 0.0)), 0.0)
            parts.append(_dot((gmat[g] * dec).astype(BF16), xdt_b))
        hp = h_ref[p * LANES:(p + 1) * LANES, :]
        y_p = jnp.where(lo, parts[0], parts[1])
        y_p = y_p + _dot_nt(cm[g], hp.astype(BF16)) * pick(ecum, h0)
        y_p = y_p + dsk_ref[:, p * LANES:(p + 1) * LANES] * xs_p
        xw = (xdt * pick(tail, h0)).astype(BF16)
        decay_rows = jnp.where(row_lo, elast[:, h0:h0 + 1], elast[:, h0 + 1:h0 + 2])
        h_ref[p * LANES:(p + 1) * LANES, :] = hp * decay_rows + _dot_tn(xw, bm[g])
        ys.append(y_p)

    y = jnp.concatenate(ys, axis=1)
    zz = z_ref[0]
    yz = y * (zz * _sigmoid(zz))
    gwid = SSM_INNER // SSM_GROUPS
    normed = []
    for g in range(SSM_GROUPS):
        part = yz[:, g * gwid:(g + 1) * gwid]
        normed.append(part * lax.rsqrt(jnp.mean(part * part, axis=-1, keepdims=True) + EPS))
    y_ref[0] = (jnp.concatenate(normed, axis=1) * gn_ref[...]).astype(y_ref.dtype)

    @pl.when(c == nc - 1)
    def _():
        ho_ref[0] = h_ref[...]


def _ssd(xbc, dtr, z, conv0, h0, cw, cb, dtb, alog, dsk, gn, cl, valid, out_dtype):
    b, t, _ = xbc.shape
    nc = t // cl
    rows = lambda w: pl.BlockSpec((1, cl, w), lambda bi, c: (bi, c, 0))
    per_b = lambda shape: pl.BlockSpec((1,) + shape, lambda bi, c: (bi, 0, 0))
    hshape = (SSM_HEADS * SSM_HEAD_DIM, SSM_STATE)
    return pl.pallas_call(
        functools.partial(_ssd_kernel, cl=cl, valid=valid),
        grid=(b, nc),
        in_specs=[rows(CONV_DIM), rows(LANES), rows(SSM_INNER), per_b((CONV_PAD, CONV_DIM)),
                  per_b(hshape), _const(cw.shape), _const(cb.shape), _const(dtb.shape),
                  _const(alog.shape), _const(dsk.shape), _const(gn.shape)],
        out_specs=[rows(SSM_INNER), per_b((CONV_W - 1, CONV_DIM)), per_b(hshape)],
        out_shape=[jax.ShapeDtypeStruct((b, t, SSM_INNER), out_dtype),
                   jax.ShapeDtypeStruct((b, CONV_W - 1, CONV_DIM), F32),
                   jax.ShapeDtypeStruct((b,) + hshape, F32)],
        scratch_shapes=[pltpu.VMEM((CONV_PAD + cl, CONV_DIM), F32), pltpu.VMEM(hshape, F32)],
        compiler_params=_params("parallel", "arbitrary"),
        name="ssd",
    )(xbc, dtr, z, conv0, h0, cw, cb, dtb, alog, dsk, gn)


def _outproj_kernel(x_ref, att_ref, y_ref, wa_ref, wy_ref, g_ref, wq_ref, gq_ref, x1_ref, qx_ref):
    mix = _dot(att_ref[...].astype(BF16), wa_ref[...]) + _dot(y_ref[...].astype(BF16), wy_ref[...])
    x1 = x_ref[...] + mix
    x1_ref[...] = x1
    hx = _rms(x1, g_ref[...]).astype(BF16)
    qx = _dot(hx, wq_ref[...])
    gq = gq_ref[...]
    for h in range(X_HEADS):
        sl = slice(h * X_HEAD_DIM, (h + 1) * X_HEAD_DIM)
        qx_ref[:, sl] = _rms(qx[:, sl], gq).astype(qx_ref.dtype)


def _outproj(x, att, y, w_att, w_y, g, wq, gq, tm, act_dtype):
    n = x.shape[0]
    row = lambda w: pl.BlockSpec((tm, w), lambda i: (i, 0))
    return pl.pallas_call(
        _outproj_kernel,
        grid=(n // tm,),
        in_specs=[row(D_MODEL), row(ATT_WIDTH), row(SSM_INNER), _const(w_att.shape), _const(w_y.shape),
                  _const(g.shape), _const(wq.shape), _const(gq.shape)],
        out_specs=[row(D_MODEL), row(X_WIDTH)],
        out_shape=[jax.ShapeDtypeStruct((n, D_MODEL), F32), jax.ShapeDtypeStruct((n, X_WIDTH), act_dtype)],
        compiler_params=_params("parallel"),
        name="outproj",
    )(x, att, y, w_att, w_y, g, wq, gq)


def _xattn_kernel(q_ref, k_ref, v_ref, o_ref):
    scale = X_HEAD_DIM ** -0.5
    for h in range(X_HEADS):
        sl = slice(h * X_HEAD_DIM, (h + 1) * X_HEAD_DIM)
        q = (q_ref[0, :, sl].astype(F32) * scale).astype(BF16)
        s = _dot_nt(q, k_ref[0, :, sl].astype(BF16))
        m = jnp.max(s, axis=-1, keepdims=True)
        p = jnp.exp(s - m)
        l = jnp.sum(p, axis=-1, keepdims=True)
        o = _dot(p.astype(BF16), v_ref[0, :, sl].astype(BF16)) / l
        o_ref[0, :, sl] = o.astype(o_ref.dtype)


def _xattn(q, mk, mv, tq, act_dtype):
    b, t, _ = q.shape
    m = mk.shape[1]
    qspec = pl.BlockSpec((1, tq, X_WIDTH), lambda bi, i: (bi, i, 0))
    kvspec = pl.BlockSpec((1, m, X_WIDTH), lambda bi, i: (bi, 0, 0))
    return pl.pallas_call(
        _xattn_kernel,
        grid=(b, t // tq),
        in_specs=[qspec, kvspec, kvspec],
        out_specs=qspec,
        out_shape=jax.ShapeDtypeStruct((b, t, X_WIDTH), act_dtype),
        compiler_params=_params("parallel", "parallel"),
        name="xattn",
    )(q, mk, mv)


def _mlp_kernel(x1_ref, o_ref, wo_ref, g_ref, wu_ref, wd_ref, out_ref, *, ff_chunk):
    x2 = x1_ref[...] + _dot(o_ref[...].astype(BF16), wo_ref[...])
    hm = _rms(x2, g_ref[...]).astype(BF16)
    acc = x2
    for cidx in range(D_FF // ff_chunk):
        sl = slice(cidx * ff_chunk, (cidx + 1) * ff_chunk)
        up = jnp.maximum(_dot(hm, wu_ref[:, sl]), 0.0)
        acc = acc + _dot((up * up).astype(BF16), wd_ref[sl, :])
    out_ref[...] = acc


def _mlp(x1, o, wo, g, wu, wd, tm):
    n = x1.shape[0]
    row = lambda w: pl.BlockSpec((tm, w), lambda i: (i, 0))
    return pl.pallas_call(
        functools.partial(_mlp_kernel, ff_chunk=1024),
        grid=(n // tm,),
        in_specs=[row(D_MODEL), row(X_WIDTH), _const_single(wo.shape), _const(g.shape),
                  _const_single(wu.shape), _const_single(wd.shape)],
        out_specs=row(D_MODEL),
        out_shape=jax.ShapeDtypeStruct((n, D_MODEL), F32),
        compiler_params=_params("parallel"),
        name="mlp",
    )(x1, o, wo, g, wu, wd)


def _memkv_kernel(m_ref, g_ref, wk_ref, wv_ref, gk_ref, mk_ref, mv_ref):
    mn = _rms(m_ref[...], g_ref[...]).astype(BF16)
    mk = _dot(mn, wk_ref[...])
    gk = gk_ref[...]
    for h in range(X_HEADS):
        sl = slice(h * X_HEAD_DIM, (h + 1) * X_HEAD_DIM)
        mk_ref[:, sl] = _rms(mk[:, sl], gk)
    mv_ref[...] = _dot(mn, wv_ref[...])


def _memkv(mem, g, wk, wv, gk, tm):
    n = mem.shape[0]
    row = lambda w: pl.BlockSpec((tm, w), lambda i: (i, 0))
    return pl.pallas_call(
        _memkv_kernel,
        grid=(n // tm,),
        in_specs=[row(D_MODEL), _const(g.shape), _const(wk.shape), _const(wv.shape), _const(gk.shape)],
        out_specs=[row(X_WIDTH), row(X_WIDTH)],
        out_shape=[jax.ShapeDtypeStruct((n, X_WIDTH), F32)] * 2,
        compiler_params=_params("parallel"),
        name="memkv",
    )(mem, g, wk, wv, gk)


def _rope_tables(pos):
    half = HEAD_DIM // 2
    inv_freq = ROPE_THETA ** (-jnp.arange(half, dtype=F32) / half)
    ang = pos.astype(F32)[:, None] * inv_freq[None, :]
    cos = jnp.cos(ang)
    sin = jnp.sin(ang)
    cos_h = jnp.concatenate([cos, cos], axis=-1)
    sin_h = jnp.concatenate([-sin, sin], axis=-1)
    return jnp.tile(cos_h, (1, ATT_HEADS)), jnp.tile(sin_h, (1, ATT_HEADS))


def _row_tile(n, want):
    tm = min(n, want)
    assert n % tm == 0
    return tm


def _layer(x, cos, sin, conv0, h0, mem_k, mem_v, wts, attend, tm, act_dtype):
    b, t, _ = x.shape
    n = b * t
    xf = x.reshape(n, D_MODEL)
    q, k, v, z, xbc, dtr = _inproj(xf, wts["ln_mix_g"], wts["w_main"], wts["w_dt"], wts["q_norm_g"],
                                   wts["k_norm_g"], cos, sin, wts["seg"], tm)
    k3 = k.reshape(b, t, ATT_WIDTH)
    v3 = v.reshape(b, t, ATT_WIDTH)
    att = attend(q.reshape(b, t, ATT_WIDTH), k3, v3)

    cl = SSD_CHUNK
    tp = -(-t // cl) * cl
    pad3 = lambda a: jnp.pad(a.reshape(b, t, -1), ((0, 0), (0, tp - t), (0, 0)))
    y, new_conv, new_h = _ssd(pad3(xbc), pad3(dtr), pad3(z), conv0, h0, wts["conv_w"], wts["conv_b"],
                              wts["dt_bias"], wts["a_log"], wts["d_skip"], wts["ssm_norm_g"],
                              cl, min(t, cl) if tp != t else cl, act_dtype)
    y = y[:, :t].reshape(n, SSM_INNER)

    x1, qx = _outproj(xf, att.reshape(n, ATT_WIDTH), y, wts["w_out_att"], wts["w_out_ssm"], wts["ln_x_g"],
                      wts["wq_x"], wts["qx_norm_g"], tm, act_dtype)
    tq = _row_tile(t, 512)
    o = _xattn(qx.reshape(b, t, X_WIDTH), mem_k, mem_v, tq, act_dtype)
    out = _mlp(x1, o.reshape(n, X_WIDTH), wts["wo_x"], wts["ln_mlp_g"], wts["w_up"], wts["w_down"], tm)
    return out.reshape(b, t, D_MODEL), k3, v3, new_conv, new_h


def kernel(x_prompt, x_sample, mem_prompt, cache_k, cache_v, page_table, state_conv, state_ssm,
           cache_mem_k, cache_mem_v, ln_mix_g, w_in, q_norm_g, k_norm_g, conv_w, conv_b, dt_bias,
           a_log, d_skip, ssm_norm_g, w_out, ln_x_g, ln_mem_g, wq_x, wk_x, wv_x, qx_norm_g,
           kx_norm_g, wo_x, ln_mlp_g, w_up, w_down):
    depth = w_in.shape[0]
    assert depth == 1, "single-layer step"
    bp, tp, _ = x_prompt.shape
    bs, ts, _ = x_sample.shape
    n_pool, page = cache_k.shape[1], cache_k.shape[2]
    past_len = page_table.shape[1] * page
    assert tp % MOBA_BLOCK == 0 and past_len % MOBA_BLOCK == 0 and MOBA_BLOCK % page == 0
    assert ts <= MOBA_BLOCK and ts <= SSD_CHUNK and ts >= CONV_W - 1
    l = 0
    main_cols = 3 * ATT_WIDTH + SSM_INNER + CONV_DIM

    def lanes(vec, width=LANES):
        return jnp.pad(vec.astype(F32), (0, width - vec.shape[0])).reshape(1, width)

    seg = jnp.kron(jnp.eye(ATT_HEADS, dtype=F32), jnp.full((HEAD_DIM, HEAD_DIM), 1.0 / HEAD_DIM, F32))
    wts = {
        "ln_mix_g": ln_mix_g[l].reshape(1, D_MODEL),
        "w_main": w_in[l][:, :main_cols].astype(BF16),
        "w_dt": jnp.pad(w_in[l][:, main_cols:], ((0, 0), (0, LANES - SSM_HEADS))).astype(BF16),
        "q_norm_g": jnp.tile(q_norm_g[l], ATT_HEADS).reshape(1, ATT_WIDTH),
        "k_norm_g": jnp.tile(k_norm_g[l], ATT_HEADS).reshape(1, ATT_WIDTH),
        "seg": seg.astype(BF16),
        "conv_w": conv_w[l],
        "conv_b": conv_b[l].reshape(1, CONV_DIM),
        "dt_bias": lanes(dt_bias[l]),
        "a_log": lanes(a_log[l]),
        "d_skip": jnp.repeat(d_skip[l].astype(F32), SSM_HEAD_DIM).reshape(1, SSM_INNER),
        "ssm_norm_g": ssm_norm_g[l].reshape(1, SSM_INNER),
        "w_out_att": w_out[l][:ATT_WIDTH].astype(BF16),
        "w_out_ssm": w_out[l][ATT_WIDTH:].astype(BF16),
        "ln_x_g": ln_x_g[l].reshape(1, D_MODEL),
        "wq_x": wq_x[l].astype(BF16),
        "qx_norm_g": qx_norm_g[l].reshape(1, X_HEAD_DIM),
        "wo_x": wo_x[l].astype(BF16),
        "ln_mlp_g": ln_mlp_g[l].reshape(1, D_MODEL),
        "w_up": w_up[l].astype(BF16),
        "w_down": w_down[l].astype(BF16),
    }

    mem_rows = mem_prompt.reshape(-1, D_MODEL)
    mk_p, mv_p = _memkv(mem_rows, ln_mem_g[l].reshape(1, D_MODEL), wk_x[l].astype(BF16),
                        wv_x[l].astype(BF16), kx_norm_g[l].reshape(1, X_HEAD_DIM),
                        _row_tile(mem_rows.shape[0], 512))
    mem_len = mem_prompt.shape[1]
    mk_p = mk_p.reshape(bp, mem_len, X_WIDTH)
    mv_p = mv_p.reshape(bp, mem_len, X_WIDTH)
    cos_p, sin_p = _rope_tables(jnp.arange(tp, dtype=jnp.int32))
    conv0_p = jnp.zeros((bp, CONV_PAD, CONV_DIM), F32)
    h0_p = jnp.zeros((bp, SSM_HEADS * SSM_HEAD_DIM, SSM_STATE), F32)
    tm_p = _row_tile(tp, 512)
    y_p, k_p, v_p, conv_p, ssm_p = _layer(x_prompt, cos_p, sin_p, conv0_p, h0_p, mk_p, mv_p, wts,
                                          _moba_prompt, tm_p, BF16)

    cos_s, sin_s = _rope_tables(past_len + jnp.arange(ts, dtype=jnp.int32))
    cos_s = jnp.tile(cos_s, (bs, 1))
    sin_s = jnp.tile(sin_s, (bs, 1))
    conv0_s = jnp.pad(state_conv[l], ((0, 0), (CONV_PAD - (CONV_W - 1), 0), (0, 0)))
    h0_s = state_ssm[l].reshape(bs, SSM_HEADS * SSM_HEAD_DIM, SSM_STATE)
    ck = cache_k[l].reshape(n_pool, page, ATT_WIDTH)
    cv = cache_v[l].reshape(n_pool, page, ATT_WIDTH)

    def attend_sample(q, k_new, v_new):
        scores, gate = _sample_scores(page_table, q, ck)
        return _sample_attend(page_table, scores, gate, q, k_new, v_new, cv)

    y_s, k_s, v_s, conv_s, ssm_s = _layer(
        x_sample, cos_s, sin_s, conv0_s, h0_s, cache_mem_k[l].reshape(bs, -1, X_WIDTH),
        cache_mem_v[l].reshape(bs, -1, X_WIDTH), wts, attend_sample, bs * ts, F32)

    heads = lambda a, b, t: a.reshape(1, b, t, ATT_HEADS, HEAD_DIM)
    state = lambda a, b: a.reshape(1, b, SSM_HEADS, SSM_HEAD_DIM, SSM_STATE)
    return (y_p, y_s, heads(k_p, bp, tp), heads(v_p, bp, tp), conv_p[None], state(ssm_p, bp),
            mk_p.reshape(1, bp, mem_len, X_HEADS, X_HEAD_DIM), mv_p.reshape(1, bp, mem_len, X_HEADS, X_HEAD_DIM),
            heads(k_s, bs, ts), heads(v_s, bs, ts), conv_s[None], state(ssm_s, bs))
```

```python
import functools

import jax
import jax.numpy as jnp
from jax import lax
from jax.experimental import pallas as pl
from jax.experimental.pallas import tpu as pltpu

F32 = jnp.float32
BF16 = jnp.bfloat16

D_MODEL = 1024
ATT_HEADS = 8
HEAD_DIM = 64
ATT_WIDTH = ATT_HEADS * HEAD_DIM
MOBA_BLOCK = 256
TOPK = 3
ROPE_THETA = 10000.0
SSM_HEADS = 8
SSM_HEAD_DIM = 64
SSM_INNER = SSM_HEADS * SSM_HEAD_DIM
SSM_GROUPS = 2
SSM_STATE = 128
CONV_W = 4
CONV_DIM = SSM_INNER + 2 * SSM_GROUPS * SSM_STATE
SSD_CHUNK = 128
X_HEADS = 4
X_HEAD_DIM = 128
X_WIDTH = X_HEADS * X_HEAD_DIM
D_FF = 4 * D_MODEL
EPS = 1e-6

LANES = 128
CONV_PAD = 8
VMEM_LIMIT = 56 * 1024 * 1024
NEG = -1e30
LOG2E = 1.4426950408889634
PAGES_PER_STEP = 8


def _dot(a, b):
    return jnp.dot(a, b, preferred_element_type=F32)


def _dot_nt(a, b):
    return lax.dot_general(a, b, (((1,), (1,)), ((), ())), preferred_element_type=F32)


def _dot_tn(a, b):
    return lax.dot_general(a, b, (((0,), (0,)), ((), ())), preferred_element_type=F32)


def _split3(x):
    hi = x.astype(BF16)
    r = x - hi.astype(F32)
    mid = r.astype(BF16)
    lo = (r - mid.astype(F32)).astype(BF16)
    return hi, mid, lo


def _dot_nt_f32(a, b):
    a0, a1, a2 = _split3(a)
    b0, b1, b2 = _split3(b)
    return (_dot_nt(a0, b0) + (_dot_nt(a0, b1) + _dot_nt(a1, b0))
            + (_dot_nt(a1, b1) + _dot_nt(a0, b2) + _dot_nt(a2, b0)))


def _rms(x, g):
    ms = jnp.mean(x * x, axis=-1, keepdims=True)
    return x * lax.rsqrt(ms + EPS) * g


def _sigmoid(x):
    return 1.0 / (1.0 + jnp.exp(-x))


def _params(*sem):
    return pltpu.CompilerParams(dimension_semantics=sem, vmem_limit_bytes=VMEM_LIMIT)


def _const(shape):
    return pl.BlockSpec(shape, lambda *_: (0,) * len(shape))


def _const_single(shape):
    return pl.BlockSpec(shape, lambda *_: (0,) * len(shape), pipeline_mode=pl.Buffered(1))


def _inproj_kernel(x_ref, g_ref, w_ref, wdt_ref, gq_ref, gk_ref, cos_ref, sin_ref, seg_ref,
                   q_ref, k_ref, v_ref, z_ref, xbc_ref, dt_ref):
    hn = _rms(x_ref[...], g_ref[...]).astype(BF16)
    cos = cos_ref[...]
    sin = sin_ref[...]
    lane = lax.broadcasted_iota(jnp.int32, cos.shape, 1)
    first_half = (lane % HEAD_DIM) < (HEAD_DIM // 2)

    def head_norm_rot(p, g):
        ms = _dot((p * p).astype(BF16), seg_ref[...])
        pn = p * lax.rsqrt(ms + EPS) * g
        swapped = jnp.where(first_half, pltpu.roll(pn, ATT_WIDTH - HEAD_DIM // 2, 1),
                            pltpu.roll(pn, HEAD_DIM // 2, 1))
        return pn * cos + swapped * sin

    a = ATT_WIDTH
    q_ref[...] = head_norm_rot(_dot(hn, w_ref[:, 0:a]), gq_ref[...])
    k_ref[...] = head_norm_rot(_dot(hn, w_ref[:, a:2 * a]), gk_ref[...])
    v_ref[...] = _dot(hn, w_ref[:, 2 * a:3 * a])
    z_ref[...] = _dot(hn, w_ref[:, 3 * a:3 * a + SSM_INNER])
    xbc_ref[...] = _dot(hn, w_ref[:, 3 * a + SSM_INNER:3 * a + SSM_INNER + CONV_DIM])
    dt_ref[...] = _dot(hn, wdt_ref[...])


def _inproj(x, g, w_main, w_dt, gq, gk, cos, sin, seg, tm):
    n = x.shape[0]
    tab_blocks = cos.shape[0] // tm
    row = lambda w: pl.BlockSpec((tm, w), lambda i: (i, 0))
    tab = pl.BlockSpec((tm, ATT_WIDTH), lambda i: (i % tab_blocks, 0))
    outs = [jax.ShapeDtypeStruct((n, w), F32)
            for w in (ATT_WIDTH, ATT_WIDTH, ATT_WIDTH, SSM_INNER, CONV_DIM, LANES)]
    return pl.pallas_call(
        _inproj_kernel,
        grid=(n // tm,),
        in_specs=[row(D_MODEL), _const(g.shape), _const_single(w_main.shape), _const(w_dt.shape),
                  _const(gq.shape), _const(gk.shape), tab, tab, _const(seg.shape)],
        out_specs=[row(ATT_WIDTH), row(ATT_WIDTH), row(ATT_WIDTH), row(SSM_INNER), row(CONV_DIM),
                   row(LANES)],
        out_shape=outs,
        compiler_params=_params("parallel"),
        name="inproj",
    )(x, g, w_main, w_dt, gq, gk, cos, sin, seg)


def _topk_keep(gate, n_valid, axis):
    idx = lax.broadcasted_iota(jnp.int32, gate.shape, axis)
    cnt = jnp.zeros(gate.shape, jnp.int32)
    for jp in range(gate.shape[axis]):
        gj = lax.slice_in_dim(gate, jp, jp + 1, axis=axis)
        beats = (gj > gate) | ((gj == gate) & (jp < idx))
        cnt = cnt + jnp.where(beats, jnp.where(jp < n_valid, 1, 0), 0)
    return (idx < n_valid) & (cnt < TOPK)


def _moba_kernel(q_ref, k_ref, v_ref, o_ref, kb_ref, vt_ref, km_ref, *, nblk):
    bq = MOBA_BLOCK
    km_ref[...] = jnp.zeros(km_ref.shape, F32)
    for j in range(nblk):
        kj = k_ref[0, j * bq:(j + 1) * bq, :]
        kb_ref[j] = kj.astype(BF16)
        km_ref[j:j + 1, :] = jnp.mean(kj, axis=0, keepdims=True)
        vt_ref[j] = v_ref[0, j * bq:(j + 1) * bq, :].T.astype(BF16)
    kmean = km_ref[...]

    keyi = lax.broadcasted_iota(jnp.int32, (bq, bq), 0)
    qryi = lax.broadcasted_iota(jnp.int32, (bq, bq), 1)
    causal = jnp.where(keyi <= qryi, 0.0, NEG)
    lo = lax.broadcasted_iota(jnp.int32, (bq, LANES), 1) < HEAD_DIM
    qscale = (HEAD_DIM ** -0.5) * LOG2E

    for own in range(nblk):
        q = q_ref[0, own * bq:(own + 1) * bq, :]
        outs = []
        for h in range(2):
            qh = jnp.where(lo, 0.0, q) if h else jnp.where(lo, q, 0.0)
            qsb = (qh * qscale).astype(BF16)
            bias = None
            if own > TOPK:
                keep = _topk_keep(_dot_nt_f32(kmean, qh), own, 0)
                bias = jnp.where(keep, 0.0, NEG)
            tiles = []
            m = None
            for j in range(own + 1):
                s = _dot_nt(kb_ref[j], qsb)
                if j == own:
                    s = s + causal
                elif bias is not None:
                    s = s + bias[j:j + 1, :]
                tiles.append(s)
                mj = jnp.max(s, axis=0, keepdims=True)
                m = mj if m is None else jnp.maximum(m, mj)
            l = jnp.zeros((1, bq), F32)
            acc = jnp.zeros((HEAD_DIM, bq), F32)
            for j in range(own + 1):
                p = jnp.exp2(tiles[j] - m)
                l = l + jnp.sum(p, axis=0, keepdims=True)
                acc = acc + _dot(vt_ref[j, h * HEAD_DIM:(h + 1) * HEAD_DIM, :], p.astype(BF16))
            outs.append(acc / l)
        o_ref[0, own * bq:(own + 1) * bq, :] = jnp.concatenate(outs, axis=0).T.astype(o_ref.dtype)


def _moba_prompt(q, k, v):
    b, t, _ = q.shape
    nblk = t // MOBA_BLOCK
    npad = max(16, nblk)
    npair = ATT_WIDTH // LANES
    spec = pl.BlockSpec((1, t, LANES), lambda bi, p: (bi, 0, p))
    return pl.pallas_call(
        functools.partial(_moba_kernel, nblk=nblk),
        grid=(b, npair),
        in_specs=[spec, spec, spec],
        out_specs=spec,
        out_shape=jax.ShapeDtypeStruct((b, t, ATT_WIDTH), BF16),
        scratch_shapes=[pltpu.VMEM((nblk, MOBA_BLOCK, LANES), BF16),
                        pltpu.VMEM((nblk, LANES, MOBA_BLOCK), BF16),
                        pltpu.VMEM((npad, LANES), F32)],
        compiler_params=_params("parallel", "parallel"),
        name="moba_prompt",
    )(q, k, v)


def _sample_moba_kernel(pt_ref, q_ref, kn_ref, vn_ref, *rest, page, npg, nblk):
    del pt_ref
    k_refs = rest[:npg]
    v_refs = rest[npg:2 * npg]
    o_ref, km_ref, m_ref, l_ref, oblk_ref, hmask_ref = rest[2 * npg:]
    c = pl.program_id(1)
    rows = q_ref.shape[1]
    t = rows // ATT_HEADS
    pages_per_block = MOBA_BLOCK // page
    blocks_per_step = npg // pages_per_block
    flat = page * ATT_HEADS

    @pl.when(c == 0)
    def _():
        rowh = lax.broadcasted_iota(jnp.int32, (rows, flat), 0) // t
        laneh = lax.broadcasted_iota(jnp.int32, (rows, flat), 1) % ATT_HEADS
        hmask_ref[...] = jnp.where(rowh == laneh, 0.0, NEG)

    qall = q_ref[0]
    qb = (qall * ((HEAD_DIM ** -0.5) * LOG2E)).astype(BF16)
    hmask = hmask_ref[...]
    for jj in range(blocks_per_step):
        tiles = []
        ksum = None
        m = None
        for u in range(pages_per_block):
            kp = k_refs[jj * pages_per_block + u][...]
            ks = jnp.sum(kp, axis=0)
            ksum = ks if ksum is None else ksum + ks
            s = _dot_nt(qb, kp.reshape(flat, HEAD_DIM).astype(BF16)) + hmask
            tiles.append(s)
            mu = jnp.max(s, axis=1, keepdims=True)
            m = mu if m is None else jnp.maximum(m, mu)
        l = jnp.zeros((rows, 1), F32)
        o = jnp.zeros((rows, HEAD_DIM), F32)
        for u in range(pages_per_block):
            p = jnp.exp2(tiles[u] - m)
            l = l + jnp.sum(p, axis=1, keepdims=True)
            vp = v_refs[jj * pages_per_block + u][...].reshape(flat, HEAD_DIM).astype(BF16)
            o = o + _dot(p.astype(BF16), vp)
        j = c * blocks_per_step + jj
        kmean = ksum * (1.0 / MOBA_BLOCK)
        for h in range(ATT_HEADS):
            km_ref[h, pl.ds(j, 1), :] = kmean[h:h + 1, :]
        m_ref[j] = jnp.broadcast_to(m, (rows, LANES))
        l_ref[j] = jnp.broadcast_to(l, (rows, LANES))
        oblk_ref[j] = o

    @pl.when(c == pl.num_programs(1) - 1)
    def _():
        g_all = _dot_nt_f32(qall, km_ref[...].reshape(ATT_HEADS * nblk, HEAD_DIM))
        gate = jnp.concatenate([g_all[h * t:(h + 1) * t, h * nblk:(h + 1) * nblk] for h in range(ATT_HEADS)],
                               axis=0)
        keep = _topk_keep(gate, nblk, 1)
        s_own = _dot_nt(qb, kn_ref[0].reshape(t * ATT_HEADS, HEAD_DIM).astype(BF16))
        ri = lax.broadcasted_iota(jnp.int32, s_own.shape, 0)
        li = lax.broadcasted_iota(jnp.int32, s_own.shape, 1)
        ok = (li % ATT_HEADS == ri // t) & (li // ATT_HEADS <= ri % t)
        s_own = jnp.where(ok, s_own, NEG)
        lane_j = lax.broadcasted_iota(jnp.int32, (rows, nblk), 1)
        m_all = jnp.zeros((rows, nblk), F32)
        l_all = jnp.zeros((rows, nblk), F32)
        for j in range(nblk):
            m_all = jnp.where(lane_j == j, m_ref[j][:, 0:nblk], m_all)
            l_all = jnp.where(lane_j == j, l_ref[j][:, 0:nblk], l_all)
        mx = jnp.maximum(jnp.max(s_own, axis=1, keepdims=True),
                         jnp.max(jnp.where(keep, m_all, NEG), axis=1, keepdims=True))
        p_own = jnp.exp2(s_own - mx)
        w = jnp.where(keep, jnp.exp2(jnp.minimum(m_all - mx, 0.0)), 0.0)
        den = jnp.sum(p_own, axis=1, keepdims=True) + jnp.sum(w * l_all, axis=1, keepdims=True)
        num = _dot(p_own.astype(BF16), vn_ref[0].reshape(t * ATT_HEADS, HEAD_DIM).astype(BF16))
        for j in range(nblk):
            num = num + w[:, j:j + 1] * oblk_ref[j]
        o_ref[0] = num / den


def _sample_moba(page_table, q, k_new, v_new, cache_k, cache_v):
    b, rows, _ = q.shape
    t = rows // ATT_HEADS
    page = cache_k.shape[2]
    n_pages = page_table.shape[1]
    npg = PAGES_PER_STEP
    nsteps = n_pages // npg
    nblk = n_pages * page // MOBA_BLOCK
    per_b = lambda shape: pl.BlockSpec((1,) + shape, lambda bi, c, pt: (bi,) + (0,) * len(shape))

    def pspec(pi):
        return pl.BlockSpec((None, None, page, ATT_HEADS, HEAD_DIM),
                            lambda bi, c, pt: (0, pt[bi, c * npg + pi], 0, 0, 0))

    grid_spec = pltpu.PrefetchScalarGridSpec(
        num_scalar_prefetch=1,
        grid=(b, nsteps),
        in_specs=[per_b((rows, HEAD_DIM)), per_b((t, ATT_HEADS, HEAD_DIM)), per_b((t, ATT_HEADS, HEAD_DIM))]
                 + [pspec(pi) for pi in range(npg)] * 2,
        out_specs=per_b((rows, HEAD_DIM)),
        scratch_shapes=[pltpu.VMEM((ATT_HEADS, nblk, HEAD_DIM), F32),
                        pltpu.VMEM((nblk, rows, LANES), F32),
                        pltpu.VMEM((nblk, rows, LANES), F32),
                        pltpu.VMEM((nblk, rows, HEAD_DIM), F32),
                        pltpu.VMEM((rows, page * ATT_HEADS), F32)],
    )
    return pl.pallas_call(
        functools.partial(_sample_moba_kernel, page=page, npg=npg, nblk=nblk),
        grid_spec=grid_spec,
        out_shape=jax.ShapeDtypeStruct((b, rows, HEAD_DIM), F32),
        compiler_params=_params("parallel", "arbitrary"),
        name="sample_moba",
    )(page_table, q, k_new, v_new, *([cache_k] * npg), *([cache_v] * npg))


def _cumsum_rows(x):
    n = x.shape[0]
    row = lax.broadcasted_iota(jnp.int32, x.shape, 0)
    step = 1
    while step < n:
        x = x + jnp.where(row >= step, pltpu.roll(x, step, 0), 0.0)
        step *= 2
    return x


def _ssd_kernel(xbc_ref, dtr_ref, z_ref, conv0_ref, h0_ref, cw_ref, cb_ref, dtb_ref, alog_ref,
                dsk_ref, gn_ref, y_ref, convo_ref, ho_ref, xpad_ref, h_ref, *, cl, valid):
    c = pl.program_id(1)
    nc = pl.num_programs(1)

    @pl.when(c == 0)
    def _():
        xpad_ref[0:CONV_PAD, :] = conv0_ref[0]
        h_ref[...] = h0_ref[0]

    x = xbc_ref[0]
    xpad_ref[CONV_PAD:CONV_PAD + cl, :] = x
    w = cw_ref[...]
    conv = cb_ref[...] + x * w[CONV_W - 1:CONV_W, :]
    for back in range(1, CONV_W):
        conv = conv + xpad_ref[pl.ds(CONV_PAD - back, cl), :] * w[CONV_W - 1 - back:CONV_W - back, :]

    @pl.when(c == nc - 1)
    def _():
        convo_ref[0] = xpad_ref[CONV_PAD + valid - (CONV_W - 1):CONV_PAD + valid, :]

    xpad_ref[0:CONV_PAD, :] = xpad_ref[cl:cl + CONV_PAD, :]
    xc = conv * _sigmoid(conv)
    xs = xc[:, :SSM_INNER]
    gw = SSM_STATE
    bm = [xc[:, SSM_INNER + g * gw:SSM_INNER + (g + 1) * gw].astype(BF16) for g in range(SSM_GROUPS)]
    cm = [xc[:, SSM_INNER + (SSM_GROUPS + g) * gw:SSM_INNER + (SSM_GROUPS + g + 1) * gw].astype(BF16)
          for g in range(SSM_GROUPS)]

    pre = dtr_ref[0] + dtb_ref[...]
    dt = jnp.maximum(pre, 0.0) + jnp.log1p(jnp.exp(-jnp.abs(pre)))
    rowl = lax.broadcasted_iota(jnp.int32, dt.shape, 0)
    if valid < cl:
        dt = jnp.where(rowl < valid, dt, 0.0)
    da = dt * (-jnp.exp(alog_ref[...]))
    cum = _cumsum_rows(da)
    cum_t = cum.T
    cum_last = cum[cl - 1:cl, :]
    ecum = jnp.exp(cum)
    tail = jnp.exp(cum_last - cum)
    elast = jnp.exp(cum_last)

    ri = lax.broadcasted_iota(jnp.int32, (cl, cl), 0)
    ci = lax.broadcasted_iota(jnp.int32, (cl, cl), 1)
    causal = ci <= ri
    lane = lax.broadcasted_iota(jnp.int32, (cl, LANES), 1)
    lo = lane < SSM_HEAD_DIM
    row_lo = lax.broadcasted_iota(jnp.int32, (LANES, 1), 0) < SSM_HEAD_DIM
    heads_per_group = SSM_HEADS // SSM_GROUPS
    gmat = [_dot_nt(cm[g], bm[g]) for g in range(SSM_GROUPS)]

    def pick(cols, h0):
        return jnp.where(lo, cols[:, h0:h0 + 1], cols[:, h0 + 1:h0 + 2])

    ys = []
    for p in range(SSM_HEADS // 2):
        h0 = 2 * p
        g = h0 // heads_per_group
        xs_p = xs[:, p * LANES:(p + 1) * LANES]
        xdt = xs_p * pick(dt, h0)
        xdt_b = xdt.astype(BF16)
        parts = []
        for h in (h0, h0 + 1):
            seg = cum[:, h:h + 1] - cum_t[h:h + 1, :]
            dec = jnp.where(causal, jnp.exp(jnp.minimum(seg, 0.0)), 0.0)
            parts.append(_dot((gmat[g] * dec).astype(BF16), xdt_b))
        hp = h_ref[p * LANES:(p + 1) * LANES, :]
        y_p = jnp.where(lo, parts[0], parts[1])
        y_p = y_p + _dot_nt(cm[g], hp.astype(BF16)) * pick(ecum, h0)
        y_p = y_p + dsk_ref[:, p * LANES:(p + 1) * LANES] * xs_p
        xw = (xdt * pick(tail, h0)).astype(BF16)
        decay_rows = jnp.where(row_lo, elast[:, h0:h0 + 1], elast[:, h0 + 1:h0 + 2])
        h_ref[p * LANES:(p + 1) * LANES, :] = hp * decay_rows + _dot_tn(xw, bm[g])
        ys.append(y_p)

    y = jnp.concatenate(ys, axis=1)
    zz = z_ref[0]
    yz = y * (zz * _sigmoid(zz))
    gwid = SSM_INNER // SSM_GROUPS
    normed = []
    for g in range(SSM_GROUPS):
        part = yz[:, g * gwid:(g + 1) * gwid]
        normed.append(part * lax.rsqrt(jnp.mean(part * part, axis=-1, keepdims=True) + EPS))
    y_ref[0] = (jnp.concatenate(normed, axis=1) * gn_ref[...]).astype(y_ref.dtype)

    @pl.when(c == nc - 1)
    def _():
        ho_ref[0] = h_ref[...]


def _ssd(xbc, dtr, z, conv0, h0, cw, cb, dtb, alog, dsk, gn, cl, valid, out_dtype):
    b, t, _ = xbc.shape
    nc = t // cl
    rows = lambda w: pl.BlockSpec((1, cl, w), lambda bi, c: (bi, c, 0))
    per_b = lambda shape: pl.BlockSpec((1,) + shape, lambda bi, c: (bi, 0, 0))
    hshape = (SSM_HEADS * SSM_HEAD_DIM, SSM_STATE)
    return pl.pallas_call(
        functools.partial(_ssd_kernel, cl=cl, valid=valid),
        grid=(b, nc),
        in_specs=[rows(CONV_DIM), rows(LANES), rows(SSM_INNER), per_b((CONV_PAD, CONV_DIM)),
                  per_b(hshape), _const(cw.shape), _const(cb.shape), _const(dtb.shape),
                  _const(alog.shape), _const(dsk.shape), _const(gn.shape)],
        out_specs=[rows(SSM_INNER), per_b((CONV_W - 1, CONV_DIM)), per_b(hshape)],
        out_shape=[jax.ShapeDtypeStruct((b, t, SSM_INNER), out_dtype),
                   jax.ShapeDtypeStruct((b, CONV_W - 1, CONV_DIM), F32),
                   jax.ShapeDtypeStruct((b,) + hshape, F32)],
        scratch_shapes=[pltpu.VMEM((CONV_PAD + cl, CONV_DIM), F32), pltpu.VMEM(hshape, F32)],
        compiler_params=_params("parallel", "arbitrary"),
        name="ssd",
    )(xbc, dtr, z, conv0, h0, cw, cb, dtb, alog, dsk, gn)


def _outproj_kernel(x_ref, att_ref, y_ref, wa_ref, wy_ref, g_ref, wq_ref, gq_ref, x1_ref, qx_ref):
    mix = _dot(att_ref[...].astype(BF16), wa_ref[...]) + _dot(y_ref[...].astype(BF16), wy_ref[...])
    x1 = x_ref[...] + mix
    x1_ref[...] = x1
    hx = _rms(x1, g_ref[...]).astype(BF16)
    qx = _dot(hx, wq_ref[...])
    gq = gq_ref[...]
    for h in range(X_HEADS):
        sl = slice(h * X_HEAD_DIM, (h + 1) * X_HEAD_DIM)
        qx_ref[:, sl] = _rms(qx[:, sl], gq).astype(qx_ref.dtype)


def _outproj(x, att, y, w_att, w_y, g, wq, gq, tm, act_dtype):
    n = x.shape[0]
    row = lambda w: pl.BlockSpec((tm, w), lambda i: (i, 0))
    return pl.pallas_call(
        _outproj_kernel,
        grid=(n // tm,),
        in_specs=[row(D_MODEL), row(ATT_WIDTH), row(SSM_INNER), _const(w_att.shape), _const(w_y.shape),
                  _const(g.shape), _const(wq.shape), _const(gq.shape)],
        out_specs=[row(D_MODEL), row(X_WIDTH)],
        out_shape=[jax.ShapeDtypeStruct((n, D_MODEL), F32), jax.ShapeDtypeStruct((n, X_WIDTH), act_dtype)],
        compiler_params=_params("parallel"),
        name="outproj",
    )(x, att, y, w_att, w_y, g, wq, gq)


def _xattn_kernel(q_ref, k_ref, v_ref, o_ref):
    scale = X_HEAD_DIM ** -0.5
    for h in range(X_HEADS):
        sl = slice(h * X_HEAD_DIM, (h + 1) * X_HEAD_DIM)
        q = (q_ref[0, :, sl].astype(F32) * scale).astype(BF16)
        s = _dot_nt(q, k_ref[0, :, sl].astype(BF16))
        m = jnp.max(s, axis=-1, keepdims=True)
        p = jnp.exp(s - m)
        l = jnp.sum(p, axis=-1, keepdims=True)
        o = _dot(p.astype(BF16), v_ref[0, :, sl].astype(BF16)) / l
        o_ref[0, :, sl] = o.astype(o_ref.dtype)


def _xattn(q, mk, mv, tq, act_dtype):
    b, t, _ = q.shape
    m = mk.shape[1]
    qspec = pl.BlockSpec((1, tq, X_WIDTH), lambda bi, i: (bi, i, 0))
    kvspec = pl.BlockSpec((1, m, X_WIDTH), lambda bi, i: (bi, 0, 0))
    return pl.pallas_call(
        _xattn_kernel,
        grid=(b, t // tq),
        in_specs=[qspec, kvspec, kvspec],
        out_specs=qspec,
        out_shape=jax.ShapeDtypeStruct((b, t, X_WIDTH), act_dtype),
        compiler_params=_params("parallel", "parallel"),
        name="xattn",
    )(q, mk, mv)


def _mlp_kernel(x1_ref, o_ref, wo_ref, g_ref, wu_ref, wd_ref, out_ref, *, ff_chunk):
    x2 = x1_ref[...] + _dot(o_ref[...].astype(BF16), wo_ref[...])
    hm = _rms(x2, g_ref[...]).astype(BF16)
    acc = x2
    for cidx in range(D_FF // ff_chunk):
        sl = slice(cidx * ff_chunk, (cidx + 1) * ff_chunk)
        up = jnp.maximum(_dot(hm, wu_ref[:, sl]), 0.0)
        acc = acc + _dot((up * up).astype(BF16), wd_ref[sl, :])
    out_ref[...] = acc


def _mlp(x1, o, wo, g, wu, wd, tm):
    n = x1.shape[0]
    row = lambda w: pl.BlockSpec((tm, w), lambda i: (i, 0))
    return pl.pallas_call(
        functools.partial(_mlp_kernel, ff_chunk=1024),
        grid=(n // tm,),
        in_specs=[row(D_MODEL), row(X_WIDTH), _const_single(wo.shape), _const(g.shape),
                  _const_single(wu.shape), _const_single(wd.shape)],
        out_specs=row(D_MODEL),
        out_shape=jax.ShapeDtypeStruct((n, D_MODEL), F32),
        compiler_params=_params("parallel"),
        name="mlp",
    )(x1, o, wo, g, wu, wd)


def _memkv_kernel(m_ref, g_ref, wk_ref, wv_ref, gk_ref, mk_ref, mv_ref):
    mn = _rms(m_ref[...], g_ref[...]).astype(BF16)
    mk = _dot(mn, wk_ref[...])
    gk = gk_ref[...]
    for h in range(X_HEADS):
        sl = slice(h * X_HEAD_DIM, (h + 1) * X_HEAD_DIM)
        mk_ref[:, sl] = _rms(mk[:, sl], gk)
    mv_ref[...] = _dot(mn, wv_ref[...])


def _memkv(mem, g, wk, wv, gk, tm):
    n = mem.shape[0]
    row = lambda w: pl.BlockSpec((tm, w), lambda i: (i, 0))
    return pl.pallas_call(
        _memkv_kernel,
        grid=(n // tm,),
        in_specs=[row(D_MODEL), _const(g.shape), _const(wk.shape), _const(wv.shape), _const(gk.shape)],
        out_specs=[row(X_WIDTH), row(X_WIDTH)],
        out_shape=[jax.ShapeDtypeStruct((n, X_WIDTH), F32)] * 2,
        compiler_params=_params("parallel"),
        name="memkv",
    )(mem, g, wk, wv, gk)


def _rope_tables(pos):
    half = HEAD_DIM // 2
    inv_freq = ROPE_THETA ** (-jnp.arange(half, dtype=F32) / half)
    ang = pos.astype(F32)[:, None] * inv_freq[None, :]
    cos = jnp.cos(ang)
    sin = jnp.sin(ang)
    cos_h = jnp.concatenate([cos, cos], axis=-1)
    sin_h = jnp.concatenate([-sin, sin], axis=-1)
    return jnp.tile(cos_h, (1, ATT_HEADS)), jnp.tile(sin_h, (1, ATT_HEADS))


def _row_tile(n, want):
    tm = min(n, want)
    assert n % tm == 0
    return tm


def _layer(x, cos, sin, conv0, h0, mem_k, mem_v, wts, attend, tm, act_dtype):
    b, t, _ = x.shape
    n = b * t
    xf = x.reshape(n, D_MODEL)
    q, k, v, z, xbc, dtr = _inproj(xf, wts["ln_mix_g"], wts["w_main"], wts["w_dt"], wts["q_norm_g"],
                                   wts["k_norm_g"], cos, sin, wts["seg"], tm)
    k3 = k.reshape(b, t, ATT_WIDTH)
    v3 = v.reshape(b, t, ATT_WIDTH)
    att = attend(q.reshape(b, t, ATT_WIDTH), k3, v3)

    cl = SSD_CHUNK
    tp = -(-t // cl) * cl
    pad3 = lambda a: jnp.pad(a.reshape(b, t, -1), ((0, 0), (0, tp - t), (0, 0)))
    y, new_conv, new_h = _ssd(pad3(xbc), pad3(dtr), pad3(z), conv0, h0, wts["conv_w"], wts["conv_b"],
                              wts["dt_bias"], wts["a_log"], wts["d_skip"], wts["ssm_norm_g"],
                              cl, min(t, cl) if tp != t else cl, act_dtype)
    y = y[:, :t].reshape(n, SSM_INNER)

    x1, qx = _outproj(xf, att.reshape(n, ATT_WIDTH), y, wts["w_out_att"], wts["w_out_ssm"], wts["ln_x_g"],
                      wts["wq_x"], wts["qx_norm_g"], tm, act_dtype)
    tq = _row_tile(t, 512)
    o = _xattn(qx.reshape(b, t, X_WIDTH), mem_k, mem_v, tq, act_dtype)
    out = _mlp(x1, o.reshape(n, X_WIDTH), wts["wo_x"], wts["ln_mlp_g"], wts["w_up"], wts["w_down"], tm)
    return out.reshape(b, t, D_MODEL), k3, v3, new_conv, new_h


def kernel(x_prompt, x_sample, mem_prompt, cache_k, cache_v, page_table, state_conv, state_ssm,
           cache_mem_k, cache_mem_v, ln_mix_g, w_in, q_norm_g, k_norm_g, conv_w, conv_b, dt_bias,
           a_log, d_skip, ssm_norm_g, w_out, ln_x_g, ln_mem_g, wq_x, wk_x, wv_x, qx_norm_g,
           kx_norm_g, wo_x, ln_mlp_g, w_up, w_down):
    depth = w_in.shape[0]
    assert depth == 1, "single-layer step"
    bp, tp, _ = x_prompt.shape
    bs, ts, _ = x_sample.shape
    n_pool, page = cache_k.shape[1], cache_k.shape[2]
    past_len = page_table.shape[1] * page
    assert tp % MOBA_BLOCK == 0 and past_len % MOBA_BLOCK == 0 and MOBA_BLOCK % page == 0
    assert ts <= MOBA_BLOCK and ts <= SSD_CHUNK and ts >= CONV_W - 1
    l = 0
    main_cols = 3 * ATT_WIDTH + SSM_INNER + CONV_DIM

    def lanes(vec, width=LANES):
        return jnp.pad(vec.astype(F32), (0, width - vec.shape[0])).reshape(1, width)

    seg = jnp.kron(jnp.eye(ATT_HEADS, dtype=F32), jnp.full((HEAD_DIM, HEAD_DIM), 1.0 / HEAD_DIM, F32))
    wts = {
        "ln_mix_g": ln_mix_g[l].reshape(1, D_MODEL),
        "w_main": w_in[l][:, :main_cols].astype(BF16),
        "w_dt": jnp.pad(w_in[l][:, main_cols:], ((0, 0), (0, LANES - SSM_HEADS))).astype(BF16),
        "q_norm_g": jnp.tile(q_norm_g[l], ATT_HEADS).reshape(1, ATT_WIDTH),
        "k_norm_g": jnp.tile(k_norm_g[l], ATT_HEADS).reshape(1, ATT_WIDTH),
        "seg": seg.astype(BF16),
        "conv_w": conv_w[l],
        "conv_b": conv_b[l].reshape(1, CONV_DIM),
        "dt_bias": lanes(dt_bias[l]),
        "a_log": lanes(a_log[l]),
        "d_skip": jnp.repeat(d_skip[l].astype(F32), SSM_HEAD_DIM).reshape(1, SSM_INNER),
        "ssm_norm_g": ssm_norm_g[l].reshape(1, SSM_INNER),
        "w_out_att": w_out[l][:ATT_WIDTH].astype(BF16),
        "w_out_ssm": w_out[l][ATT_WIDTH:].astype(BF16),
        "ln_x_g": ln_x_g[l].reshape(1, D_MODEL),
        "wq_x": wq_x[l].astype(BF16),
        "qx_norm_g": qx_norm_g[l].reshape(1, X_HEAD_DIM),
        "wo_x": wo_x[l].astype(BF16),
        "ln_mlp_g": ln_mlp_g[l].reshape(1, D_MODEL),
        "w_up": w_up[l].astype(BF16),
        "w_down": w_down[l].astype(BF16),
    }

    mem_rows = mem_prompt.reshape(-1, D_MODEL)
    mk_p, mv_p = _memkv(mem_rows, ln_mem_g[l].reshape(1, D_MODEL), wk_x[l].astype(BF16),
                        wv_x[l].astype(BF16), kx_norm_g[l].reshape(1, X_HEAD_DIM),
                        _row_tile(mem_rows.shape[0], 512))
    mem_len = mem_prompt.shape[1]
    mk_p = mk_p.reshape(bp, mem_len, X_WIDTH)
    mv_p = mv_p.reshape(bp, mem_len, X_WIDTH)
    cos_p, sin_p = _rope_tables(jnp.arange(tp, dtype=jnp.int32))
    conv0_p = jnp.zeros((bp, CONV_PAD, CONV_DIM), F32)
    h0_p = jnp.zeros((bp, SSM_HEADS * SSM_HEAD_DIM, SSM_STATE), F32)
    tm_p = _row_tile(tp, 512)
    y_p, k_p, v_p, conv_p, ssm_p = _layer(x_prompt, cos_p, sin_p, conv0_p, h0_p, mk_p, mv_p, wts,
                                          _moba_prompt, tm_p, BF16)

    cos_s, sin_s = _rope_tables(past_len + jnp.arange(ts, dtype=jnp.int32))
    cos_s = jnp.tile(cos_s, (bs, 1))
    sin_s = jnp.tile(sin_s, (bs, 1))
    conv0_s = jnp.pad(state_conv[l], ((0, 0), (CONV_PAD - (CONV_W - 1), 0), (0, 0)))
    h0_s = state_ssm[l].reshape(bs, SSM_HEADS * SSM_HEAD_DIM, SSM_STATE)

    def attend_sample(q, k_new, v_new):
        split = lambda a: a.reshape(bs, ts, ATT_HEADS, HEAD_DIM)
        q_rows = split(q).transpose(0, 2, 1, 3).reshape(bs, ATT_HEADS * ts, HEAD_DIM)
        o = _sample_moba(page_table, q_rows, split(k_new), split(v_new), cache_k, cache_v)
        return o.reshape(bs, ATT_HEADS, ts, HEAD_DIM).transpose(0, 2, 1, 3).reshape(bs, ts, ATT_WIDTH)

    y_s, k_s, v_s, conv_s, ssm_s = _layer(
        x_sample, cos_s, sin_s, conv0_s, h0_s, cache_mem_k[l].reshape(bs, -1, X_WIDTH),
        cache_mem_v[l].reshape(bs, -1, X_WIDTH), wts, attend_sample, bs * ts, F32)

    heads = lambda a, b, t: a.reshape(1, b, t, ATT_HEADS, HEAD_DIM)
    state = lambda a, b: a.reshape(1, b, SSM_HEADS, SSM_HEAD_DIM, SSM_STATE)
    return (y_p, y_s, heads(k_p, bp, tp), heads(v_p, bp, tp), conv_p[None], state(ssm_p, bp),
            mk_p.reshape(1, bp, mem_len, X_HEADS, X_HEAD_DIM), mv_p.reshape(1, bp, mem_len, X_HEADS, X_HEAD_DIM),
            heads(k_s, bs, ts), heads(v_s, bs, ts), conv_s[None], state(ssm_s, bs))
```

```python
import functools

import jax
import jax.numpy as jnp
from jax import lax
from jax.experimental import pallas as pl
from jax.experimental.pallas import tpu as pltpu

F32 = jnp.float32
BF16 = jnp.bfloat16

D_MODEL = 1024
ATT_HEADS = 8
HEAD_DIM = 64
ATT_WIDTH = ATT_HEADS * HEAD_DIM
MOBA_BLOCK = 256
TOPK = 3
ROPE_THETA = 10000.0
SSM_HEADS = 8
SSM_HEAD_DIM = 64
SSM_INNER = SSM_HEADS * SSM_HEAD_DIM
SSM_GROUPS = 2
SSM_STATE = 128
CONV_W = 4
CONV_DIM = SSM_INNER + 2 * SSM_GROUPS * SSM_STATE
SSD_CHUNK = 128
X_HEADS = 4
X_HEAD_DIM = 128
X_WIDTH = X_HEADS * X_HEAD_DIM
D_FF = 4 * D_MODEL
EPS = 1e-6

LANES = 128
CONV_PAD = 8
VMEM_LIMIT = 56 * 1024 * 1024
NEG = -1e30
LOG2E = 1.4426950408889634
PAGES_PER_STEP = 8


def _dot(a, b):
    return jnp.dot(a, b, preferred_element_type=F32)


def _dot_nt(a, b):
    return lax.dot_general(a, b, (((1,), (1,)), ((), ())), preferred_element_type=F32)


def _dot_tn(a, b):
    return lax.dot_general(a, b, (((0,), (0,)), ((), ())), preferred_element_type=F32)


def _split3(x):
    hi = x.astype(BF16)
    r = x - hi.astype(F32)
    mid = r.astype(BF16)
    lo = (r - mid.astype(F32)).astype(BF16)
    return hi, mid, lo


def _dot_f32(a, b, dot=None):
    dot = dot or _dot
    a0, a1, a2 = _split3(a)
    b0, b1, b2 = _split3(b)
    return (dot(a0, b0) + (dot(a0, b1) + dot(a1, b0)) + (dot(a1, b1) + dot(a0, b2) + dot(a2, b0)))


def _rms(x, g):
    ms = jnp.mean(x * x, axis=-1, keepdims=True)
    return x * lax.rsqrt(ms + EPS) * g


def _sigmoid(x):
    return 1.0 / (1.0 + jnp.exp(-x))


def _params(*sem):
    return pltpu.CompilerParams(dimension_semantics=sem, vmem_limit_bytes=VMEM_LIMIT)


def _const(shape):
    return pl.BlockSpec(shape, lambda *_: (0,) * len(shape))


def _const_single(shape):
    return pl.BlockSpec(shape, lambda *_: (0,) * len(shape), pipeline_mode=pl.Buffered(1))


def _inproj_kernel(x_ref, g_ref, wqkv_ref, wrest_ref, wdt_ref, gq_ref, gk_ref, cos_ref, sin_ref,
                   q_ref, k_ref, v_ref, z_ref, xbc_ref, dt_ref):
    hn = _rms(x_ref[...], g_ref[...]).astype(BF16)
    tm = hn.shape[0]
    cos = cos_ref[...]
    sin = sin_ref[...]
    half = HEAD_DIM // 2

    def proj_t(idx):
        return _dot_nt(wqkv_ref[idx * ATT_WIDTH:(idx + 1) * ATT_WIDTH, :], hn)

    def head_norm_rot(p, g_ref_):
        p3 = p.reshape(ATT_HEADS, HEAD_DIM, tm)
        gain = g_ref_[:, 0:1]
        pn = p3 * lax.rsqrt(jnp.mean(p3 * p3, axis=1, keepdims=True) + EPS) * gain
        x1 = pn[:, :half]
        x2 = pn[:, half:]
        rot = jnp.concatenate([x1 * cos - x2 * sin, x2 * cos + x1 * sin], axis=1)
        return rot.reshape(ATT_WIDTH, tm)

    q_ref[0] = head_norm_rot(proj_t(0), gq_ref)
    k_ref[0] = head_norm_rot(proj_t(1), gk_ref)
    v_ref[0] = proj_t(2)
    z_ref[...] = _dot(hn, wrest_ref[:, 0:SSM_INNER])
    xbc_ref[...] = _dot(hn, wrest_ref[:, SSM_INNER:SSM_INNER + CONV_DIM])
    dt_ref[...] = _dot(hn, wdt_ref[...])


def _inproj(x, nb, g, w_qkv_t, w_rest, w_dt, gq, gk, cos_t, sin_t, tm):
    n = x.shape[0]
    cols = n // nb
    tpb = cols // tm
    row = lambda w: pl.BlockSpec((tm, w), lambda i: (i, 0))
    tab = pl.BlockSpec((HEAD_DIM // 2, tm), lambda i: (0, i % tpb))
    tspec = pl.BlockSpec((1, ATT_WIDTH, tm), lambda i: (i // tpb, 0, i % tpb))
    outs = ([jax.ShapeDtypeStruct((nb, ATT_WIDTH, cols), F32)] * 3
            + [jax.ShapeDtypeStruct((n, w), F32) for w in (SSM_INNER, CONV_DIM, LANES)])
    return pl.pallas_call(
        _inproj_kernel,
        grid=(n // tm,),
        in_specs=[row(D_MODEL), _const(g.shape), _const_single(w_qkv_t.shape), _const_single(w_rest.shape),
                  _const(w_dt.shape), _const(gq.shape), _const(gk.shape), tab, tab],
        out_specs=[tspec, tspec, tspec, row(SSM_INNER), row(CONV_DIM), row(LANES)],
        out_shape=outs,
        compiler_params=_params("parallel"),
        name="inproj",
    )(x, g, w_qkv_t, w_rest, w_dt, gq, gk, cos_t, sin_t)


def _topk_keep(gate, n_valid, axis):
    idx = lax.broadcasted_iota(jnp.int32, gate.shape, axis)
    cnt = jnp.zeros(gate.shape, jnp.int32)
    for jp in range(gate.shape[axis]):
        gj = lax.slice_in_dim(gate, jp, jp + 1, axis=axis)
        beats = (gj > gate) | ((gj == gate) & (jp < idx))
        cnt = cnt + jnp.where(beats, jnp.where(jp < n_valid, 1, 0), 0)
    return (idx < n_valid) & (cnt < TOPK)


def _moba_kernel(qt_ref, kt_ref, vt_ref, o_ref, kb_ref, vb_ref, km_ref, *, nblk):
    bq = MOBA_BLOCK
    km_ref[...] = jnp.zeros(km_ref.shape, F32)
    for j in range(nblk):
        kj = kt_ref[0, :, j * bq:(j + 1) * bq].T
        kb_ref[j] = kj.astype(BF16)
        km_ref[j:j + 1, :] = jnp.mean(kj, axis=0, keepdims=True)
        vb_ref[j] = vt_ref[0, :, j * bq:(j + 1) * bq].astype(BF16)
    kmean = km_ref[...]

    keyi = lax.broadcasted_iota(jnp.int32, (bq, bq), 0)
    qryi = lax.broadcasted_iota(jnp.int32, (bq, bq), 1)
    causal = jnp.where(keyi <= qryi, 0.0, NEG)
    row_lo = lax.broadcasted_iota(jnp.int32, (LANES, bq), 0) < HEAD_DIM
    qscale = (HEAD_DIM ** -0.5) * LOG2E

    for own in range(nblk):
        qt = qt_ref[0, :, own * bq:(own + 1) * bq]
        outs = []
        for h in range(2):
            qh = jnp.where(row_lo, 0.0, qt) if h else jnp.where(row_lo, qt, 0.0)
            qsb = (qh * qscale).astype(BF16)
            bias = None
            if own > TOPK:
                keep = _topk_keep(_dot_f32(kmean, qh), own, 0)
                bias = jnp.where(keep, 0.0, NEG)
            tiles = []
            m = None
            for j in range(own + 1):
                s = _dot(kb_ref[j], qsb)
                if j == own:
                    s = s + causal
                elif bias is not None:
                    s = s + bias[j:j + 1, :]
                tiles.append(s)
                mj = jnp.max(s, axis=0, keepdims=True)
                m = mj if m is None else jnp.maximum(m, mj)
            l = jnp.zeros((1, bq), F32)
            acc = jnp.zeros((HEAD_DIM, bq), F32)
            for j in range(own + 1):
                p = jnp.exp2(tiles[j] - m)
                l = l + jnp.sum(p, axis=0, keepdims=True)
                acc = acc + _dot(vb_ref[j, h * HEAD_DIM:(h + 1) * HEAD_DIM, :], p.astype(BF16))
            outs.append(acc / l)
        o_ref[0, own * bq:(own + 1) * bq, :] = jnp.concatenate(outs, axis=0).T.astype(o_ref.dtype)


def _moba_prompt(qt, kt, vt):
    b, _, t = qt.shape
    nblk = t // MOBA_BLOCK
    npad = max(16, nblk)
    npair = ATT_WIDTH // LANES
    spec = pl.BlockSpec((1, LANES, t), lambda bi, p: (bi, p, 0))
    return pl.pallas_call(
        functools.partial(_moba_kernel, nblk=nblk),
        grid=(b, npair),
        in_specs=[spec, spec, spec],
        out_specs=pl.BlockSpec((1, t, LANES), lambda bi, p: (bi, 0, p)),
        out_shape=jax.ShapeDtypeStruct((b, t, ATT_WIDTH), BF16),
        scratch_shapes=[pltpu.VMEM((nblk, MOBA_BLOCK, LANES), BF16),
                        pltpu.VMEM((nblk, LANES, MOBA_BLOCK), BF16),
                        pltpu.VMEM((npad, LANES), F32)],
        compiler_params=_params("parallel", "parallel"),
        name="moba_prompt",
    )(qt, kt, vt)


def _expand_heads(q):
    t = q.shape[0]
    tiled = jnp.concatenate([q] * ATT_HEADS, axis=0)
    rowh = lax.broadcasted_iota(jnp.int32, tiled.shape, 0) // t
    laneh = lax.broadcasted_iota(jnp.int32, tiled.shape, 1) // HEAD_DIM
    return jnp.where(rowh == laneh, tiled, 0.0)


def _sample_moba_kernel(pt_ref, q_ref, kn_ref, vn_ref, *rest, page, npg, nblk):
    del pt_ref
    k_refs = rest[:npg]
    v_refs = rest[npg:2 * npg]
    o_ref, km_ref, m_ref, l_ref, oblk_ref = rest[2 * npg:]
    c = pl.program_id(1)
    t = q_ref.shape[1]
    rows = ATT_HEADS * t
    pages_per_block = MOBA_BLOCK // page
    blocks_per_step = npg // pages_per_block

    qe = _expand_heads(q_ref[0])
    qb = (qe * ((HEAD_DIM ** -0.5) * LOG2E)).astype(BF16)
    ones = jnp.ones((8, page), BF16)
    for jj in range(blocks_per_step):
        tiles = []
        ksum = None
        m = None
        for u in range(pages_per_block):
            kt = k_refs[jj * pages_per_block + u][...].reshape(ATT_WIDTH, page)
            ksum = kt if ksum is None else ksum + kt
            s = _dot(qb, kt.astype(BF16))
            tiles.append(s)
            mu = jnp.max(s, axis=1, keepdims=True)
            m = mu if m is None else jnp.maximum(m, mu)
        l = jnp.zeros((rows, 1), F32)
        o = jnp.zeros((rows, ATT_WIDTH), F32)
        for u in range(pages_per_block):
            p = jnp.exp2(tiles[u] - m)
            l = l + jnp.sum(p, axis=1, keepdims=True)
            vt = v_refs[jj * pages_per_block + u][...].reshape(ATT_WIDTH, page).astype(BF16)
            o = o + _dot_nt(p.astype(BF16), vt)
        j = c * blocks_per_step + jj
        k0, k1, k2 = _split3(ksum)
        krow = _dot_nt(ones, k0) + _dot_nt(ones, k1) + _dot_nt(ones, k2)
        km_ref[pl.ds(j, 1), :] = krow[0:1, :] * (1.0 / MOBA_BLOCK)
        m_ref[j] = jnp.broadcast_to(m, (rows, LANES))
        l_ref[j] = jnp.broadcast_to(l, (rows, LANES))
        oblk_ref[j] = o

    @pl.when(c == pl.num_programs(1) - 1)
    def _():
        keep = _topk_keep(_dot_f32(qe, km_ref[...], _dot_nt), nblk, 1)
        s_own = _dot_nt(qb, kn_ref[0].astype(BF16))
        qpos = lax.broadcasted_iota(jnp.int32, s_own.shape, 0) % t
        kpos = lax.broadcasted_iota(jnp.int32, s_own.shape, 1)
        s_own = jnp.where(kpos <= qpos, s_own, NEG)
        lane_j = lax.broadcasted_iota(jnp.int32, (rows, nblk), 1)
        m_all = jnp.zeros((rows, nblk), F32)
        l_all = jnp.zeros((rows, nblk), F32)
        for j in range(nblk):
            m_all = jnp.where(lane_j == j, m_ref[j][:, 0:nblk], m_all)
            l_all = jnp.where(lane_j == j, l_ref[j][:, 0:nblk], l_all)
        mx = jnp.maximum(jnp.max(s_own, axis=1, keepdims=True),
                         jnp.max(jnp.where(keep, m_all, NEG), axis=1, keepdims=True))
        p_own = jnp.exp2(s_own - mx)
        w = jnp.where(keep, jnp.exp2(jnp.minimum(m_all - mx, 0.0)), 0.0)
        den = jnp.sum(p_own, axis=1, keepdims=True) + jnp.sum(w * l_all, axis=1, keepdims=True)
        vn = vn_ref[0]
        num = jnp.zeros((rows, ATT_WIDTH), F32)
        for tt in range(t):
            num = num + p_own[:, tt:tt + 1] * vn[tt:tt + 1, :]
        for j in range(nblk):
            num = num + w[:, j:j + 1] * oblk_ref[j]
        full = num / den
        laneh = lax.broadcasted_iota(jnp.int32, (t, ATT_WIDTH), 1) // HEAD_DIM
        out = jnp.zeros((t, ATT_WIDTH), F32)
        for h in range(ATT_HEADS):
            out = out + jnp.where(laneh == h, full[h * t:(h + 1) * t, :], 0.0)
        o_ref[0] = out


def _sample_moba(page_table, q, k_new, v_new, cache_kt, cache_vt):
    b, t, _ = q.shape
    page = cache_kt.shape[4]
    n_pages = page_table.shape[1]
    npg = PAGES_PER_STEP
    nsteps = n_pages // npg
    nblk = n_pages * page // MOBA_BLOCK
    rows = ATT_HEADS * t
    per_b = pl.BlockSpec((1, t, ATT_WIDTH), lambda bi, c, pt: (bi, 0, 0))

    def pspec(pi):
        return pl.BlockSpec((None, None, ATT_HEADS, HEAD_DIM, page),
                            lambda bi, c, pt: (0, pt[bi, c * npg + pi], 0, 0, 0))

    grid_spec = pltpu.PrefetchScalarGridSpec(
        num_scalar_prefetch=1,
        grid=(b, nsteps),
        in_specs=[per_b, per_b, per_b] + [pspec(pi) for pi in range(npg)] * 2,
        out_specs=per_b,
        scratch_shapes=[pltpu.VMEM((nblk, ATT_WIDTH), F32),
                        pltpu.VMEM((nblk, rows, LANES), F32),
                        pltpu.VMEM((nblk, rows, LANES), F32),
                        pltpu.VMEM((nblk, rows, ATT_WIDTH), F32)],
    )
    return pl.pallas_call(
        functools.partial(_sample_moba_kernel, page=page, npg=npg, nblk=nblk),
        grid_spec=grid_spec,
        out_shape=jax.ShapeDtypeStruct((b, t, ATT_WIDTH), F32),
        compiler_params=_params("parallel", "arbitrary"),
        name="sample_moba",
    )(page_table, q, k_new, v_new, *([cache_kt] * npg), *([cache_vt] * npg))


def _cumsum_rows(x):
    n = x.shape[0]
    row = lax.broadcasted_iota(jnp.int32, x.shape, 0)
    step = 1
    while step < n:
        x = x + jnp.where(row >= step, pltpu.roll(x, step, 0), 0.0)
        step *= 2
    return x


def _ssd_kernel(xbc_ref, dtr_ref, z_ref, conv0_ref, h0_ref, cw_ref, cb_ref, dtb_ref, alog_ref,
                dsk_ref, gn_ref, y_ref, convo_ref, ho_ref, xpad_ref, h_ref, *, cl, valid):
    c = pl.program_id(1)
    nc = pl.num_programs(1)

    @pl.when(c == 0)
    def _():
        xpad_ref[0:CONV_PAD, :] = conv0_ref[0]
        h_ref[...] = h0_ref[0]

    x = xbc_ref[0]
    xpad_ref[CONV_PAD:CONV_PAD + cl, :] = x
    w = cw_ref[...]
    conv = cb_ref[...] + x * w[CONV_W - 1:CONV_W, :]
    for back in range(1, CONV_W):
        conv = conv + xpad_ref[pl.ds(CONV_PAD - back, cl), :] * w[CONV_W - 1 - back:CONV_W - back, :]

    @pl.when(c == nc - 1)
    def _():
        convo_ref[0] = xpad_ref[CONV_PAD + valid - (CONV_W - 1):CONV_PAD + valid, :]

    xpad_ref[0:CONV_PAD, :] = xpad_ref[cl:cl + CONV_PAD, :]
    xc = conv * _sigmoid(conv)
    xs = xc[:, :SSM_INNER]
    gw = SSM_STATE
    bm = [xc[:, SSM_INNER + g * gw:SSM_INNER + (g + 1) * gw].astype(BF16) for g in range(SSM_GROUPS)]
    cm = [xc[:, SSM_INNER + (SSM_GROUPS + g) * gw:SSM_INNER + (SSM_GROUPS + g + 1) * gw].astype(BF16)
          for g in range(SSM_GROUPS)]

    pre = dtr_ref[0] + dtb_ref[...]
    dt = jnp.maximum(pre, 0.0) + jnp.log1p(jnp.exp(-jnp.abs(pre)))
    rowl = lax.broadcasted_iota(jnp.int32, dt.shape, 0)
    if valid < cl:
        dt = jnp.where(rowl < valid, dt, 0.0)
    da = dt * (-jnp.exp(alog_ref[...]))
    cum = _cumsum_rows(da)
    cum_t = cum.T
    cum_last = cum[cl - 1:cl, :]
    ecum = jnp.exp(cum)
    tail = jnp.exp(cum_last - cum)
    elast = jnp.exp(cum_last)

    ri = lax.broadcasted_iota(jnp.int32, (cl, cl), 0)
    ci = lax.broadcasted_iota(jnp.int32, (cl, cl), 1)
    causal = ci <= ri
    lane = lax.broadcasted_iota(jnp.int32, (cl, LANES), 1)
    lo = lane < SSM_HEAD_DIM
    row_lo = lax.broadcasted_iota(jnp.int32, (LANES, 1), 0) < SSM_HEAD_DIM
    heads_per_group = SSM_HEADS // SSM_GROUPS
    gmat = [_dot_nt(cm[g], bm[g]) for g in range(SSM_GROUPS)]

    def pick(cols, h0):
        return jnp.where(lo, cols[:, h0:h0 + 1], cols[:, h0 + 1:h0 + 2])

    ys = []
    for p in range(SSM_HEADS // 2):
        h0 = 2 * p
        g = h0 // heads_per_group
        xs_p = xs[:, p * LANES:(p + 1) * LANES]
        xdt = xs_p * pick(dt, h0)
        xdt_b = xdt.astype(BF16)
        parts = []
        for h in (h0, h0 + 1):
            seg = cum[:, h:h + 1] - cum_t[h:h + 1, :]
            dec = jnp.where(causal, jnp.exp(jnp.minimum(seg, 0.0)), 0.0)
            parts.append(_dot((gmat[g] * dec).astype(BF16), xdt_b))
        hp = h_ref[p * LANES:(p + 1) * LANES, :]
        y_p = jnp.where(lo, parts[0], parts[1])
        y_p = y_p + _dot_nt(cm[g], hp.astype(BF16)) * pick(ecum, h0)
        y_p = y_p + dsk_ref[:, p * LANES:(p + 1) * LANES] * xs_p
        xw = (xdt * pick(tail, h0)).astype(BF16)
        decay_rows = jnp.where(row_lo, elast[:, h0:h0 + 1], elast[:, h0 + 1:h0 + 2])
        h_ref[p * LANES:(p + 1) * LANES, :] = hp * decay_rows + _dot_tn(xw, bm[g])
        ys.append(y_p)

    y = jnp.concatenate(ys, axis=1)
    zz = z_ref[0]
    yz = y * (zz * _sigmoid(zz))
    gwid = SSM_INNER // SSM_GROUPS
    normed = []
    for g in range(SSM_GROUPS):
        part = yz[:, g * gwid:(g + 1) * gwid]
        normed.append(part * lax.rsqrt(jnp.mean(part * part, axis=-1, keepdims=True) + EPS))
    y_ref[0] = (jnp.concatenate(normed, axis=1) * gn_ref[...]).astype(y_ref.dtype)

    @pl.when(c == nc - 1)
    def _():
        ho_ref[0] = h_ref[...]


def _ssd(xbc, dtr, z, conv0, h0, cw, cb, dtb, alog, dsk, gn, cl, valid, out_dtype):
    b, t, _ = xbc.shape
    nc = t // cl
    rows = lambda w: pl.BlockSpec((1, cl, w), lambda bi, c: (bi, c, 0))
    per_b = lambda shape: pl.BlockSpec((1,) + shape, lambda bi, c: (bi, 0, 0))
    hshape = (SSM_HEADS * SSM_HEAD_DIM, SSM_STATE)
    return pl.pallas_call(
        functools.partial(_ssd_kernel, cl=cl, valid=valid),
        grid=(b, nc),
        in_specs=[rows(CONV_DIM), rows(LANES), rows(SSM_INNER), per_b((CONV_PAD, CONV_DIM)),
                  per_b(hshape), _const(cw.shape), _const(cb.shape), _const(dtb.shape),
                  _const(alog.shape), _const(dsk.shape), _const(gn.shape)],
        out_specs=[rows(SSM_INNER), per_b((CONV_W - 1, CONV_DIM)), per_b(hshape)],
        out_shape=[jax.ShapeDtypeStruct((b, t, SSM_INNER), out_dtype),
                   jax.ShapeDtypeStruct((b, CONV_W - 1, CONV_DIM), F32),
                   jax.ShapeDtypeStruct((b,) + hshape, F32)],
        scratch_shapes=[pltpu.VMEM((CONV_PAD + cl, CONV_DIM), F32), pltpu.VMEM(hshape, F32)],
        compiler_params=_params("parallel", "arbitrary"),
        name="ssd",
    )(xbc, dtr, z, conv0, h0, cw, cb, dtb, alog, dsk, gn)


def _outproj_kernel(x_ref, att_ref, y_ref, wa_ref, wy_ref, g_ref, wq_ref, gq_ref, x1_ref, qx_ref):
    mix = _dot(att_ref[...].astype(BF16), wa_ref[...]) + _dot(y_ref[...].astype(BF16), wy_ref[...])
    x1 = x_ref[...] + mix
    x1_ref[...] = x1
    hx = _rms(x1, g_ref[...]).astype(BF16)
    qx = _dot(hx, wq_ref[...])
    gq = gq_ref[...]
    for h in range(X_HEADS):
        sl = slice(h * X_HEAD_DIM, (h + 1) * X_HEAD_DIM)
        qx_ref[:, sl] = _rms(qx[:, sl], gq).astype(qx_ref.dtype)


def _outproj(x, att, y, w_att, w_y, g, wq, gq, tm, act_dtype):
    n = x.shape[0]
    row = lambda w: pl.BlockSpec((tm, w), lambda i: (i, 0))
    return pl.pallas_call(
        _outproj_kernel,
        grid=(n // tm,),
        in_specs=[row(D_MODEL), row(ATT_WIDTH), row(SSM_INNER), _const(w_att.shape), _const(w_y.shape),
                  _const(g.shape), _const(wq.shape), _const(gq.shape)],
        out_specs=[row(D_MODEL), row(X_WIDTH)],
        out_shape=[jax.ShapeDtypeStruct((n, D_MODEL), F32), jax.ShapeDtypeStruct((n, X_WIDTH), act_dtype)],
        compiler_params=_params("parallel"),
        name="outproj",
    )(x, att, y, w_att, w_y, g, wq, gq)


def _xattn_kernel(q_ref, k_ref, v_ref, o_ref):
    scale = X_HEAD_DIM ** -0.5
    for h in range(X_HEADS):
        sl = slice(h * X_HEAD_DIM, (h + 1) * X_HEAD_DIM)
        q = (q_ref[0, :, sl].astype(F32) * scale).astype(BF16)
        s = _dot_nt(q, k_ref[0, :, sl].astype(BF16))
        m = jnp.max(s, axis=-1, keepdims=True)
        p = jnp.exp(s - m)
        l = jnp.sum(p, axis=-1, keepdims=True)
        o = _dot(p.astype(BF16), v_ref[0, :, sl].astype(BF16)) / l
        o_ref[0, :, sl] = o.astype(o_ref.dtype)


def _xattn(q, mk, mv, tq, act_dtype):
    b, t, _ = q.shape
    m = mk.shape[1]
    qspec = pl.BlockSpec((1, tq, X_WIDTH), lambda bi, i: (bi, i, 0))
    kvspec = pl.BlockSpec((1, m, X_WIDTH), lambda bi, i: (bi, 0, 0))
    return pl.pallas_call(
        _xattn_kernel,
        grid=(b, t // tq),
        in_specs=[qspec, kvspec, kvspec],
        out_specs=qspec,
        out_shape=jax.ShapeDtypeStruct((b, t, X_WIDTH), act_dtype),
        compiler_params=_params("parallel", "parallel"),
        name="xattn",
    )(q, mk, mv)


def _mlp_kernel(x1_ref, o_ref, wo_ref, g_ref, wu_ref, wd_ref, out_ref, *, ff_chunk):
    x2 = x1_ref[...] + _dot(o_ref[...].astype(BF16), wo_ref[...])
    hm = _rms(x2, g_ref[...]).astype(BF16)
    acc = x2
    for cidx in range(D_FF // ff_chunk):
        sl = slice(cidx * ff_chunk, (cidx + 1) * ff_chunk)
        up = jnp.maximum(_dot(hm, wu_ref[:, sl]), 0.0)
        acc = acc + _dot((up * up).astype(BF16), wd_ref[sl, :])
    out_ref[...] = acc


def _mlp(x1, o, wo, g, wu, wd, tm):
    n = x1.shape[0]
    row = lambda w: pl.BlockSpec((tm, w), lambda i: (i, 0))
    return pl.pallas_call(
        functools.partial(_mlp_kernel, ff_chunk=1024),
        grid=(n // tm,),
        in_specs=[row(D_MODEL), row(X_WIDTH), _const_single(wo.shape), _const(g.shape),
                  _const_single(wu.shape), _const_single(wd.shape)],
        out_specs=row(D_MODEL),
        out_shape=jax.ShapeDtypeStruct((n, D_MODEL), F32),
        compiler_params=_params("parallel"),
        name="mlp",
    )(x1, o, wo, g, wu, wd)


def _memkv_kernel(m_ref, g_ref, wk_ref, wv_ref, gk_ref, mk_ref, mv_ref):
    mn = _rms(m_ref[...], g_ref[...]).astype(BF16)
    mk = _dot(mn, wk_ref[...])
    gk = gk_ref[...]
    for h in range(X_HEADS):
        sl = slice(h * X_HEAD_DIM, (h + 1) * X_HEAD_DIM)
        mk_ref[:, sl] = _rms(mk[:, sl], gk)
    mv_ref[...] = _dot(mn, wv_ref[...])


def _memkv(mem, g, wk, wv, gk, tm):
    n = mem.shape[0]
    row = lambda w: pl.BlockSpec((tm, w), lambda i: (i, 0))
    return pl.pallas_call(
        _memkv_kernel,
        grid=(n // tm,),
        in_specs=[row(D_MODEL), _const(g.shape), _const(wk.shape), _const(wv.shape), _const(gk.shape)],
        out_specs=[row(X_WIDTH), row(X_WIDTH)],
        out_shape=[jax.ShapeDtypeStruct((n, X_WIDTH), F32)] * 2,
        compiler_params=_params("parallel"),
        name="memkv",
    )(mem, g, wk, wv, gk)


def _rope_tables(pos):
    half = HEAD_DIM // 2
    inv_freq = ROPE_THETA ** (-jnp.arange(half, dtype=F32) / half)
    ang = pos.astype(F32)[:, None] * inv_freq[None, :]
    return jnp.cos(ang).T, jnp.sin(ang).T


def _row_tile(n, want):
    tm = min(n, want)
    assert n % tm == 0
    return tm


def _layer(x, nb, cos_t, sin_t, conv0, h0, mem_k, mem_v, wts, attend, tm, act_dtype):
    b, t, _ = x.shape
    n = b * t
    xf = x.reshape(n, D_MODEL)
    qt, kt, vt, z, xbc, dtr = _inproj(xf, nb, wts["ln_mix_g"], wts["w_qkv_t"], wts["w_rest"], wts["w_dt"],
                                      wts["q_norm_g"], wts["k_norm_g"], cos_t, sin_t, tm)
    att, extras = attend(qt, kt, vt)

    cl = SSD_CHUNK
    tp = -(-t // cl) * cl
    pad3 = lambda a: jnp.pad(a.reshape(b, t, -1), ((0, 0), (0, tp - t), (0, 0)))
    y, new_conv, new_h = _ssd(pad3(xbc), pad3(dtr), pad3(z), conv0, h0, wts["conv_w"], wts["conv_b"],
                              wts["dt_bias"], wts["a_log"], wts["d_skip"], wts["ssm_norm_g"],
                              cl, min(t, cl) if tp != t else cl, act_dtype)
    y = y[:, :t].reshape(n, SSM_INNER)

    x1, qx = _outproj(xf, att, y, wts["w_out_att"], wts["w_out_ssm"], wts["ln_x_g"],
                      wts["wq_x"], wts["qx_norm_g"], tm, act_dtype)
    tq = _row_tile(t, 512)
    o = _xattn(qx.reshape(b, t, X_WIDTH), mem_k, mem_v, tq, act_dtype)
    out = _mlp(x1, o.reshape(n, X_WIDTH), wts["wo_x"], wts["ln_mlp_g"], wts["w_up"], wts["w_down"], tm)
    return out.reshape(b, t, D_MODEL), extras, new_conv, new_h


def kernel(x_prompt, x_sample, mem_prompt, cache_k, cache_v, page_table, state_conv, state_ssm,
           cache_mem_k, cache_mem_v, ln_mix_g, w_in, q_norm_g, k_norm_g, conv_w, conv_b, dt_bias,
           a_log, d_skip, ssm_norm_g, w_out, ln_x_g, ln_mem_g, wq_x, wk_x, wv_x, qx_norm_g,
           kx_norm_g, wo_x, ln_mlp_g, w_up, w_down):
    depth = w_in.shape[0]
    assert depth == 1, "single-layer step"
    bp, tp, _ = x_prompt.shape
    bs, ts, _ = x_sample.shape
    page = cache_k.shape[2]
    past_len = page_table.shape[1] * page
    assert tp % MOBA_BLOCK == 0 and past_len % MOBA_BLOCK == 0 and MOBA_BLOCK % page == 0
    assert ts <= MOBA_BLOCK and ts <= SSD_CHUNK and ts >= CONV_W - 1
    l = 0
    qkv_cols = 3 * ATT_WIDTH
    main_cols = qkv_cols + SSM_INNER + CONV_DIM

    def lanes(vec, width=LANES):
        return jnp.pad(vec.astype(F32), (0, width - vec.shape[0])).reshape(1, width)

    def dim_gain(vec):
        return jnp.broadcast_to(vec.astype(F32)[:, None], (HEAD_DIM, LANES))

    wts = {
        "ln_mix_g": ln_mix_g[l].reshape(1, D_MODEL),
        "w_qkv_t": w_in[l][:, :qkv_cols].T.astype(BF16),
        "w_rest": w_in[l][:, qkv_cols:main_cols].astype(BF16),
        "w_dt": jnp.pad(w_in[l][:, main_cols:], ((0, 0), (0, LANES - SSM_HEADS))).astype(BF16),
        "q_norm_g": dim_gain(q_norm_g[l]),
        "k_norm_g": dim_gain(k_norm_g[l]),
        "conv_w": conv_w[l],
        "conv_b": conv_b[l].reshape(1, CONV_DIM),
        "dt_bias": lanes(dt_bias[l]),
        "a_log": lanes(a_log[l]),
        "d_skip": jnp.repeat(d_skip[l].astype(F32), SSM_HEAD_DIM).reshape(1, SSM_INNER),
        "ssm_norm_g": ssm_norm_g[l].reshape(1, SSM_INNER),
        "w_out_att": w_out[l][:ATT_WIDTH].astype(BF16),
        "w_out_ssm": w_out[l][ATT_WIDTH:].astype(BF16),
        "ln_x_g": ln_x_g[l].reshape(1, D_MODEL),
        "wq_x": wq_x[l].astype(BF16),
        "qx_norm_g": qx_norm_g[l].reshape(1, X_HEAD_DIM),
        "wo_x": wo_x[l].astype(BF16),
        "ln_mlp_g": ln_mlp_g[l].reshape(1, D_MODEL),
        "w_up": w_up[l].astype(BF16),
        "w_down": w_down[l].astype(BF16),
    }

    mem_rows = mem_prompt.reshape(-1, D_MODEL)
    mk_p, mv_p = _memkv(mem_rows, ln_mem_g[l].reshape(1, D_MODEL), wk_x[l].astype(BF16),
                        wv_x[l].astype(BF16), kx_norm_g[l].reshape(1, X_HEAD_DIM),
                        _row_tile(mem_rows.shape[0], 512))
    mem_len = mem_prompt.shape[1]
    mk_p = mk_p.reshape(bp, mem_len, X_WIDTH)
    mv_p = mv_p.reshape(bp, mem_len, X_WIDTH)
    cos_p, sin_p = _rope_tables(jnp.arange(tp, dtype=jnp.int32))
    conv0_p = jnp.zeros((bp, CONV_PAD, CONV_DIM), F32)
    h0_p = jnp.zeros((bp, SSM_HEADS * SSM_HEAD_DIM, SSM_STATE), F32)

    def attend_prompt(qt, kt, vt):
        heads_t = lambda a: a.reshape(1, bp, ATT_HEADS, HEAD_DIM, tp).transpose(0, 1, 4, 2, 3)
        return _moba_prompt(qt, kt, vt).reshape(bp * tp, ATT_WIDTH), (heads_t(kt), heads_t(vt))

    y_p, (k_p, v_p), conv_p, ssm_p = _layer(x_prompt, bp, cos_p, sin_p, conv0_p, h0_p, mk_p, mv_p, wts,
                                            attend_prompt, _row_tile(tp, 512), BF16)

    cos_s, sin_s = _rope_tables(past_len + jnp.arange(ts, dtype=jnp.int32))
    cos_s = jnp.tile(cos_s, (1, bs))
    sin_s = jnp.tile(sin_s, (1, bs))
    conv0_s = jnp.pad(state_conv[l], ((0, 0), (CONV_PAD - (CONV_W - 1), 0), (0, 0)))
    h0_s = state_ssm[l].reshape(bs, SSM_HEADS * SSM_HEAD_DIM, SSM_STATE)
    cache_kt = cache_k.transpose(0, 1, 3, 4, 2)
    cache_vt = cache_v.transpose(0, 1, 3, 4, 2)

    def attend_sample(qt, kt, vt):
        rows = lambda a: a[0].T.reshape(bs, ts, ATT_WIDTH)
        k_new, v_new = rows(kt), rows(vt)
        o = _sample_moba(page_table, rows(qt), k_new, v_new, cache_kt, cache_vt)
        heads = lambda a: a.reshape(1, bs, ts, ATT_HEADS, HEAD_DIM)
        return o.reshape(bs * ts, ATT_WIDTH), (heads(k_new), heads(v_new))

    y_s, (k_s, v_s), conv_s, ssm_s = _layer(
        x_sample, 1, cos_s, sin_s, conv0_s, h0_s, cache_mem_k[l].reshape(bs, -1, X_WIDTH),
        cache_mem_v[l].reshape(bs, -1, X_WIDTH), wts, attend_sample, bs * ts, F32)

    state = lambda a, b: a.reshape(1, b, SSM_HEADS, SSM_HEAD_DIM, SSM_STATE)
    return (y_p, y_s, k_p, v_p, conv_p[None], state(ssm_p, bp),
            mk_p.reshape(1, bp, mem_len, X_HEADS, X_HEAD_DIM), mv_p.reshape(1, bp, mem_len, X_HEADS, X_HEAD_DIM),
            k_s, v_s, conv_s[None], state(ssm_s, bs))
```

```python
import functools

import jax
import jax.numpy as jnp
from jax import lax
from jax.experimental import pallas as pl
from jax.experimental.pallas import tpu as pltpu

F32 = jnp.float32
BF16 = jnp.bfloat16

D_MODEL = 1024
ATT_HEADS = 8
HEAD_DIM = 64
ATT_WIDTH = ATT_HEADS * HEAD_DIM
MOBA_BLOCK = 256
TOPK = 3
ROPE_THETA = 10000.0
SSM_HEADS = 8
SSM_HEAD_DIM = 64
SSM_INNER = SSM_HEADS * SSM_HEAD_DIM
SSM_GROUPS = 2
SSM_STATE = 128
CONV_W = 4
CONV_DIM = SSM_INNER + 2 * SSM_GROUPS * SSM_STATE
SSD_CHUNK = 128
X_HEADS = 4
X_HEAD_DIM = 128
X_WIDTH = X_HEADS * X_HEAD_DIM
D_FF = 4 * D_MODEL
EPS = 1e-6

LANES = 128
CONV_PAD = 8
VMEM_LIMIT = 56 * 1024 * 1024
NEG = -1e30
LOG2E = 1.4426950408889634
SSD_SEQS_PER_STEP = 4
PAGES_PER_STEP = 16


def _dot(a, b):
    return jnp.dot(a, b, preferred_element_type=F32)


def _dot_nt(a, b):
    return lax.dot_general(a, b, (((1,), (1,)), ((), ())), preferred_element_type=F32)


def _dot_tn(a, b):
    return lax.dot_general(a, b, (((0,), (0,)), ((), ())), preferred_element_type=F32)


def _split3(x):
    hi = x.astype(BF16)
    r = x - hi.astype(F32)
    mid = r.astype(BF16)
    lo = (r - mid.astype(F32)).astype(BF16)
    return hi, mid, lo


def _dot_f32(a, b, dot=None):
    dot = dot or _dot
    a0, a1, a2 = _split3(a)
    b0, b1, b2 = _split3(b)
    return (dot(a0, b0) + (dot(a0, b1) + dot(a1, b0)) + (dot(a1, b1) + dot(a0, b2) + dot(a2, b0)))


def _rms(x, g):
    ms = jnp.mean(x * x, axis=-1, keepdims=True)
    return x * lax.rsqrt(ms + EPS) * g


def _sigmoid(x):
    return 1.0 / (1.0 + jnp.exp(-x))


def _params(*sem):
    return pltpu.CompilerParams(dimension_semantics=sem, vmem_limit_bytes=VMEM_LIMIT)


def _const(shape):
    return pl.BlockSpec(shape, lambda *_: (0,) * len(shape))


def _const_single(shape):
    return pl.BlockSpec(shape, lambda *_: (0,) * len(shape), pipeline_mode=pl.Buffered(1))


def _inproj_kernel(x_ref, g_ref, wqkv_ref, wrest_ref, wdt_ref, gq_ref, gk_ref, cos_ref, sin_ref,
                   q_ref, k_ref, v_ref, z_ref, xbc_ref, dt_ref):
    hn = _rms(x_ref[...], g_ref[...]).astype(BF16)
    tm = hn.shape[0]
    cos = cos_ref[...]
    sin = sin_ref[...]
    half = HEAD_DIM // 2

    def proj_t(idx):
        return _dot_nt(wqkv_ref[idx * ATT_WIDTH:(idx + 1) * ATT_WIDTH, :], hn)

    def head_norm_rot(p, g_ref_):
        p3 = p.reshape(ATT_HEADS, HEAD_DIM, tm)
        gain = g_ref_[:, 0:1]
        pn = p3 * lax.rsqrt(jnp.mean(p3 * p3, axis=1, keepdims=True) + EPS) * gain
        x1 = pn[:, :half]
        x2 = pn[:, half:]
        rot = jnp.concatenate([x1 * cos - x2 * sin, x2 * cos + x1 * sin], axis=1)
        return rot.reshape(ATT_WIDTH, tm)

    q_ref[0] = head_norm_rot(proj_t(0), gq_ref)
    k_ref[0] = head_norm_rot(proj_t(1), gk_ref)
    v_ref[0] = proj_t(2)
    z_ref[...] = _dot(hn, wrest_ref[:, 0:SSM_INNER])
    xbc_ref[...] = _dot(hn, wrest_ref[:, SSM_INNER:SSM_INNER + CONV_DIM])
    dt_ref[...] = _dot(hn, wdt_ref[...])


def _inproj(x, nb, g, w_qkv_t, w_rest, w_dt, gq, gk, cos_t, sin_t, tm):
    n = x.shape[0]
    cols = n // nb
    tpb = cols // tm
    row = lambda w: pl.BlockSpec((tm, w), lambda i: (i, 0))
    tab = pl.BlockSpec((HEAD_DIM // 2, tm), lambda i: (0, i % tpb))
    tspec = pl.BlockSpec((1, ATT_WIDTH, tm), lambda i: (i // tpb, 0, i % tpb))
    outs = ([jax.ShapeDtypeStruct((nb, ATT_WIDTH, cols), F32)] * 3
            + [jax.ShapeDtypeStruct((n, w), F32) for w in (SSM_INNER, CONV_DIM, LANES)])
    return pl.pallas_call(
        _inproj_kernel,
        grid=(n // tm,),
        in_specs=[row(D_MODEL), _const(g.shape), _const_single(w_qkv_t.shape), _const_single(w_rest.shape),
                  _const(w_dt.shape), _const(gq.shape), _const(gk.shape), tab, tab],
        out_specs=[tspec, tspec, tspec, row(SSM_INNER), row(CONV_DIM), row(LANES)],
        out_shape=outs,
        compiler_params=_params("parallel"),
        name="inproj",
    )(x, g, w_qkv_t, w_rest, w_dt, gq, gk, cos_t, sin_t)


def _topk_keep(gate, n_valid, axis):
    idx = lax.broadcasted_iota(jnp.int32, gate.shape, axis)
    cnt = jnp.zeros(gate.shape, jnp.int32)
    for jp in range(gate.shape[axis]):
        gj = lax.slice_in_dim(gate, jp, jp + 1, axis=axis)
        beats = (gj > gate) | ((gj == gate) & (jp < idx))
        cnt = cnt + jnp.where(beats, jnp.where(jp < n_valid, 1, 0), 0)
    return (idx < n_valid) & (cnt < TOPK)


def _moba_kernel(qt_ref, kt_ref, vt_ref, o_ref, kb_ref, vb_ref, km_ref, *, nblk):
    bq = MOBA_BLOCK
    km_ref[...] = jnp.zeros(km_ref.shape, F32)
    for j in range(nblk):
        kj = kt_ref[0, :, j * bq:(j + 1) * bq].T
        kb_ref[j] = kj.astype(BF16)
        km_ref[j:j + 1, :] = jnp.mean(kj, axis=0, keepdims=True)
        vb_ref[j] = vt_ref[0, :, j * bq:(j + 1) * bq].astype(BF16)
    kmean = km_ref[...]

    keyi = lax.broadcasted_iota(jnp.int32, (bq, bq), 0)
    qryi = lax.broadcasted_iota(jnp.int32, (bq, bq), 1)
    causal = jnp.where(keyi <= qryi, 0.0, NEG)
    row_lo = lax.broadcasted_iota(jnp.int32, (LANES, bq), 0) < HEAD_DIM
    qscale = (HEAD_DIM ** -0.5) * LOG2E

    for own in range(nblk):
        qt = qt_ref[0, :, own * bq:(own + 1) * bq]
        outs = []
        for h in range(2):
            qh = jnp.where(row_lo, 0.0, qt) if h else jnp.where(row_lo, qt, 0.0)
            qsb = (qh * qscale).astype(BF16)
            bias = None
            if own > TOPK:
                keep = _topk_keep(_dot_f32(kmean, qh), own, 0)
                bias = jnp.where(keep, 0.0, NEG)
            tiles = []
            m = None
            for j in range(own + 1):
                s = _dot(kb_ref[j], qsb)
                if j == own:
                    s = s + causal
                elif bias is not None:
                    s = s + bias[j:j + 1, :]
                tiles.append(s)
                mj = jnp.max(s, axis=0, keepdims=True)
                m = mj if m is None else jnp.maximum(m, mj)
            l = jnp.zeros((1, bq), F32)
            acc = jnp.zeros((HEAD_DIM, bq), F32)
            for j in range(own + 1):
                p = jnp.exp2(tiles[j] - m)
                l = l + jnp.sum(p, axis=0, keepdims=True)
                acc = acc + _dot(vb_ref[j, h * HEAD_DIM:(h + 1) * HEAD_DIM, :], p.astype(BF16))
            outs.append(acc / l)
        o_ref[0, own * bq:(own + 1) * bq, :] = jnp.concatenate(outs, axis=0).T.astype(o_ref.dtype)


def _moba_prompt(qt, kt, vt):
    b, _, t = qt.shape
    nblk = t // MOBA_BLOCK
    npad = max(16, nblk)
    npair = ATT_WIDTH // LANES
    spec = pl.BlockSpec((1, LANES, t), lambda bi, p: (bi, p, 0))
    return pl.pallas_call(
        functools.partial(_moba_kernel, nblk=nblk),
        grid=(b, npair),
        in_specs=[spec, spec, spec],
        out_specs=pl.BlockSpec((1, t, LANES), lambda bi, p: (bi, 0, p)),
        out_shape=jax.ShapeDtypeStruct((b, t, ATT_WIDTH), BF16),
        scratch_shapes=[pltpu.VMEM((nblk, MOBA_BLOCK, LANES), BF16),
                        pltpu.VMEM((nblk, LANES, MOBA_BLOCK), BF16),
                        pltpu.VMEM((npad, LANES), F32)],
        compiler_params=_params("parallel", "parallel"),
        name="moba_prompt",
    )(qt, kt, vt)


def _expand_heads(q):
    t = q.shape[0]
    tiled = jnp.concatenate([q] * ATT_HEADS, axis=0)
    rowh = lax.broadcasted_iota(jnp.int32, tiled.shape, 0) // t
    laneh = lax.broadcasted_iota(jnp.int32, tiled.shape, 1) // HEAD_DIM
    return jnp.where(rowh == laneh, tiled, 0.0)


def _sample_moba_kernel(pt_ref, q_ref, kn_ref, vn_ref, *rest, page, npg, nblk):
    del pt_ref
    k_refs = rest[:npg]
    v_refs = rest[npg:2 * npg]
    o_ref, km_ref, m_ref, l_ref, oblk_ref = rest[2 * npg:]
    c = pl.program_id(1)
    t = q_ref.shape[1]
    rows = ATT_HEADS * t
    pages_per_block = MOBA_BLOCK // page
    blocks_per_step = npg // pages_per_block

    qe = _expand_heads(q_ref[0])
    qb = (qe * ((HEAD_DIM ** -0.5) * LOG2E)).astype(BF16)
    blk_lane = lax.broadcasted_iota(jnp.int32, (ATT_WIDTH, LANES), 1)

    @pl.when(c == 0)
    def _():
        km_ref[...] = jnp.zeros(km_ref.shape, F32)

    tiles = []
    for jj in range(blocks_per_step):
        ksum = None
        for u in range(pages_per_block):
            kt = k_refs[jj * pages_per_block + u][...].reshape(ATT_WIDTH, page)
            ksum = kt if ksum is None else ksum + kt
            tiles.append(_dot(qb, kt.astype(BF16)))
        kmean = jnp.sum(ksum, axis=1, keepdims=True) * (1.0 / MOBA_BLOCK)
        km_ref[...] = jnp.where(blk_lane == c * blocks_per_step + jj, kmean, km_ref[...])
    probs = []
    for jj in range(blocks_per_step):
        mine = tiles[jj * pages_per_block:(jj + 1) * pages_per_block]
        m = jnp.max(mine[0], axis=1, keepdims=True)
        for s in mine[1:]:
            m = jnp.maximum(m, jnp.max(s, axis=1, keepdims=True))
        l = jnp.zeros((rows, 1), F32)
        for s in mine:
            p = jnp.exp2(s - m)
            l = l + jnp.sum(p, axis=1, keepdims=True)
            probs.append(p.astype(BF16))
        m_ref[c * blocks_per_step + jj] = jnp.broadcast_to(m, (rows, LANES))
        l_ref[c * blocks_per_step + jj] = jnp.broadcast_to(l, (rows, LANES))
    for jj in range(blocks_per_step):
        o = jnp.zeros((rows, ATT_WIDTH), F32)
        for u in range(pages_per_block):
            vt = v_refs[jj * pages_per_block + u][...].reshape(ATT_WIDTH, page).astype(BF16)
            o = o + _dot_nt(probs[jj * pages_per_block + u], vt)
        oblk_ref[c * blocks_per_step + jj] = o

    @pl.when(c == pl.num_programs(1) - 1)
    def _():
        keep = _topk_keep(_dot_f32(qe, km_ref[:, 0:nblk]), nblk, 1)
        s_own = _dot_nt(qb, kn_ref[0].astype(BF16))
        qpos = lax.broadcasted_iota(jnp.int32, s_own.shape, 0) % t
        kpos = lax.broadcasted_iota(jnp.int32, s_own.shape, 1)
        s_own = jnp.where(kpos <= qpos, s_own, NEG)
        lane_j = lax.broadcasted_iota(jnp.int32, (rows, nblk), 1)
        m_all = jnp.zeros((rows, nblk), F32)
        l_all = jnp.zeros((rows, nblk), F32)
        for j in range(nblk):
            m_all = jnp.where(lane_j == j, m_ref[j][:, 0:nblk], m_all)
            l_all = jnp.where(lane_j == j, l_ref[j][:, 0:nblk], l_all)
        mx = jnp.maximum(jnp.max(s_own, axis=1, keepdims=True),
                         jnp.max(jnp.where(keep, m_all, NEG), axis=1, keepdims=True))
        p_own = jnp.exp2(s_own - mx)
        w = jnp.where(keep, jnp.exp2(jnp.minimum(m_all - mx, 0.0)), 0.0)
        den = jnp.sum(p_own, axis=1, keepdims=True) + jnp.sum(w * l_all, axis=1, keepdims=True)
        vn = vn_ref[0]
        num = jnp.zeros((rows, ATT_WIDTH), F32)
        for tt in range(t):
            num = num + p_own[:, tt:tt + 1] * vn[tt:tt + 1, :]
        for j in range(nblk):
            num = num + w[:, j:j + 1] * oblk_ref[j]
        full = num / den
        laneh = lax.broadcasted_iota(jnp.int32, (t, ATT_WIDTH), 1) // HEAD_DIM
        out = jnp.zeros((t, ATT_WIDTH), F32)
        for h in range(ATT_HEADS):
            out = out + jnp.where(laneh == h, full[h * t:(h + 1) * t, :], 0.0)
        o_ref[0] = out


def _sample_moba(page_table, q, k_new, v_new, cache_kt, cache_vt):
    b, t, _ = q.shape
    page = cache_kt.shape[4]
    n_pages = page_table.shape[1]
    npg = PAGES_PER_STEP
    nsteps = n_pages // npg
    nblk = n_pages * page // MOBA_BLOCK
    assert nblk <= LANES and n_pages % npg == 0
    rows = ATT_HEADS * t
    per_b = pl.BlockSpec((1, t, ATT_WIDTH), lambda bi, c, pt: (bi, 0, 0))

    def pspec(pi):
        return pl.BlockSpec((None, None, ATT_HEADS, HEAD_DIM, page),
                            lambda bi, c, pt: (0, pt[bi, c * npg + pi], 0, 0, 0))

    grid_spec = pltpu.PrefetchScalarGridSpec(
        num_scalar_prefetch=1,
        grid=(b, nsteps),
        in_specs=[per_b, per_b, per_b] + [pspec(pi) for pi in range(npg)] * 2,
        out_specs=per_b,
        scratch_shapes=[pltpu.VMEM((ATT_WIDTH, LANES), F32),
                        pltpu.VMEM((nblk, rows, LANES), F32),
                        pltpu.VMEM((nblk, rows, LANES), F32),
                        pltpu.VMEM((nblk, rows, ATT_WIDTH), F32)],
    )
    return pl.pallas_call(
        functools.partial(_sample_moba_kernel, page=page, npg=npg, nblk=nblk),
        grid_spec=grid_spec,
        out_shape=jax.ShapeDtypeStruct((b, t, ATT_WIDTH), F32),
        compiler_params=_params("parallel", "arbitrary"),
        name="sample_moba",
    )(page_table, q, k_new, v_new, *([cache_kt] * npg), *([cache_vt] * npg))


def _cumsum_rows(x):
    n = x.shape[0]
    row = lax.broadcasted_iota(jnp.int32, x.shape, 0)
    step = 1
    while step < n:
        x = x + jnp.where(row >= step, pltpu.roll(x, step, 0), 0.0)
        step *= 2
    return x


def _ssd_kernel(xbc_ref, dtr_ref, z_ref, conv0_ref, h0_ref, cw_ref, cb_ref, dtb_ref, alog_ref,
                dsk_ref, gn_ref, y_ref, convo_ref, ho_ref, xpad_ref, h_ref, *, cl, valid, nbat):
    c = pl.program_id(1)
    nc = pl.num_programs(1)

    @pl.when(c == 0)
    def _():
        for bb in range(nbat):
            xpad_ref[bb, 0:CONV_PAD, :] = conv0_ref[bb]
            h_ref[bb] = h0_ref[bb]

    for bb in range(nbat):
        _ssd_chunk(bb, xbc_ref, dtr_ref, z_ref, cw_ref, cb_ref, dtb_ref, alog_ref, dsk_ref, gn_ref,
                   y_ref, xpad_ref, h_ref, cl, valid)

    @pl.when(c == nc - 1)
    def _():
        for bb in range(nbat):
            convo_ref[bb] = xpad_ref[bb, CONV_PAD + valid - (CONV_W - 1):CONV_PAD + valid, :]
            ho_ref[bb] = h_ref[bb]


def _ssd_chunk(bb, xbc_ref, dtr_ref, z_ref, cw_ref, cb_ref, dtb_ref, alog_ref, dsk_ref, gn_ref,
               y_ref, xpad_ref, h_ref, cl, valid):
    x = xbc_ref[bb]
    xpad_ref[bb, CONV_PAD:CONV_PAD + cl, :] = x
    w = cw_ref[...]
    conv = cb_ref[...] + x * w[CONV_W - 1:CONV_W, :]
    for back in range(1, CONV_W):
        conv = conv + xpad_ref[bb, pl.ds(CONV_PAD - back, cl), :] * w[CONV_W - 1 - back:CONV_W - back, :]

    xpad_ref[bb, 0:CONV_PAD, :] = xpad_ref[bb, cl:cl + CONV_PAD, :]
    xc = conv * _sigmoid(conv)
    xs = xc[:, :SSM_INNER]
    gw = SSM_STATE
    bm = [xc[:, SSM_INNER + g * gw:SSM_INNER + (g + 1) * gw].astype(BF16) for g in range(SSM_GROUPS)]
    cm = [xc[:, SSM_INNER + (SSM_GROUPS + g) * gw:SSM_INNER + (SSM_GROUPS + g + 1) * gw].astype(BF16)
          for g in range(SSM_GROUPS)]

    pre = dtr_ref[bb] + dtb_ref[...]
    dt = jnp.maximum(pre, 0.0) + jnp.log1p(jnp.exp(-jnp.abs(pre)))
    rowl = lax.broadcasted_iota(jnp.int32, dt.shape, 0)
    if valid < cl:
        dt = jnp.where(rowl < valid, dt, 0.0)
    da = dt * (-jnp.exp(alog_ref[...]))
    cum = _cumsum_rows(da)
    cum_t = cum.T
    cum_last = cum[cl - 1:cl, :]
    ecum = jnp.exp(cum)
    tail = jnp.exp(cum_last - cum)
    elast = jnp.exp(cum_last)

    ri = lax.broadcasted_iota(jnp.int32, (cl, cl), 0)
    ci = lax.broadcasted_iota(jnp.int32, (cl, cl), 1)
    causal = ci <= ri
    lane = lax.broadcasted_iota(jnp.int32, (cl, LANES), 1)
    lo = lane < SSM_HEAD_DIM
    row_lo = lax.broadcasted_iota(jnp.int32, (LANES, 1), 0) < SSM_HEAD_DIM
    heads_per_group = SSM_HEADS // SSM_GROUPS
    gmat = [_dot_nt(cm[g], bm[g]) for g in range(SSM_GROUPS)]

    def pick(cols, h0):
        return jnp.where(lo, cols[:, h0:h0 + 1], cols[:, h0 + 1:h0 + 2])

    ys = []
    for p in range(SSM_HEADS // 2):
        h0 = 2 * p
        g = h0 // heads_per_group
        xs_p = xs[:, p * LANES:(p + 1) * LANES]
        xdt = xs_p * pick(dt, h0)
        xdt_b = xdt.astype(BF16)
        parts = []
        for h in (h0, h0 + 1):
            seg = cum[:, h:h + 1] - cum_t[h:h + 1, :]
            dec = jnp.where(causal, jnp.exp(jnp.minimum(seg, 0.0)), 0.0)
            parts.append(_dot((gmat[g] * dec).astype(BF16), xdt_b))
        hp = h_ref[bb, p * LANES:(p + 1) * LANES, :]
        y_p = jnp.where(lo, parts[0], parts[1])
        y_p = y_p + _dot_nt(cm[g], hp.astype(BF16)) * pick(ecum, h0)
        y_p = y_p + dsk_ref[:, p * LANES:(p + 1) * LANES] * xs_p
        xw = (xdt * pick(tail, h0)).astype(BF16)
        decay_rows = jnp.where(row_lo, elast[:, h0:h0 + 1], elast[:, h0 + 1:h0 + 2])
        h_ref[bb, p * LANES:(p + 1) * LANES, :] = hp * decay_rows + _dot_tn(xw, bm[g])
        ys.append(y_p)

    y = jnp.concatenate(ys, axis=1)
    zz = z_ref[bb]
    yz = y * (zz * _sigmoid(zz))
    gwid = SSM_INNER // SSM_GROUPS
    normed = []
    for g in range(SSM_GROUPS):
        part = yz[:, g * gwid:(g + 1) * gwid]
        normed.append(part * lax.rsqrt(jnp.mean(part * part, axis=-1, keepdims=True) + EPS))
    y_ref[bb] = (jnp.concatenate(normed, axis=1) * gn_ref[...]).astype(y_ref.dtype)


def _ssd(xbc, dtr, z, conv0, h0, cw, cb, dtb, alog, dsk, gn, cl, valid, out_dtype):
    b, t, _ = xbc.shape
    nc = t // cl
    nbat = SSD_SEQS_PER_STEP
    assert b % nbat == 0
    rows = lambda w: pl.BlockSpec((nbat, cl, w), lambda bi, c: (bi, c, 0))
    per_b = lambda shape: pl.BlockSpec((nbat,) + shape, lambda bi, c: (bi, 0, 0))
    hshape = (SSM_HEADS * SSM_HEAD_DIM, SSM_STATE)
    return pl.pallas_call(
        functools.partial(_ssd_kernel, cl=cl, valid=valid, nbat=nbat),
        grid=(b // nbat, nc),
        in_specs=[rows(CONV_DIM), rows(LANES), rows(SSM_INNER), per_b((CONV_PAD, CONV_DIM)),
                  per_b(hshape), _const(cw.shape), _const(cb.shape), _const(dtb.shape),
                  _const(alog.shape), _const(dsk.shape), _const(gn.shape)],
        out_specs=[rows(SSM_INNER), per_b((CONV_W - 1, CONV_DIM)), per_b(hshape)],
        out_shape=[jax.ShapeDtypeStruct((b, t, SSM_INNER), out_dtype),
                   jax.ShapeDtypeStruct((b, CONV_W - 1, CONV_DIM), F32),
                   jax.ShapeDtypeStruct((b,) + hshape, F32)],
        scratch_shapes=[pltpu.VMEM((nbat, CONV_PAD + cl, CONV_DIM), F32), pltpu.VMEM((nbat,) + hshape, F32)],
        compiler_params=_params("parallel", "arbitrary"),
        name="ssd",
    )(xbc, dtr, z, conv0, h0, cw, cb, dtb, alog, dsk, gn)


def _outproj_kernel(x_ref, att_ref, y_ref, wa_ref, wy_ref, g_ref, wq_ref, gq_ref, x1_ref, qx_ref):
    mix = _dot(att_ref[...].astype(BF16), wa_ref[...]) + _dot(y_ref[...].astype(BF16), wy_ref[...])
    x1 = x_ref[...] + mix
    x1_ref[...] = x1
    hx = _rms(x1, g_ref[...]).astype(BF16)
    qx = _dot(hx, wq_ref[...])
    gq = gq_ref[...]
    for h in range(X_HEADS):
        sl = slice(h * X_HEAD_DIM, (h + 1) * X_HEAD_DIM)
        qx_ref[:, sl] = _rms(qx[:, sl], gq).astype(qx_ref.dtype)


def _outproj(x, att, y, w_att, w_y, g, wq, gq, tm, act_dtype):
    n = x.shape[0]
    row = lambda w: pl.BlockSpec((tm, w), lambda i: (i, 0))
    return pl.pallas_call(
        _outproj_kernel,
        grid=(n // tm,),
        in_specs=[row(D_MODEL), row(ATT_WIDTH), row(SSM_INNER), _const(w_att.shape), _const(w_y.shape),
                  _const(g.shape), _const(wq.shape), _const(gq.shape)],
        out_specs=[row(D_MODEL), row(X_WIDTH)],
        out_shape=[jax.ShapeDtypeStruct((n, D_MODEL), F32), jax.ShapeDtypeStruct((n, X_WIDTH), act_dtype)],
        compiler_params=_params("parallel"),
        name="outproj",
    )(x, att, y, w_att, w_y, g, wq, gq)


def _xattn_kernel(q_ref, k_ref, v_ref, o_ref):
    scale = X_HEAD_DIM ** -0.5
    for h in range(X_HEADS):
        sl = slice(h * X_HEAD_DIM, (h + 1) * X_HEAD_DIM)
        q = (q_ref[0, :, sl].astype(F32) * scale).astype(BF16)
        s = _dot_nt(q, k_ref[0, :, sl].astype(BF16))
        m = jnp.max(s, axis=-1, keepdims=True)
        p = jnp.exp(s - m)
        l = jnp.sum(p, axis=-1, keepdims=True)
        o = _dot(p.astype(BF16), v_ref[0, :, sl].astype(BF16)) / l
        o_ref[0, :, sl] = o.astype(o_ref.dtype)


def _xattn(q, mk, mv, tq, act_dtype):
    b, t, _ = q.shape
    m = mk.shape[1]
    qspec = pl.BlockSpec((1, tq, X_WIDTH), lambda bi, i: (bi, i, 0))
    kvspec = pl.BlockSpec((1, m, X_WIDTH), lambda bi, i: (bi, 0, 0))
    return pl.pallas_call(
        _xattn_kernel,
        grid=(b, t // tq),
        in_specs=[qspec, kvspec, kvspec],
        out_specs=qspec,
        out_shape=jax.ShapeDtypeStruct((b, t, X_WIDTH), act_dtype),
        compiler_params=_params("parallel", "parallel"),
        name="xattn",
    )(q, mk, mv)


def _mlp_kernel(x1_ref, o_ref, wo_ref, g_ref, wu_ref, wd_ref, out_ref, *, ff_chunk):
    x2 = x1_ref[...] + _dot(o_ref[...].astype(BF16), wo_ref[...])
    hm = _rms(x2, g_ref[...]).astype(BF16)
    acc = x2
    for cidx in range(D_FF // ff_chunk):
        sl = slice(cidx * ff_chunk, (cidx + 1) * ff_chunk)
        up = jnp.maximum(_dot(hm, wu_ref[:, sl]), 0.0)
        acc = acc + _dot((up * up).astype(BF16), wd_ref[sl, :])
    out_ref[...] = acc


def _mlp(x1, o, wo, g, wu, wd, tm):
    n = x1.shape[0]
    row = lambda w: pl.BlockSpec((tm, w), lambda i: (i, 0))
    return pl.pallas_call(
        functools.partial(_mlp_kernel, ff_chunk=1024),
        grid=(n // tm,),
        in_specs=[row(D_MODEL), row(X_WIDTH), _const_single(wo.shape), _const(g.shape),
                  _const_single(wu.shape), _const_single(wd.shape)],
        out_specs=row(D_MODEL),
        out_shape=jax.ShapeDtypeStruct((n, D_MODEL), F32),
        compiler_params=_params("parallel"),
        name="mlp",
    )(x1, o, wo, g, wu, wd)


def _memkv_kernel(m_ref, g_ref, wk_ref, wv_ref, gk_ref, mk_ref, mv_ref):
    mn = _rms(m_ref[...], g_ref[...]).astype(BF16)
    mk = _dot(mn, wk_ref[...])
    gk = gk_ref[...]
    for h in range(X_HEADS):
        sl = slice(h * X_HEAD_DIM, (h + 1) * X_HEAD_DIM)
        mk_ref[:, sl] = _rms(mk[:, sl], gk)
    mv_ref[...] = _dot(mn, wv_ref[...])


def _memkv(mem, g, wk, wv, gk, tm):
    n = mem.shape[0]
    row = lambda w: pl.BlockSpec((tm, w), lambda i: (i, 0))
    return pl.pallas_call(
        _memkv_kernel,
        grid=(n // tm,),
        in_specs=[row(D_MODEL), _const(g.shape), _const(wk.shape), _const(wv.shape), _const(gk.shape)],
        out_specs=[row(X_WIDTH), row(X_WIDTH)],
        out_shape=[jax.ShapeDtypeStruct((n, X_WIDTH), F32)] * 2,
        compiler_params=_params("parallel"),
        name="memkv",
    )(mem, g, wk, wv, gk)


def _rope_tables(pos):
    half = HEAD_DIM // 2
    inv_freq = ROPE_THETA ** (-jnp.arange(half, dtype=F32) / half)
    ang = pos.astype(F32)[:, None] * inv_freq[None, :]
    return jnp.cos(ang).T, jnp.sin(ang).T


def _row_tile(n, want):
    tm = min(n, want)
    assert n % tm == 0
    return tm


def _layer(x, nb, cos_t, sin_t, conv0, h0, mem_k, mem_v, wts, attend, tm, act_dtype):
    b, t, _ = x.shape
    n = b * t
    xf = x.reshape(n, D_MODEL)
    qt, kt, vt, z, xbc, dtr = _inproj(xf, nb, wts["ln_mix_g"], wts["w_qkv_t"], wts["w_rest"], wts["w_dt"],
                                      wts["q_norm_g"], wts["k_norm_g"], cos_t, sin_t, tm)
    att, extras = attend(qt, kt, vt)

    cl = SSD_CHUNK
    tp = -(-t // cl) * cl
    pad3 = lambda a: jnp.pad(a.reshape(b, t, -1), ((0, 0), (0, tp - t), (0, 0)))
    y, new_conv, new_h = _ssd(pad3(xbc), pad3(dtr), pad3(z), conv0, h0, wts["conv_w"], wts["conv_b"],
                              wts["dt_bias"], wts["a_log"], wts["d_skip"], wts["ssm_norm_g"],
                              cl, min(t, cl) if tp != t else cl, act_dtype)
    y = y[:, :t].reshape(n, SSM_INNER)

    x1, qx = _outproj(xf, att, y, wts["w_out_att"], wts["w_out_ssm"], wts["ln_x_g"],
                      wts["wq_x"], wts["qx_norm_g"], tm, act_dtype)
    tq = _row_tile(t, 512)
    o = _xattn(qx.reshape(b, t, X_WIDTH), mem_k, mem_v, tq, act_dtype)
    out = _mlp(x1, o.reshape(n, X_WIDTH), wts["wo_x"], wts["ln_mlp_g"], wts["w_up"], wts["w_down"], tm)
    return out.reshape(b, t, D_MODEL), extras, new_conv, new_h


def kernel(x_prompt, x_sample, mem_prompt, cache_k, cache_v, page_table, state_conv, state_ssm,
           cache_mem_k, cache_mem_v, ln_mix_g, w_in, q_norm_g, k_norm_g, conv_w, conv_b, dt_bias,
           a_log, d_skip, ssm_norm_g, w_out, ln_x_g, ln_mem_g, wq_x, wk_x, wv_x, qx_norm_g,
           kx_norm_g, wo_x, ln_mlp_g, w_up, w_down):
    depth = w_in.shape[0]
    assert depth == 1, "single-layer step"
    bp, tp, _ = x_prompt.shape
    bs, ts, _ = x_sample.shape
    page = cache_k.shape[2]
    past_len = page_table.shape[1] * page
    assert tp % MOBA_BLOCK == 0 and past_len % MOBA_BLOCK == 0 and MOBA_BLOCK % page == 0
    assert ts <= MOBA_BLOCK and ts <= SSD_CHUNK and ts >= CONV_W - 1
    l = 0
    qkv_cols = 3 * ATT_WIDTH
    main_cols = qkv_cols + SSM_INNER + CONV_DIM

    def lanes(vec, width=LANES):
        return jnp.pad(vec.astype(F32), (0, width - vec.shape[0])).reshape(1, width)

    def dim_gain(vec):
        return jnp.broadcast_to(vec.astype(F32)[:, None], (HEAD_DIM, LANES))

    wts = {
        "ln_mix_g": ln_mix_g[l].reshape(1, D_MODEL),
        "w_qkv_t": w_in[l][:, :qkv_cols].T.astype(BF16),
        "w_rest": w_in[l][:, qkv_cols:main_cols].astype(BF16),
        "w_dt": jnp.pad(w_in[l][:, main_cols:], ((0, 0), (0, LANES - SSM_HEADS))).astype(BF16),
        "q_norm_g": dim_gain(q_norm_g[l]),
        "k_norm_g": dim_gain(k_norm_g[l]),
        "conv_w": conv_w[l],
        "conv_b": conv_b[l].reshape(1, CONV_DIM),
        "dt_bias": lanes(dt_bias[l]),
        "a_log": lanes(a_log[l]),
        "d_skip": jnp.repeat(d_skip[l].astype(F32), SSM_HEAD_DIM).reshape(1, SSM_INNER),
        "ssm_norm_g": ssm_norm_g[l].reshape(1, SSM_INNER),
        "w_out_att": w_out[l][:ATT_WIDTH].astype(BF16),
        "w_out_ssm": w_out[l][ATT_WIDTH:].astype(BF16),
        "ln_x_g": ln_x_g[l].reshape(1, D_MODEL),
        "wq_x": wq_x[l].astype(BF16),
        "qx_norm_g": qx_norm_g[l].reshape(1, X_HEAD_DIM),
        "wo_x": wo_x[l].astype(BF16),
        "ln_mlp_g": ln_mlp_g[l].reshape(1, D_MODEL),
        "w_up": w_up[l].astype(BF16),
        "w_down": w_down[l].astype(BF16),
    }

    mem_rows = mem_prompt.reshape(-1, D_MODEL)
    mk_p, mv_p = _memkv(mem_rows, ln_mem_g[l].reshape(1, D_MODEL), wk_x[l].astype(BF16),
                        wv_x[l].astype(BF16), kx_norm_g[l].reshape(1, X_HEAD_DIM),
                        _row_tile(mem_rows.shape[0], 512))
    mem_len = mem_prompt.shape[1]
    mk_p = mk_p.reshape(bp, mem_len, X_WIDTH)
    mv_p = mv_p.reshape(bp, mem_len, X_WIDTH)
    cos_p, sin_p = _rope_tables(jnp.arange(tp, dtype=jnp.int32))
    conv0_p = jnp.zeros((bp, CONV_PAD, CONV_DIM), F32)
    h0_p = jnp.zeros((bp, SSM_HEADS * SSM_HEAD_DIM, SSM_STATE), F32)

    def attend_prompt(qt, kt, vt):
        heads_t = lambda a: a.reshape(1, bp, ATT_HEADS, HEAD_DIM, tp).transpose(0, 1, 4, 2, 3)
        return _moba_prompt(qt, kt, vt).reshape(bp * tp, ATT_WIDTH), (heads_t(kt), heads_t(vt))

    y_p, (k_p, v_p), conv_p, ssm_p = _layer(x_prompt, bp, cos_p, sin_p, conv0_p, h0_p, mk_p, mv_p, wts,
                                            attend_prompt, _row_tile(tp, 512), BF16)

    cos_s, sin_s = _rope_tables(past_len + jnp.arange(ts, dtype=jnp.int32))
    cos_s = jnp.tile(cos_s, (1, bs))
    sin_s = jnp.tile(sin_s, (1, bs))
    conv0_s = jnp.pad(state_conv[l], ((0, 0), (CONV_PAD - (CONV_W - 1), 0), (0, 0)))
    h0_s = state_ssm[l].reshape(bs, SSM_HEADS * SSM_HEAD_DIM, SSM_STATE)
    cache_kt = cache_k.transpose(0, 1, 3, 4, 2)
    cache_vt = cache_v.transpose(0, 1, 3, 4, 2)

    def attend_sample(qt, kt, vt):
        rows = lambda a: a[0].T.reshape(bs, ts, ATT_WIDTH)
        k_new, v_new = rows(kt), rows(vt)
        o = _sample_moba(page_table, rows(qt), k_new, v_new, cache_kt, cache_vt)
        heads = lambda a: a.reshape(1, bs, ts, ATT_HEADS, HEAD_DIM)
        return o.reshape(bs * ts, ATT_WIDTH), (heads(k_new), heads(v_new))

    y_s, (k_s, v_s), conv_s, ssm_s = _layer(
        x_sample, 1, cos_s, sin_s, conv0_s, h0_s, cache_mem_k[l].reshape(bs, -1, X_WIDTH),
        cache_mem_v[l].reshape(bs, -1, X_WIDTH), wts, attend_sample, bs * ts, F32)

    state = lambda a, b: a.reshape(1, b, SSM_HEADS, SSM_HEAD_DIM, SSM_STATE)
    return (y_p, y_s, k_p, v_p, conv_p[None], state(ssm_p, bp),
            mk_p.reshape(1, bp, mem_len, X_HEADS, X_HEAD_DIM), mv_p.reshape(1, bp, mem_len, X_HEADS, X_HEAD_DIM),
            k_s, v_s, conv_s[None], state(ssm_s, bs))
```

```python
import functools
import math

import jax
import jax.numpy as jnp
from jax import lax
from jax.experimental import pallas as pl
from jax.experimental.pallas import tpu as pltpu

F32 = jnp.float32
BF16 = jnp.bfloat16

D_MODEL = 1024
ATT_HEADS = 8
HEAD_DIM = 64
ATT_WIDTH = ATT_HEADS * HEAD_DIM
MOBA_BLOCK = 256
TOPK = 3
ROPE_THETA = 10000.0
SSM_HEADS = 8
SSM_HEAD_DIM = 64
SSM_INNER = SSM_HEADS * SSM_HEAD_DIM
SSM_GROUPS = 2
SSM_STATE = 128
CONV_W = 4
CONV_DIM = SSM_INNER + 2 * SSM_GROUPS * SSM_STATE
SSD_CHUNK = 128
X_HEADS = 4
X_HEAD_DIM = 128
X_WIDTH = X_HEADS * X_HEAD_DIM
D_FF = 4 * D_MODEL
EPS = 1e-6

LANES = 128
CONV_PAD = 8
VMEM_LIMIT = 56 * 1024 * 1024
NEG = -1e30
LOG2E = 1.4426950408889634
ONES_ROWS = 16
FF_CHUNK = 1024
XATTN_SEQS_PER_STEP = 8
SSD_SEQS_PER_STEP = 4
PAGES_PER_STEP = 16


def _dot(a, b):
    return jnp.dot(a, b, preferred_element_type=F32)


def _dot_nt(a, b):
    return lax.dot_general(a, b, (((1,), (1,)), ((), ())), preferred_element_type=F32)


def _dot_tn(a, b):
    return lax.dot_general(a, b, (((0,), (0,)), ((), ())), preferred_element_type=F32)


def _split3(x):
    hi = x.astype(BF16)
    r = x - hi.astype(F32)
    mid = r.astype(BF16)
    lo = (r - mid.astype(F32)).astype(BF16)
    return hi, mid, lo


def _dot_f32(a, b, dot=None):
    dot = dot or _dot
    a0, a1, a2 = _split3(a)
    b0, b1, b2 = _split3(b)
    return (dot(a0, b0) + (dot(a0, b1) + dot(a1, b0)) + (dot(a1, b1) + dot(a0, b2) + dot(a2, b0)))


def _rms(x, g):
    ms = jnp.mean(x * x, axis=-1, keepdims=True)
    return x * lax.rsqrt(ms + EPS) * g


def _sigmoid(x):
    return 1.0 / (1.0 + jnp.exp(-x))


def _params(*sem):
    return pltpu.CompilerParams(dimension_semantics=sem, vmem_limit_bytes=VMEM_LIMIT)


def _const(shape):
    return pl.BlockSpec(shape, lambda *_: (0,) * len(shape))


def _const_single(shape):
    return pl.BlockSpec(shape, lambda *_: (0,) * len(shape), pipeline_mode=pl.Buffered(1))


def _inproj_kernel(x_ref, g_ref, wqkv_ref, wrest_ref, wdt_ref, gq_ref, gk_ref, cos_ref, sin_ref,
                   q_ref, k_ref, v_ref, z_ref, xbc_ref, dt_ref):
    hn = _rms(x_ref[...], g_ref[...]).astype(BF16)
    tm = hn.shape[0]
    cos = cos_ref[...]
    sin = sin_ref[...]
    half = HEAD_DIM // 2

    def proj_t(idx):
        return _dot_nt(wqkv_ref[idx * ATT_WIDTH:(idx + 1) * ATT_WIDTH, :], hn)

    def head_norm_rot(p, g_ref_):
        p3 = p.reshape(ATT_HEADS, HEAD_DIM, tm)
        gain = g_ref_[:, 0:1]
        pn = p3 * lax.rsqrt(jnp.mean(p3 * p3, axis=1, keepdims=True) + EPS) * gain
        x1 = pn[:, :half]
        x2 = pn[:, half:]
        rot = jnp.concatenate([x1 * cos - x2 * sin, x2 * cos + x1 * sin], axis=1)
        return rot.reshape(ATT_WIDTH, tm)

    q_ref[0] = head_norm_rot(proj_t(0), gq_ref)
    k_ref[0] = head_norm_rot(proj_t(1), gk_ref)
    v_ref[0] = proj_t(2)
    z_ref[...] = _dot(hn, wrest_ref[:, 0:SSM_INNER])
    xbc_ref[...] = _dot(hn, wrest_ref[:, SSM_INNER:SSM_INNER + CONV_DIM])
    dt_ref[...] = _dot(hn, wdt_ref[...])


def _inproj(x, nb, g, w_qkv_t, w_rest, w_dt, gq, gk, cos_t, sin_t, tm):
    n = x.shape[0]
    cols = n // nb
    tpb = cols // tm
    row = lambda w: pl.BlockSpec((tm, w), lambda i: (i, 0))
    tab = pl.BlockSpec((HEAD_DIM // 2, tm), lambda i: (0, i % tpb))
    tspec = pl.BlockSpec((1, ATT_WIDTH, tm), lambda i: (i // tpb, 0, i % tpb))
    outs = ([jax.ShapeDtypeStruct((nb, ATT_WIDTH, cols), F32)] * 3
            + [jax.ShapeDtypeStruct((n, w), F32) for w in (SSM_INNER, CONV_DIM, LANES)])
    return pl.pallas_call(
        _inproj_kernel,
        grid=(n // tm,),
        in_specs=[row(D_MODEL), _const(g.shape), _const_single(w_qkv_t.shape), _const_single(w_rest.shape),
                  _const(w_dt.shape), _const(gq.shape), _const(gk.shape), tab, tab],
        out_specs=[tspec, tspec, tspec, row(SSM_INNER), row(CONV_DIM), row(LANES)],
        out_shape=outs,
        compiler_params=_params("parallel"),
        name="inproj",
    )(x, g, w_qkv_t, w_rest, w_dt, gq, gk, cos_t, sin_t)


def _topk_keep(gate, n_valid, axis):
    idx = lax.broadcasted_iota(jnp.int32, gate.shape, axis)
    cnt = jnp.zeros(gate.shape, jnp.int32)
    for jp in range(gate.shape[axis]):
        gj = lax.slice_in_dim(gate, jp, jp + 1, axis=axis)
        beats = (gj > gate) | ((gj == gate) & (jp < idx))
        cnt = cnt + jnp.where(beats, jnp.where(jp < n_valid, 1, 0), 0)
    return (idx < n_valid) & (cnt < TOPK)


def _moba_kernel(qt_ref, kt_ref, vt_ref, o_ref, kb_ref, vb_ref, km_ref, *, nblk):
    bq = MOBA_BLOCK
    km_ref[...] = jnp.zeros(km_ref.shape, F32)
    for j in range(nblk):
        kj = kt_ref[0, :, j * bq:(j + 1) * bq].T
        kb_ref[j] = kj.astype(BF16)
        km_ref[j:j + 1, :] = jnp.mean(kj, axis=0, keepdims=True)
        vj = vt_ref[0, :, j * bq:(j + 1) * bq].astype(BF16)
        for h in range(2):
            vb_ref[j, h, 0:HEAD_DIM, :] = vj[h * HEAD_DIM:(h + 1) * HEAD_DIM, :]
            vb_ref[j, h, HEAD_DIM:, :] = jnp.ones((ONES_ROWS, bq), BF16)
    kmean = km_ref[...]

    keyi = lax.broadcasted_iota(jnp.int32, (bq, bq), 0)
    qryi = lax.broadcasted_iota(jnp.int32, (bq, bq), 1)
    causal = jnp.where(keyi <= qryi, 0.0, NEG)
    row_lo = lax.broadcasted_iota(jnp.int32, (LANES, bq), 0) < HEAD_DIM
    qscale = (HEAD_DIM ** -0.5) * LOG2E

    for own in range(nblk):
        qt = qt_ref[0, :, own * bq:(own + 1) * bq]
        outs = []
        for h in range(2):
            qh = jnp.where(row_lo, 0.0, qt) if h else jnp.where(row_lo, qt, 0.0)
            qsb = (qh * qscale).astype(BF16)
            bias = None
            if own > TOPK:
                keep = _topk_keep(_dot_f32(kmean, qh), own, 0)
                bias = jnp.where(keep, 0.0, NEG)
            tiles = []
            m = None
            for j in range(own + 1):
                s = _dot(kb_ref[j], qsb)
                if j == own:
                    s = s + causal
                elif bias is not None:
                    s = s + bias[j:j + 1, :]
                tiles.append(s)
                mj = jnp.max(s, axis=0, keepdims=True)
                m = mj if m is None else jnp.maximum(m, mj)
            acc = jnp.zeros((HEAD_DIM + ONES_ROWS, bq), F32)
            for j in range(own + 1):
                p = jnp.exp2(tiles[j] - m)
                acc = acc + _dot(vb_ref[j, h], p.astype(BF16))
            outs.append(acc[0:HEAD_DIM] / acc[HEAD_DIM:HEAD_DIM + 1])
        o_ref[0, own * bq:(own + 1) * bq, :] = jnp.concatenate(outs, axis=0).T.astype(o_ref.dtype)


def _moba_prompt(qt, kt, vt):
    b, _, t = qt.shape
    nblk = t // MOBA_BLOCK
    npad = max(16, nblk)
    npair = ATT_WIDTH // LANES
    spec = pl.BlockSpec((1, LANES, t), lambda bi, p: (bi, p, 0))
    return pl.pallas_call(
        functools.partial(_moba_kernel, nblk=nblk),
        grid=(b, npair),
        in_specs=[spec, spec, spec],
        out_specs=pl.BlockSpec((1, t, LANES), lambda bi, p: (bi, 0, p)),
        out_shape=jax.ShapeDtypeStruct((b, t, ATT_WIDTH), BF16),
        scratch_shapes=[pltpu.VMEM((nblk, MOBA_BLOCK, LANES), BF16),
                        pltpu.VMEM((nblk, 2, HEAD_DIM + ONES_ROWS, MOBA_BLOCK), BF16),
                        pltpu.VMEM((npad, LANES), F32)],
        compiler_params=_params("parallel", "parallel"),
        name="moba_prompt",
    )(qt, kt, vt)


def _expand_heads(q):
    t = q.shape[0]
    tiled = jnp.concatenate([q] * ATT_HEADS, axis=0)
    rowh = lax.broadcasted_iota(jnp.int32, tiled.shape, 0) // t
    laneh = lax.broadcasted_iota(jnp.int32, tiled.shape, 1) // HEAD_DIM
    return jnp.where(rowh == laneh, tiled, 0.0)


def _sample_moba_kernel(pt_ref, q_ref, kn_ref, vn_ref, *rest, page, npg, nblk):
    del pt_ref
    k_refs = rest[:npg]
    v_refs = rest[npg:2 * npg]
    o_ref, km_ref, m_ref, l_ref, oblk_ref = rest[2 * npg:]
    c = pl.program_id(1)
    t = q_ref.shape[1]
    rows = ATT_HEADS * t
    pages_per_block = MOBA_BLOCK // page
    blocks_per_step = npg // pages_per_block

    qe = _expand_heads(q_ref[0])
    qb = (qe * ((HEAD_DIM ** -0.5) * LOG2E)).astype(BF16)
    blk_lane = lax.broadcasted_iota(jnp.int32, (ATT_WIDTH, LANES), 1)

    @pl.when(c == 0)
    def _():
        km_ref[...] = jnp.zeros(km_ref.shape, F32)

    tiles = []
    for jj in range(blocks_per_step):
        ksum = None
        for u in range(pages_per_block):
            kt = k_refs[jj * pages_per_block + u][...].reshape(ATT_WIDTH, page)
            ksum = kt if ksum is None else ksum + kt
            tiles.append(_dot(qb, kt.astype(BF16)))
        kmean = jnp.sum(ksum, axis=1, keepdims=True) * (1.0 / MOBA_BLOCK)
        km_ref[...] = jnp.where(blk_lane == c * blocks_per_step + jj, kmean, km_ref[...])
    probs = []
    for jj in range(blocks_per_step):
        mine = tiles[jj * pages_per_block:(jj + 1) * pages_per_block]
        m = jnp.max(mine[0], axis=1, keepdims=True)
        for s in mine[1:]:
            m = jnp.maximum(m, jnp.max(s, axis=1, keepdims=True))
        l = jnp.zeros((rows, 1), F32)
        for s in mine:
            p = jnp.exp2(s - m)
            l = l + jnp.sum(p, axis=1, keepdims=True)
            probs.append(p.astype(BF16))
        m_ref[c * blocks_per_step + jj] = jnp.broadcast_to(m, (rows, LANES))
        l_ref[c * blocks_per_step + jj] = jnp.broadcast_to(l, (rows, LANES))
    for jj in range(blocks_per_step):
        o = jnp.zeros((rows, ATT_WIDTH), F32)
        for u in range(pages_per_block):
            vt = v_refs[jj * pages_per_block + u][...].reshape(ATT_WIDTH, page).astype(BF16)
            o = o + _dot_nt(probs[jj * pages_per_block + u], vt)
        oblk_ref[c * blocks_per_step + jj] = o

    @pl.when(c == pl.num_programs(1) - 1)
    def _():
        keep = _topk_keep(_dot_f32(qe, km_ref[:, 0:nblk]), nblk, 1)
        s_own = _dot_nt(qb, kn_ref[0].astype(BF16))
        qpos = lax.broadcasted_iota(jnp.int32, s_own.shape, 0) % t
        kpos = lax.broadcasted_iota(jnp.int32, s_own.shape, 1)
        s_own = jnp.where(kpos <= qpos, s_own, NEG)
        lane_j = lax.broadcasted_iota(jnp.int32, (rows, nblk), 1)
        m_all = jnp.zeros((rows, nblk), F32)
        l_all = jnp.zeros((rows, nblk), F32)
        for j in range(nblk):
            m_all = jnp.where(lane_j == j, m_ref[j][:, 0:nblk], m_all)
            l_all = jnp.where(lane_j == j, l_ref[j][:, 0:nblk], l_all)
        mx = jnp.maximum(jnp.max(s_own, axis=1, keepdims=True),
                         jnp.max(jnp.where(keep, m_all, NEG), axis=1, keepdims=True))
        p_own = jnp.exp2(s_own - mx)
        w = jnp.where(keep, jnp.exp2(jnp.minimum(m_all - mx, 0.0)), 0.0)
        den = jnp.sum(p_own, axis=1, keepdims=True) + jnp.sum(w * l_all, axis=1, keepdims=True)
        vn = vn_ref[0]
        num = jnp.zeros((rows, ATT_WIDTH), F32)
        for tt in range(t):
            num = num + p_own[:, tt:tt + 1] * vn[tt:tt + 1, :]
        for j in range(nblk):
            num = num + w[:, j:j + 1] * oblk_ref[j]
        full = num / den
        laneh = lax.broadcasted_iota(jnp.int32, (t, ATT_WIDTH), 1) // HEAD_DIM
        out = jnp.zeros((t, ATT_WIDTH), F32)
        for h in range(ATT_HEADS):
            out = out + jnp.where(laneh == h, full[h * t:(h + 1) * t, :], 0.0)
        o_ref[0] = out


def _sample_moba(page_table, q, k_new, v_new, cache_kt, cache_vt):
    b, t, _ = q.shape
    page = cache_kt.shape[4]
    n_pages = page_table.shape[1]
    npg = PAGES_PER_STEP
    nsteps = n_pages // npg
    nblk = n_pages * page // MOBA_BLOCK
    assert nblk <= LANES and n_pages % npg == 0
    rows = ATT_HEADS * t
    per_b = pl.BlockSpec((1, t, ATT_WIDTH), lambda bi, c, pt: (bi, 0, 0))

    def pspec(pi):
        return pl.BlockSpec((None, None, ATT_HEADS, HEAD_DIM, page),
                            lambda bi, c, pt: (0, pt[bi, c * npg + pi], 0, 0, 0))

    grid_spec = pltpu.PrefetchScalarGridSpec(
        num_scalar_prefetch=1,
        grid=(b, nsteps),
        in_specs=[per_b, per_b, per_b] + [pspec(pi) for pi in range(npg)] * 2,
        out_specs=per_b,
        scratch_shapes=[pltpu.VMEM((ATT_WIDTH, LANES), F32),
                        pltpu.VMEM((nblk, rows, LANES), F32),
                        pltpu.VMEM((nblk, rows, LANES), F32),
                        pltpu.VMEM((nblk, rows, ATT_WIDTH), F32)],
    )
    return pl.pallas_call(
        functools.partial(_sample_moba_kernel, page=page, npg=npg, nblk=nblk),
        grid_spec=grid_spec,
        out_shape=jax.ShapeDtypeStruct((b, t, ATT_WIDTH), F32),
        compiler_params=_params("parallel", "arbitrary"),
        name="sample_moba",
    )(page_table, q, k_new, v_new, *([cache_kt] * npg), *([cache_vt] * npg))


def _cumsum_rows(x):
    n = x.shape[0]
    row = lax.broadcasted_iota(jnp.int32, x.shape, 0)
    step = 1
    while step < n:
        x = x + jnp.where(row >= step, pltpu.roll(x, step, 0), 0.0)
        step *= 2
    return x


def _ssd_kernel(xbc_ref, dtr_ref, z_ref, conv0_ref, h0_ref, cw_ref, cb_ref, dtb_ref, alog_ref,
                dsk_ref, gn_ref, y_ref, convo_ref, ho_ref, xpad_ref, h_ref, *, cl, valid, nbat):
    c = pl.program_id(1)
    nc = pl.num_programs(1)

    @pl.when(c == 0)
    def _():
        for bb in range(nbat):
            xpad_ref[bb, 0:CONV_PAD, :] = conv0_ref[bb]
            h_ref[bb] = h0_ref[bb]

    for bb in range(nbat):
        _ssd_chunk(bb, xbc_ref, dtr_ref, z_ref, cw_ref, cb_ref, dtb_ref, alog_ref, dsk_ref, gn_ref,
                   y_ref, xpad_ref, h_ref, cl, valid)

    @pl.when(c == nc - 1)
    def _():
        for bb in range(nbat):
            convo_ref[bb] = xpad_ref[bb, CONV_PAD + valid - (CONV_W - 1):CONV_PAD + valid, :]
            ho_ref[bb] = h_ref[bb]


def _ssd_chunk(bb, xbc_ref, dtr_ref, z_ref, cw_ref, cb_ref, dtb_ref, alog_ref, dsk_ref, gn_ref,
               y_ref, xpad_ref, h_ref, cl, valid):
    x = xbc_ref[bb]
    xpad_ref[bb, CONV_PAD:CONV_PAD + cl, :] = x
    w = cw_ref[...]
    conv = cb_ref[...] + x * w[CONV_W - 1:CONV_W, :]
    for back in range(1, CONV_W):
        conv = conv + xpad_ref[bb, pl.ds(CONV_PAD - back, cl), :] * w[CONV_W - 1 - back:CONV_W - back, :]

    xpad_ref[bb, 0:CONV_PAD, :] = xpad_ref[bb, cl:cl + CONV_PAD, :]
    xc = conv * _sigmoid(conv)
    xs = xc[:, :SSM_INNER]
    gw = SSM_STATE
    bm = [xc[:, SSM_INNER + g * gw:SSM_INNER + (g + 1) * gw].astype(BF16) for g in range(SSM_GROUPS)]
    cm = [xc[:, SSM_INNER + (SSM_GROUPS + g) * gw:SSM_INNER + (SSM_GROUPS + g + 1) * gw].astype(BF16)
          for g in range(SSM_GROUPS)]

    pre = dtr_ref[bb] + dtb_ref[...]
    dt = jnp.maximum(pre, 0.0) + jnp.log1p(jnp.exp(-jnp.abs(pre)))
    rowl = lax.broadcasted_iota(jnp.int32, dt.shape, 0)
    if valid < cl:
        dt = jnp.where(rowl < valid, dt, 0.0)
    da = dt * (-jnp.exp(alog_ref[...]))
    cum = _cumsum_rows(da)
    cum_t = cum.T
    cum_last = cum[cl - 1:cl, :]
    ecum = jnp.exp(cum)
    tail = jnp.exp(cum_last - cum)
    elast = jnp.exp(cum_last)

    ri = lax.broadcasted_iota(jnp.int32, (cl, cl), 0)
    ci = lax.broadcasted_iota(jnp.int32, (cl, cl), 1)
    causal = ci <= ri
    lane = lax.broadcasted_iota(jnp.int32, (cl, LANES), 1)
    lo = lane < SSM_HEAD_DIM
    row_lo = lax.broadcasted_iota(jnp.int32, (LANES, 1), 0) < SSM_HEAD_DIM
    heads_per_group = SSM_HEADS // SSM_GROUPS
    gmat = [_dot_nt(cm[g], bm[g]) for g in range(SSM_GROUPS)]

    def pick(cols, h0):
        return jnp.where(lo, cols[:, h0:h0 + 1], cols[:, h0 + 1:h0 + 2])

    ys = []
    for p in range(SSM_HEADS // 2):
        h0 = 2 * p
        g = h0 // heads_per_group
        xs_p = xs[:, p * LANES:(p + 1) * LANES]
        xdt = xs_p * pick(dt, h0)
        xdt_b = xdt.astype(BF16)
        parts = []
        for h in (h0, h0 + 1):
            seg = cum[:, h:h + 1] - cum_t[h:h + 1, :]
            dec = jnp.where(causal, jnp.exp(jnp.minimum(seg, 0.0)), 0.0)
            parts.append(_dot((gmat[g] * dec).astype(BF16), xdt_b))
        hp = h_ref[bb, p * LANES:(p + 1) * LANES, :]
        y_p = jnp.where(lo, parts[0], parts[1])
        y_p = y_p + _dot_nt(cm[g], hp.astype(BF16)) * pick(ecum, h0)
        y_p = y_p + dsk_ref[:, p * LANES:(p + 1) * LANES] * xs_p
        xw = (xdt * pick(tail, h0)).astype(BF16)
        decay_rows = jnp.where(row_lo, elast[:, h0:h0 + 1], elast[:, h0 + 1:h0 + 2])
        h_ref[bb, p * LANES:(p + 1) * LANES, :] = hp * decay_rows + _dot_tn(xw, bm[g])
        ys.append(y_p)

    y = jnp.concatenate(ys, axis=1)
    zz = z_ref[bb]
    yz = y * (zz * _sigmoid(zz))
    gwid = SSM_INNER // SSM_GROUPS
    normed = []
    for g in range(SSM_GROUPS):
        part = yz[:, g * gwid:(g + 1) * gwid]
        normed.append(part * lax.rsqrt(jnp.mean(part * part, axis=-1, keepdims=True) + EPS))
    y_ref[bb] = (jnp.concatenate(normed, axis=1) * gn_ref[...]).astype(y_ref.dtype)


def _ssd(xbc, dtr, z, conv0, h0, cw, cb, dtb, alog, dsk, gn, cl, valid, out_dtype):
    b, t, _ = xbc.shape
    nc = t // cl
    nbat = math.gcd(b, SSD_SEQS_PER_STEP)
    rows = lambda w: pl.BlockSpec((nbat, cl, w), lambda bi, c: (bi, c, 0))
    per_b = lambda shape: pl.BlockSpec((nbat,) + shape, lambda bi, c: (bi, 0, 0))
    hshape = (SSM_HEADS * SSM_HEAD_DIM, SSM_STATE)
    return pl.pallas_call(
        functools.partial(_ssd_kernel, cl=cl, valid=valid, nbat=nbat),
        grid=(b // nbat, nc),
        in_specs=[rows(CONV_DIM), rows(LANES), rows(SSM_INNER), per_b((CONV_PAD, CONV_DIM)),
                  per_b(hshape), _const(cw.shape), _const(cb.shape), _const(dtb.shape),
                  _const(alog.shape), _const(dsk.shape), _const(gn.shape)],
        out_specs=[rows(SSM_INNER), per_b((CONV_W - 1, CONV_DIM)), per_b(hshape)],
        out_shape=[jax.ShapeDtypeStruct((b, t, SSM_INNER), out_dtype),
                   jax.ShapeDtypeStruct((b, CONV_W - 1, CONV_DIM), F32),
                   jax.ShapeDtypeStruct((b,) + hshape, F32)],
        scratch_shapes=[pltpu.VMEM((nbat, CONV_PAD + cl, CONV_DIM), F32), pltpu.VMEM((nbat,) + hshape, F32)],
        compiler_params=_params("parallel", "arbitrary"),
        name="ssd",
    )(xbc, dtr, z, conv0, h0, cw, cb, dtb, alog, dsk, gn)


def _head_slice(h):
    return slice(h * X_HEAD_DIM, (h + 1) * X_HEAD_DIM)


def _outproj_body(x, att, y, wa_ref, wy_ref, g_ref, wq_ref, gq_ref):
    x1 = x + _dot(att.astype(BF16), wa_ref[...]) + _dot(y.astype(BF16), wy_ref[...])
    qx = _dot(_rms(x1, g_ref[...]).astype(BF16), wq_ref[...])
    gq = gq_ref[...]
    return x1, [_rms(qx[:, _head_slice(h)], gq) for h in range(X_HEADS)]


def _outproj_kernel(x_ref, att_ref, y_ref, wa_ref, wy_ref, g_ref, wq_ref, gq_ref, x1_ref, qx_ref):
    x1, q_heads = _outproj_body(x_ref[...], att_ref[...], y_ref[...], wa_ref, wy_ref, g_ref, wq_ref, gq_ref)
    x1_ref[...] = x1
    for h in range(X_HEADS):
        qx_ref[:, _head_slice(h)] = q_heads[h].astype(qx_ref.dtype)


def _outproj(x, att, y, w_att, w_y, g, wq, gq, tm, act_dtype):
    n = x.shape[0]
    row = lambda w: pl.BlockSpec((tm, w), lambda i: (i, 0))
    return pl.pallas_call(
        _outproj_kernel,
        grid=(n // tm,),
        in_specs=[row(D_MODEL), row(ATT_WIDTH), row(SSM_INNER), _const(w_att.shape), _const(w_y.shape),
                  _const(g.shape), _const(wq.shape), _const(gq.shape)],
        out_specs=[row(D_MODEL), row(X_WIDTH)],
        out_shape=[jax.ShapeDtypeStruct((n, D_MODEL), F32), jax.ShapeDtypeStruct((n, X_WIDTH), act_dtype)],
        compiler_params=_params("parallel"),
        name="outproj",
    )(x, att, y, w_att, w_y, g, wq, gq)


def _xattn_head(q, k, v):
    s = _dot_nt((q.astype(F32) * (X_HEAD_DIM ** -0.5)).astype(BF16), k.astype(BF16))
    p = jnp.exp(s - jnp.max(s, axis=-1, keepdims=True))
    return _dot(p.astype(BF16), v.astype(BF16)) / jnp.sum(p, axis=-1, keepdims=True)


def _xattn_kernel(q_ref, k_ref, v_ref, o_ref):
    for g in range(q_ref.shape[0]):
        for h in range(X_HEADS):
            sl = _head_slice(h)
            o_ref[g, :, sl] = _xattn_head(q_ref[g, :, sl], k_ref[g, :, sl], v_ref[g, :, sl]).astype(o_ref.dtype)


def _xattn(q, mk, mv, tq, act_dtype):
    b, t, _ = q.shape
    m = mk.shape[1]
    gb = XATTN_SEQS_PER_STEP if (t == tq and b % XATTN_SEQS_PER_STEP == 0 and tq < LANES) else 1
    qspec = pl.BlockSpec((gb, tq, X_WIDTH), lambda bi, i: (bi, i, 0))
    kvspec = pl.BlockSpec((gb, m, X_WIDTH), lambda bi, i: (bi, 0, 0))
    return pl.pallas_call(
        _xattn_kernel,
        grid=(b // gb, t // tq),
        in_specs=[qspec, kvspec, kvspec],
        out_specs=qspec,
        out_shape=jax.ShapeDtypeStruct((b, t, X_WIDTH), act_dtype),
        compiler_params=_params("parallel", "parallel"),
        name="xattn",
    )(q, mk, mv)


def _mlp_body(x1, o, wo_ref, g_ref, wu_ref, wd_ref):
    x2 = x1 + _dot(o.astype(BF16), wo_ref[...])
    hm = _rms(x2, g_ref[...]).astype(BF16)
    acc = x2
    for cidx in range(D_FF // FF_CHUNK):
        sl = slice(cidx * FF_CHUNK, (cidx + 1) * FF_CHUNK)
        up = jnp.maximum(_dot(hm, wu_ref[:, sl]), 0.0)
        acc = acc + _dot((up * up).astype(BF16), wd_ref[sl, :])
    return acc


def _mlp_kernel(x1_ref, o_ref, wo_ref, g_ref, wu_ref, wd_ref, out_ref):
    out_ref[...] = _mlp_body(x1_ref[...], o_ref[...], wo_ref, g_ref, wu_ref, wd_ref)


def _mlp(x1, o, wo, g, wu, wd, tm):
    n = x1.shape[0]
    row = lambda w: pl.BlockSpec((tm, w), lambda i: (i, 0))
    return pl.pallas_call(
        _mlp_kernel,
        grid=(n // tm,),
        in_specs=[row(D_MODEL), row(X_WIDTH), _const_single(wo.shape), _const(g.shape),
                  _const_single(wu.shape), _const_single(wd.shape)],
        out_specs=row(D_MODEL),
        out_shape=jax.ShapeDtypeStruct((n, D_MODEL), F32),
        compiler_params=_params("parallel"),
        name="mlp",
    )(x1, o, wo, g, wu, wd)


def _post_kernel(x_ref, att_ref, y_ref, mk_ref, mv_ref, wa_ref, wy_ref, gx_ref, wq_ref, gq_ref,
                 wo_ref, gm_ref, wu_ref, wd_ref, out_ref):
    x1, q_heads = _outproj_body(x_ref[...], att_ref[...], y_ref[...], wa_ref, wy_ref, gx_ref, wq_ref, gq_ref)
    o = jnp.concatenate([_xattn_head(q_heads[h], mk_ref[0, :, _head_slice(h)], mv_ref[0, :, _head_slice(h)])
                         for h in range(X_HEADS)], axis=1)
    out_ref[...] = _mlp_body(x1, o, wo_ref, gm_ref, wu_ref, wd_ref)


def _post(x, att, y, mk, mv, wts, tm, t):
    n = x.shape[0]
    tiles_per_seq = t // tm
    m = mk.shape[1]
    row = lambda w: pl.BlockSpec((tm, w), lambda i: (i, 0))
    mem = pl.BlockSpec((1, m, X_WIDTH), lambda i: (i // tiles_per_seq, 0, 0))
    names = ("w_out_att", "w_out_ssm", "ln_x_g", "wq_x", "qx_norm_g", "wo_x", "ln_mlp_g", "w_up", "w_down")
    big = ("w_up", "w_down")
    weights = [wts[k] for k in names]
    return pl.pallas_call(
        _post_kernel,
        grid=(n // tm,),
        in_specs=[row(D_MODEL), row(ATT_WIDTH), row(SSM_INNER), mem, mem]
                 + [(_const_single if k in big else _const)(wts[k].shape) for k in names],
        out_specs=row(D_MODEL),
        out_shape=jax.ShapeDtypeStruct((n, D_MODEL), F32),
        compiler_params=_params("parallel"),
        name="post",
    )(x, att, y, mk, mv, *weights)


def _memkv_kernel(m_ref, g_ref, wk_ref, wv_ref, gk_ref, mk_ref, mv_ref):
    mn = _rms(m_ref[...], g_ref[...]).astype(BF16)
    mk = _dot(mn, wk_ref[...])
    gk = gk_ref[...]
    for h in range(X_HEADS):
        sl = slice(h * X_HEAD_DIM, (h + 1) * X_HEAD_DIM)
        mk_ref[:, sl] = _rms(mk[:, sl], gk)
    mv_ref[...] = _dot(mn, wv_ref[...])


def _memkv(mem, g, wk, wv, gk, tm):
    n = mem.shape[0]
    row = lambda w: pl.BlockSpec((tm, w), lambda i: (i, 0))
    return pl.pallas_call(
        _memkv_kernel,
        grid=(n // tm,),
        in_specs=[row(D_MODEL), _const(g.shape), _const(wk.shape), _const(wv.shape), _const(gk.shape)],
        out_specs=[row(X_WIDTH), row(X_WIDTH)],
        out_shape=[jax.ShapeDtypeStruct((n, X_WIDTH), F32)] * 2,
        compiler_params=_params("parallel"),
        name="memkv",
    )(mem, g, wk, wv, gk)


def _rope_tables(pos):
    half = HEAD_DIM // 2
    inv_freq = ROPE_THETA ** (-jnp.arange(half, dtype=F32) / half)
    ang = pos.astype(F32)[:, None] * inv_freq[None, :]
    return jnp.cos(ang).T, jnp.sin(ang).T


def _row_tile(n, want):
    tm = min(n, want)
    assert n % tm == 0
    return tm


def _layer(x, nb, cos_t, sin_t, conv0, h0, mem_k, mem_v, wts, attend, tm, act_dtype):
    b, t, _ = x.shape
    n = b * t
    xf = x.reshape(n, D_MODEL)
    qt, kt, vt, z, xbc, dtr = _inproj(xf, nb, wts["ln_mix_g"], wts["w_qkv_t"], wts["w_rest"], wts["w_dt"],
                                      wts["q_norm_g"], wts["k_norm_g"], cos_t, sin_t, tm)
    att, extras = attend(qt, kt, vt)

    cl = SSD_CHUNK
    tp = -(-t // cl) * cl
    pad3 = lambda a: jnp.pad(a.reshape(b, t, -1), ((0, 0), (0, tp - t), (0, 0)))
    y, new_conv, new_h = _ssd(pad3(xbc), pad3(dtr), pad3(z), conv0, h0, wts["conv_w"], wts["conv_b"],
                              wts["dt_bias"], wts["a_log"], wts["d_skip"], wts["ssm_norm_g"],
                              cl, min(t, cl) if tp != t else cl, act_dtype)
    y = y[:, :t].reshape(n, SSM_INNER)

    if t % tm == 0:
        out = _post(xf, att, y, mem_k, mem_v, wts, tm, t)
    else:
        x1, qx = _outproj(xf, att, y, wts["w_out_att"], wts["w_out_ssm"], wts["ln_x_g"],
                          wts["wq_x"], wts["qx_norm_g"], tm, act_dtype)
        o = _xattn(qx.reshape(b, t, X_WIDTH), mem_k, mem_v, _row_tile(t, 512), act_dtype)
        out = _mlp(x1, o.reshape(n, X_WIDTH), wts["wo_x"], wts["ln_mlp_g"], wts["w_up"], wts["w_down"], tm)
    return out.reshape(b, t, D_MODEL), extras, new_conv, new_h


def kernel(x_prompt, x_sample, mem_prompt, cache_k, cache_v, page_table, state_conv, state_ssm,
           cache_mem_k, cache_mem_v, ln_mix_g, w_in, q_norm_g, k_norm_g, conv_w, conv_b, dt_bias,
           a_log, d_skip, ssm_norm_g, w_out, ln_x_g, ln_mem_g, wq_x, wk_x, wv_x, qx_norm_g,
           kx_norm_g, wo_x, ln_mlp_g, w_up, w_down):
    depth = w_in.shape[0]
    assert depth == 1, "single-layer step"
    bp, tp, _ = x_prompt.shape
    bs, ts, _ = x_sample.shape
    page = cache_k.shape[2]
    past_len = page_table.shape[1] * page
    assert tp % MOBA_BLOCK == 0 and past_len % MOBA_BLOCK == 0 and MOBA_BLOCK % page == 0
    assert ts <= MOBA_BLOCK and ts <= SSD_CHUNK and ts >= CONV_W - 1
    l = 0
    qkv_cols = 3 * ATT_WIDTH
    main_cols = qkv_cols + SSM_INNER + CONV_DIM

    def lanes(vec, width=LANES):
        return jnp.pad(vec.astype(F32), (0, width - vec.shape[0])).reshape(1, width)

    def dim_gain(vec):
        return jnp.broadcast_to(vec.astype(F32)[:, None], (HEAD_DIM, LANES))

    wts = {
        "ln_mix_g": ln_mix_g[l].reshape(1, D_MODEL),
        "w_qkv_t": w_in[l][:, :qkv_cols].T.astype(BF16),
        "w_rest": w_in[l][:, qkv_cols:main_cols].astype(BF16),
        "w_dt": jnp.pad(w_in[l][:, main_cols:], ((0, 0), (0, LANES - SSM_HEADS))).astype(BF16),
        "q_norm_g": dim_gain(q_norm_g[l]),
        "k_norm_g": dim_gain(k_norm_g[l]),
        "conv_w": conv_w[l],
        "conv_b": conv_b[l].reshape(1, CONV_DIM),
        "dt_bias": lanes(dt_bias[l]),
        "a_log": lanes(a_log[l]),
        "d_skip": jnp.repeat(d_skip[l].astype(F32), SSM_HEAD_DIM).reshape(1, SSM_INNER),
        "ssm_norm_g": ssm_norm_g[l].reshape(1, SSM_INNER),
        "w_out_att": w_out[l][:ATT_WIDTH].astype(BF16),
        "w_out_ssm": w_out[l][ATT_WIDTH:].astype(BF16),
        "ln_x_g": ln_x_g[l].reshape(1, D_MODEL),
        "wq_x": wq_x[l].astype(BF16),
        "qx_norm_g": qx_norm_g[l].reshape(1, X_HEAD_DIM),
        "wo_x": wo_x[l].astype(BF16),
        "ln_mlp_g": ln_mlp_g[l].reshape(1, D_MODEL),
        "w_up": w_up[l].astype(BF16),
        "w_down": w_down[l].astype(BF16),
    }

    mem_rows = mem_prompt.reshape(-1, D_MODEL)
    mk_p, mv_p = _memkv(mem_rows, ln_mem_g[l].reshape(1, D_MODEL), wk_x[l].astype(BF16),
                        wv_x[l].astype(BF16), kx_norm_g[l].reshape(1, X_HEAD_DIM),
                        _row_tile(mem_rows.shape[0], 512))
    mem_len = mem_prompt.shape[1]
    mk_p = mk_p.reshape(bp, mem_len, X_WIDTH)
    mv_p = mv_p.reshape(bp, mem_len, X_WIDTH)
    cos_p, sin_p = _rope_tables(jnp.arange(tp, dtype=jnp.int32))
    conv0_p = jnp.zeros((bp, CONV_PAD, CONV_DIM), F32)
    h0_p = jnp.zeros((bp, SSM_HEADS * SSM_HEAD_DIM, SSM_STATE), F32)

    def attend_prompt(qt, kt, vt):
        heads_t = lambda a: a.reshape(1, bp, ATT_HEADS, HEAD_DIM, tp).transpose(0, 1, 4, 2, 3)
        return _moba_prompt(qt, kt, vt).reshape(bp * tp, ATT_WIDTH), (heads_t(kt), heads_t(vt))

    y_p, (k_p, v_p), conv_p, ssm_p = _layer(x_prompt, bp, cos_p, sin_p, conv0_p, h0_p, mk_p, mv_p, wts,
                                            attend_prompt, _row_tile(tp, 512), BF16)

    cos_s, sin_s = _rope_tables(past_len + jnp.arange(ts, dtype=jnp.int32))
    cos_s = jnp.tile(cos_s, (1, bs))
    sin_s = jnp.tile(sin_s, (1, bs))
    conv0_s = jnp.pad(state_conv[l], ((0, 0), (CONV_PAD - (CONV_W - 1), 0), (0, 0)))
    h0_s = state_ssm[l].reshape(bs, SSM_HEADS * SSM_HEAD_DIM, SSM_STATE)
    cache_kt = cache_k.transpose(0, 1, 3, 4, 2)
    cache_vt = cache_v.transpose(0, 1, 3, 4, 2)

    def attend_sample(qt, kt, vt):
        rows = lambda a: a[0].T.reshape(bs, ts, ATT_WIDTH)
        k_new, v_new = rows(kt), rows(vt)
        o = _sample_moba(page_table, rows(qt), k_new, v_new, cache_kt, cache_vt)
        heads = lambda a: a.reshape(1, bs, ts, ATT_HEADS, HEAD_DIM)
        return o.reshape(bs * ts, ATT_WIDTH), (heads(k_new), heads(v_new))

    y_s, (k_s, v_s), conv_s, ssm_s = _layer(
        x_sample, 1, cos_s, sin_s, conv0_s, h0_s, cache_mem_k[l].reshape(bs, -1, X_WIDTH),
        cache_mem_v[l].reshape(bs, -1, X_WIDTH), wts, attend_sample, bs * ts, F32)

    state = lambda a, b: a.reshape(1, b, SSM_HEADS, SSM_HEAD_DIM, SSM_STATE)
    return (y_p, y_s, k_p, v_p, conv_p[None], state(ssm_p, bp),
            mk_p.reshape(1, bp, mem_len, X_HEADS, X_HEAD_DIM), mv_p.reshape(1, bp, mem_len, X_HEADS, X_HEAD_DIM),
            k_s, v_s, conv_s[None], state(ssm_s, bs))
```

```python
import functools
import math

import jax
import jax.numpy as jnp
from jax import lax
from jax.experimental import pallas as pl
from jax.experimental.pallas import tpu as pltpu

F32 = jnp.float32
BF16 = jnp.bfloat16

D_MODEL = 1024
ATT_HEADS = 8
HEAD_DIM = 64
ATT_WIDTH = ATT_HEADS * HEAD_DIM
MOBA_BLOCK = 256
TOPK = 3
ROPE_THETA = 10000.0
SSM_HEADS = 8
SSM_HEAD_DIM = 64
SSM_INNER = SSM_HEADS * SSM_HEAD_DIM
SSM_GROUPS = 2
SSM_STATE = 128
CONV_W = 4
CONV_DIM = SSM_INNER + 2 * SSM_GROUPS * SSM_STATE
SSD_CHUNK = 128
X_HEADS = 4
X_HEAD_DIM = 128
X_WIDTH = X_HEADS * X_HEAD_DIM
D_FF = 4 * D_MODEL
EPS = 1e-6

LANES = 128
CONV_PAD = 8
VMEM_LIMIT = 56 * 1024 * 1024
NEG = -1e30
LOG2E = 1.4426950408889634
MOBA_GROUP = 4
ONES_ROWS = 16
FF_CHUNK = 1024
XATTN_SEQS_PER_STEP = 8
CAST_BLOCK_BYTES = 4 * 1024 * 1024
SSD_MIN_CHUNK = 16
SSD_SEQS_PER_STEP = 4
PAGES_PER_STEP = 16


def _dot(a, b):
    return jnp.dot(a, b, preferred_element_type=F32)


def _dot_nt(a, b):
    return lax.dot_general(a, b, (((1,), (1,)), ((), ())), preferred_element_type=F32)


def _dot_tn(a, b):
    return lax.dot_general(a, b, (((0,), (0,)), ((), ())), preferred_element_type=F32)


def _split3(x):
    hi = x.astype(BF16)
    r = x - hi.astype(F32)
    mid = r.astype(BF16)
    lo = (r - mid.astype(F32)).astype(BF16)
    return hi, mid, lo


def _dot_f32(a, b, dot=None):
    dot = dot or _dot
    a0, a1, a2 = _split3(a)
    b0, b1, b2 = _split3(b)
    return (dot(a0, b0) + (dot(a0, b1) + dot(a1, b0)) + (dot(a1, b1) + dot(a0, b2) + dot(a2, b0)))


def _rms(x, g):
    ms = jnp.mean(x * x, axis=-1, keepdims=True)
    return x * lax.rsqrt(ms + EPS) * g


def _sigmoid(x):
    return 1.0 / (1.0 + jnp.exp(-x))


def _params(*sem):
    return pltpu.CompilerParams(dimension_semantics=sem, vmem_limit_bytes=VMEM_LIMIT)


def _const(shape):
    return pl.BlockSpec(shape, lambda *_: (0,) * len(shape))


def _const_single(shape):
    return pl.BlockSpec(shape, lambda *_: (0,) * len(shape), pipeline_mode=pl.Buffered(1))


def _inproj_kernel(x_ref, g_ref, wqkv_ref, wrest_ref, wdt_ref, gq_ref, gk_ref, cos_ref, sin_ref,
                   q_ref, k_ref, v_ref, z_ref, xbc_ref, dt_ref):
    hn = _rms(x_ref[...], g_ref[...]).astype(BF16)
    tm = hn.shape[0]
    cos = cos_ref[...]
    sin = sin_ref[...]
    half = HEAD_DIM // 2

    def proj_t(idx):
        return _dot_nt(wqkv_ref[idx * ATT_WIDTH:(idx + 1) * ATT_WIDTH, :], hn)

    def head_norm_rot(p, g_ref_):
        p3 = p.reshape(ATT_HEADS, HEAD_DIM, tm)
        gain = g_ref_[:, 0:1]
        pn = p3 * lax.rsqrt(jnp.mean(p3 * p3, axis=1, keepdims=True) + EPS) * gain
        x1 = pn[:, :half]
        x2 = pn[:, half:]
        rot = jnp.concatenate([x1 * cos - x2 * sin, x2 * cos + x1 * sin], axis=1)
        return rot.reshape(ATT_WIDTH, tm)

    q_ref[0] = head_norm_rot(proj_t(0), gq_ref)
    k_ref[0] = head_norm_rot(proj_t(1), gk_ref)
    v_ref[0] = proj_t(2)
    z_ref[...] = _dot(hn, wrest_ref[:, 0:SSM_INNER])
    xbc_ref[...] = _dot(hn, wrest_ref[:, SSM_INNER:SSM_INNER + CONV_DIM])
    dt_ref[...] = _dot(hn, wdt_ref[...])


def _inproj(x, nb, g, w_qkv_t, w_rest, w_dt, gq, gk, cos_t, sin_t, tm):
    n = x.shape[0]
    cols = n // nb
    tpb = cols // tm
    row = lambda w: pl.BlockSpec((tm, w), lambda i: (i, 0))
    tab = pl.BlockSpec((HEAD_DIM // 2, tm), lambda i: (0, i % tpb))
    tspec = pl.BlockSpec((1, ATT_WIDTH, tm), lambda i: (i // tpb, 0, i % tpb))
    outs = ([jax.ShapeDtypeStruct((nb, ATT_WIDTH, cols), F32)] * 3
            + [jax.ShapeDtypeStruct((n, w), F32) for w in (SSM_INNER, CONV_DIM, LANES)])
    return pl.pallas_call(
        _inproj_kernel,
        grid=(n // tm,),
        in_specs=[row(D_MODEL), _const(g.shape), _const_single(w_qkv_t.shape), _const_single(w_rest.shape),
                  _const(w_dt.shape), _const(gq.shape), _const(gk.shape), tab, tab],
        out_specs=[tspec, tspec, tspec, row(SSM_INNER), row(CONV_DIM), row(LANES)],
        out_shape=outs,
        compiler_params=_params("parallel"),
        name="inproj",
    )(x, g, w_qkv_t, w_rest, w_dt, gq, gk, cos_t, sin_t)


def _topk_keep(gate, n_valid, axis):
    idx = lax.broadcasted_iota(jnp.int32, gate.shape, axis)
    cnt = jnp.zeros(gate.shape, jnp.int32)
    for jp in range(gate.shape[axis]):
        gj = lax.slice_in_dim(gate, jp, jp + 1, axis=axis)
        beats = (gj > gate) | ((gj == gate) & (jp < idx))
        cnt = cnt + jnp.where(beats, jnp.where(jp < n_valid, 1, 0), 0)
    return (idx < n_valid) & (cnt < TOPK)


def _moba_kernel(qt_ref, kt_ref, vt_ref, o_ref, kb_ref, vb_ref, km_ref, *, nblk):
    bq = MOBA_BLOCK
    km_ref[...] = jnp.zeros(km_ref.shape, F32)
    for j in range(nblk):
        kj = kt_ref[0, :, j * bq:(j + 1) * bq].T
        kb_ref[j] = kj.astype(BF16)
        km_ref[j:j + 1, :] = jnp.mean(kj, axis=0, keepdims=True)
        vj = vt_ref[0, :, j * bq:(j + 1) * bq].astype(BF16)
        for h in range(2):
            vb_ref[j, h, 0:HEAD_DIM, :] = vj[h * HEAD_DIM:(h + 1) * HEAD_DIM, :]
            vb_ref[j, h, HEAD_DIM:, :] = jnp.ones((ONES_ROWS, bq), BF16)
    kmean = km_ref[...]

    keyi = lax.broadcasted_iota(jnp.int32, (bq, bq), 0)
    qryi = lax.broadcasted_iota(jnp.int32, (bq, bq), 1)
    causal = jnp.where(keyi <= qryi, 0.0, NEG)
    row_lo = lax.broadcasted_iota(jnp.int32, (LANES, bq), 0) < HEAD_DIM
    qscale = (HEAD_DIM ** -0.5) * LOG2E

    order = [blk for pair in zip(range(nblk), reversed(range(nblk))) for blk in pair][:nblk]
    for g0 in range(0, nblk, MOBA_GROUP):
        owns = order[g0:g0 + MOBA_GROUP]
        work = []
        for own in owns:
            qt = qt_ref[0, :, own * bq:(own + 1) * bq]
            for h in range(2):
                qh = jnp.where(row_lo, 0.0, qt) if h else jnp.where(row_lo, qt, 0.0)
                qsb = (qh * qscale).astype(BF16)
                bias = None
                if own > TOPK:
                    keep = _topk_keep(_dot_f32(kmean, qh), own, 0)
                    bias = jnp.where(keep, 0.0, NEG)
                tiles = []
                m = None
                for j in range(own + 1):
                    s = _dot(kb_ref[j], qsb)
                    if j == own:
                        s = s + causal
                    elif bias is not None:
                        s = s + bias[j:j + 1, :]
                    tiles.append(s)
                    mj = jnp.max(s, axis=0, keepdims=True)
                    m = mj if m is None else jnp.maximum(m, mj)
                work.append((own, h, tiles, m))
        outs = {}
        for own, h, tiles, m in work:
            acc = jnp.zeros((HEAD_DIM + ONES_ROWS, bq), F32)
            for j in range(own + 1):
                p = jnp.exp2(tiles[j] - m)
                acc = acc + _dot(vb_ref[j, h], p.astype(BF16))
            outs[(own, h)] = acc[0:HEAD_DIM] / acc[HEAD_DIM:HEAD_DIM + 1]
        for own in owns:
            o_t = jnp.concatenate([outs[(own, 0)], outs[(own, 1)]], axis=0)
            o_ref[0, own * bq:(own + 1) * bq, :] = o_t.T.astype(o_ref.dtype)


def _moba_prompt(qt, kt, vt):
    b, _, t = qt.shape
    nblk = t // MOBA_BLOCK
    npad = max(16, nblk)
    npair = ATT_WIDTH // LANES
    spec = pl.BlockSpec((1, LANES, t), lambda bi, p: (bi, p, 0))
    return pl.pallas_call(
        functools.partial(_moba_kernel, nblk=nblk),
        grid=(b, npair),
        in_specs=[spec, spec, spec],
        out_specs=pl.BlockSpec((1, t, LANES), lambda bi, p: (bi, 0, p)),
        out_shape=jax.ShapeDtypeStruct((b, t, ATT_WIDTH), BF16),
        scratch_shapes=[pltpu.VMEM((nblk, MOBA_BLOCK, LANES), BF16),
                        pltpu.VMEM((nblk, 2, HEAD_DIM + ONES_ROWS, MOBA_BLOCK), BF16),
                        pltpu.VMEM((npad, LANES), F32)],
        compiler_params=_params("parallel", "parallel"),
        name="moba_prompt",
    )(qt, kt, vt)


def _expand_heads(q):
    t = q.shape[0]
    tiled = jnp.concatenate([q] * ATT_HEADS, axis=0)
    rowh = lax.broadcasted_iota(jnp.int32, tiled.shape, 0) // t
    laneh = lax.broadcasted_iota(jnp.int32, tiled.shape, 1) // HEAD_DIM
    return jnp.where(rowh == laneh, tiled, 0.0)


def _sample_moba_kernel(pt_ref, q_ref, kn_ref, vn_ref, *rest, page, npg, nblk):
    del pt_ref
    k_refs = rest[:npg]
    v_refs = rest[npg:2 * npg]
    o_ref, km_ref, m_ref, l_ref, oblk_ref = rest[2 * npg:]
    c = pl.program_id(1)
    t = q_ref.shape[1]
    rows = ATT_HEADS * t
    pages_per_block = MOBA_BLOCK // page
    blocks_per_step = npg // pages_per_block

    qe = _expand_heads(q_ref[0])
    qb = (qe * ((HEAD_DIM ** -0.5) * LOG2E)).astype(BF16)
    blk_lane = lax.broadcasted_iota(jnp.int32, (ATT_WIDTH, LANES), 1)

    @pl.when(c == 0)
    def _():
        km_ref[...] = jnp.zeros(km_ref.shape, F32)

    tiles = []
    for jj in range(blocks_per_step):
        ksum = None
        for u in range(pages_per_block):
            kt = k_refs[jj * pages_per_block + u][...].reshape(ATT_WIDTH, page)
            ksum = kt if ksum is None else ksum + kt
            tiles.append(_dot(qb, kt.astype(BF16)))
        kmean = jnp.sum(ksum, axis=1, keepdims=True) * (1.0 / MOBA_BLOCK)
        km_ref[...] = jnp.where(blk_lane == c * blocks_per_step + jj, kmean, km_ref[...])
    probs = []
    for jj in range(blocks_per_step):
        mine = tiles[jj * pages_per_block:(jj + 1) * pages_per_block]
        m = jnp.max(mine[0], axis=1, keepdims=True)
        for s in mine[1:]:
            m = jnp.maximum(m, jnp.max(s, axis=1, keepdims=True))
        l = jnp.zeros((rows, 1), F32)
        for s in mine:
            p = jnp.exp2(s - m)
            l = l + jnp.sum(p, axis=1, keepdims=True)
            probs.append(p.astype(BF16))
        m_ref[c * blocks_per_step + jj] = jnp.broadcast_to(m, (rows, LANES))
        l_ref[c * blocks_per_step + jj] = jnp.broadcast_to(l, (rows, LANES))
    for jj in range(blocks_per_step):
        o = jnp.zeros((rows, ATT_WIDTH), F32)
        for u in range(pages_per_block):
            vt = v_refs[jj * pages_per_block + u][...].reshape(ATT_WIDTH, page).astype(BF16)
            o = o + _dot_nt(probs[jj * pages_per_block + u], vt)
        oblk_ref[c * blocks_per_step + jj] = o

    @pl.when(c == pl.num_programs(1) - 1)
    def _():
        keep = _topk_keep(_dot_f32(qe, km_ref[:, 0:nblk]), nblk, 1)
        s_own = _dot_nt(qb, kn_ref[0].astype(BF16))
        qpos = lax.broadcasted_iota(jnp.int32, s_own.shape, 0) % t
        kpos = lax.broadcasted_iota(jnp.int32, s_own.shape, 1)
        s_own = jnp.where(kpos <= qpos, s_own, NEG)
        lane_j = lax.broadcasted_iota(jnp.int32, (rows, nblk), 1)
        m_all = jnp.zeros((rows, nblk), F32)
        l_all = jnp.zeros((rows, nblk), F32)
        for j in range(nblk):
            m_all = jnp.where(lane_j == j, m_ref[j][:, 0:nblk], m_all)
            l_all = jnp.where(lane_j == j, l_ref[j][:, 0:nblk], l_all)
        mx = jnp.maximum(jnp.max(s_own, axis=1, keepdims=True),
                         jnp.max(jnp.where(keep, m_all, NEG), axis=1, keepdims=True))
        p_own = jnp.exp2(s_own - mx)
        w = jnp.where(keep, jnp.exp2(jnp.minimum(m_all - mx, 0.0)), 0.0)
        den = jnp.sum(p_own, axis=1, keepdims=True) + jnp.sum(w * l_all, axis=1, keepdims=True)
        vn = vn_ref[0]
        num = jnp.zeros((rows, ATT_WIDTH), F32)
        for tt in range(t):
            num = num + p_own[:, tt:tt + 1] * vn[tt:tt + 1, :]
        for j in range(nblk):
            num = num + w[:, j:j + 1] * oblk_ref[j]
        full = num / den
        laneh = lax.broadcasted_iota(jnp.int32, (t, ATT_WIDTH), 1) // HEAD_DIM
        out = jnp.zeros((t, ATT_WIDTH), F32)
        for h in range(ATT_HEADS):
            out = out + jnp.where(laneh == h, full[h * t:(h + 1) * t, :], 0.0)
        o_ref[0] = out


def _sample_moba(page_table, q, k_new, v_new, cache_kt, cache_vt):
    b, t, _ = q.shape
    page = cache_kt.shape[4]
    n_pages = page_table.shape[1]
    npg = PAGES_PER_STEP
    nsteps = n_pages // npg
    nblk = n_pages * page // MOBA_BLOCK
    assert nblk <= LANES and n_pages % npg == 0
    rows = ATT_HEADS * t
    per_b = pl.BlockSpec((1, t, ATT_WIDTH), lambda bi, c, pt: (bi, 0, 0))

    def pspec(pi):
        return pl.BlockSpec((None, None, ATT_HEADS, HEAD_DIM, page),
                            lambda bi, c, pt: (0, pt[bi, c * npg + pi], 0, 0, 0))

    grid_spec = pltpu.PrefetchScalarGridSpec(
        num_scalar_prefetch=1,
        grid=(b, nsteps),
        in_specs=[per_b, per_b, per_b] + [pspec(pi) for pi in range(npg)] * 2,
        out_specs=per_b,
        scratch_shapes=[pltpu.VMEM((ATT_WIDTH, LANES), F32),
                        pltpu.VMEM((nblk, rows, LANES), F32),
                        pltpu.VMEM((nblk, rows, LANES), F32),
                        pltpu.VMEM((nblk, rows, ATT_WIDTH), F32)],
    )
    return pl.pallas_call(
        functools.partial(_sample_moba_kernel, page=page, npg=npg, nblk=nblk),
        grid_spec=grid_spec,
        out_shape=jax.ShapeDtypeStruct((b, t, ATT_WIDTH), F32),
        compiler_params=_params("parallel", "arbitrary"),
        name="sample_moba",
    )(page_table, q, k_new, v_new, *([cache_kt] * npg), *([cache_vt] * npg))


def _cumsum_rows(x):
    n = x.shape[0]
    row = lax.broadcasted_iota(jnp.int32, x.shape, 0)
    step = 1
    while step < n:
        x = x + jnp.where(row >= step, pltpu.roll(x, step, 0), 0.0)
        step *= 2
    return x


def _ssd_kernel(xbc_ref, dtr_ref, z_ref, conv0_ref, h0_ref, cw_ref, cb_ref, dtb_ref, alog_ref,
                dsk_ref, gn_ref, y_ref, convo_ref, ho_ref, xpad_ref, h_ref, *, cl, valid, nbat):
    c = pl.program_id(1)
    nc = pl.num_programs(1)

    @pl.when(c == 0)
    def _():
        for bb in range(nbat):
            xpad_ref[bb, 0:CONV_PAD, :] = conv0_ref[bb]
            h_ref[bb] = h0_ref[bb]

    for bb in range(nbat):
        _ssd_chunk(bb, xbc_ref, dtr_ref, z_ref, cw_ref, cb_ref, dtb_ref, alog_ref, dsk_ref, gn_ref,
                   y_ref, xpad_ref, h_ref, cl, valid)

    @pl.when(c == nc - 1)
    def _():
        for bb in range(nbat):
            convo_ref[bb] = xpad_ref[bb, CONV_PAD + valid - (CONV_W - 1):CONV_PAD + valid, :]
            ho_ref[bb] = h_ref[bb]


def _ssd_chunk(bb, xbc_ref, dtr_ref, z_ref, cw_ref, cb_ref, dtb_ref, alog_ref, dsk_ref, gn_ref,
               y_ref, xpad_ref, h_ref, cl, valid):
    x = xbc_ref[bb]
    xpad_ref[bb, CONV_PAD:CONV_PAD + cl, :] = x
    w = cw_ref[...]
    conv = cb_ref[...] + x * w[CONV_W - 1:CONV_W, :]
    for back in range(1, CONV_W):
        conv = conv + xpad_ref[bb, pl.ds(CONV_PAD - back, cl), :] * w[CONV_W - 1 - back:CONV_W - back, :]

    xpad_ref[bb, 0:CONV_PAD, :] = xpad_ref[bb, cl:cl + CONV_PAD, :]
    xc = conv * _sigmoid(conv)
    xs = xc[:, :SSM_INNER]
    gw = SSM_STATE
    bm = [xc[:, SSM_INNER + g * gw:SSM_INNER + (g + 1) * gw].astype(BF16) for g in range(SSM_GROUPS)]
    cm = [xc[:, SSM_INNER + (SSM_GROUPS + g) * gw:SSM_INNER + (SSM_GROUPS + g + 1) * gw].astype(BF16)
          for g in range(SSM_GROUPS)]

    pre = dtr_ref[bb] + dtb_ref[...]
    dt = jnp.maximum(pre, 0.0) + jnp.log1p(jnp.exp(-jnp.abs(pre)))
    rowl = lax.broadcasted_iota(jnp.int32, dt.shape, 0)
    if valid < cl:
        dt = jnp.where(rowl < valid, dt, 0.0)
    da = dt * (-jnp.exp(alog_ref[...]))
    cum = _cumsum_rows(da)
    if cl < LANES:
        cum_t = jnp.concatenate([cum, jnp.zeros((LANES - cl, LANES), F32)], axis=0).T[:, 0:cl]
    else:
        cum_t = cum.T
    cum_last = cum[cl - 1:cl, :]
    ecum = jnp.exp(cum)
    tail = jnp.exp(cum_last - cum)
    elast = jnp.exp(cum_last)

    ri = lax.broadcasted_iota(jnp.int32, (cl, cl), 0)
    ci = lax.broadcasted_iota(jnp.int32, (cl, cl), 1)
    causal = ci <= ri
    lane = lax.broadcasted_iota(jnp.int32, (cl, LANES), 1)
    lo = lane < SSM_HEAD_DIM
    row_lo = lax.broadcasted_iota(jnp.int32, (LANES, 1), 0) < SSM_HEAD_DIM
    heads_per_group = SSM_HEADS // SSM_GROUPS
    gmat = [_dot_nt(cm[g], bm[g]) for g in range(SSM_GROUPS)]

    def pick(cols, h0):
        return jnp.where(lo, cols[:, h0:h0 + 1], cols[:, h0 + 1:h0 + 2])

    ys = []
    for p in range(SSM_HEADS // 2):
        h0 = 2 * p
        g = h0 // heads_per_group
        xs_p = xs[:, p * LANES:(p + 1) * LANES]
        xdt = xs_p * pick(dt, h0)
        xdt_b = xdt.astype(BF16)
        parts = []
        for h in (h0, h0 + 1):
            seg = cum[:, h:h + 1] - cum_t[h:h + 1, :]
            dec = jnp.where(causal, jnp.exp(jnp.minimum(seg, 0.0)), 0.0)
            parts.append(_dot((gmat[g] * dec).astype(BF16), xdt_b))
        hp = h_ref[bb, p * LANES:(p + 1) * LANES, :]
        y_p = jnp.where(lo, parts[0], parts[1])
        y_p = y_p + _dot_nt(cm[g], hp.astype(BF16)) * pick(ecum, h0)
        y_p = y_p + dsk_ref[:, p * LANES:(p + 1) * LANES] * xs_p
        xw = (xdt * pick(tail, h0)).astype(BF16)
        decay_rows = jnp.where(row_lo, elast[:, h0:h0 + 1], elast[:, h0 + 1:h0 + 2])
        h_ref[bb, p * LANES:(p + 1) * LANES, :] = hp * decay_rows + _dot_tn(xw, bm[g])
        ys.append(y_p)

    y = jnp.concatenate(ys, axis=1)
    zz = z_ref[bb]
    yz = y * (zz * _sigmoid(zz))
    gwid = SSM_INNER // SSM_GROUPS
    normed = []
    for g in range(SSM_GROUPS):
        part = yz[:, g * gwid:(g + 1) * gwid]
        normed.append(part * lax.rsqrt(jnp.mean(part * part, axis=-1, keepdims=True) + EPS))
    y_ref[bb] = (jnp.concatenate(normed, axis=1) * gn_ref[...]).astype(y_ref.dtype)


def _ssd(xbc, dtr, z, conv0, h0, cw, cb, dtb, alog, dsk, gn, cl, valid, out_dtype):
    b, t, _ = xbc.shape
    nc = t // cl
    nbat = math.gcd(b, SSD_SEQS_PER_STEP)
    rows = lambda w: pl.BlockSpec((nbat, cl, w), lambda bi, c: (bi, c, 0))
    per_b = lambda shape: pl.BlockSpec((nbat,) + shape, lambda bi, c: (bi, 0, 0))
    hshape = (SSM_HEADS * SSM_HEAD_DIM, SSM_STATE)
    return pl.pallas_call(
        functools.partial(_ssd_kernel, cl=cl, valid=valid, nbat=nbat),
        grid=(b // nbat, nc),
        in_specs=[rows(CONV_DIM), rows(LANES), rows(SSM_INNER), per_b((CONV_PAD, CONV_DIM)),
                  per_b(hshape), _const(cw.shape), _const(cb.shape), _const(dtb.shape),
                  _const(alog.shape), _const(dsk.shape), _const(gn.shape)],
        out_specs=[rows(SSM_INNER), per_b((CONV_W - 1, CONV_DIM)), per_b(hshape)],
        out_shape=[jax.ShapeDtypeStruct((b, t, SSM_INNER), out_dtype),
                   jax.ShapeDtypeStruct((b, CONV_W - 1, CONV_DIM), F32),
                   jax.ShapeDtypeStruct((b,) + hshape, F32)],
        scratch_shapes=[pltpu.VMEM((nbat, CONV_PAD + cl, CONV_DIM), F32), pltpu.VMEM((nbat,) + hshape, F32)],
        compiler_params=_params("parallel", "arbitrary"),
        name="ssd",
    )(xbc, dtr, z, conv0, h0, cw, cb, dtb, alog, dsk, gn)


def _head_slice(h):
    return slice(h * X_HEAD_DIM, (h + 1) * X_HEAD_DIM)


def _outproj_body(x, att, y, wa_ref, wy_ref, g_ref, wq_ref, gq_ref):
    x1 = x + _dot(att.astype(BF16), wa_ref[...]) + _dot(y.astype(BF16), wy_ref[...])
    qx = _dot(_rms(x1, g_ref[...]).astype(BF16), wq_ref[...])
    gq = gq_ref[...]
    return x1, [_rms(qx[:, _head_slice(h)], gq) for h in range(X_HEADS)]


def _outproj_kernel(x_ref, att_ref, y_ref, wa_ref, wy_ref, g_ref, wq_ref, gq_ref, x1_ref, qx_ref):
    x1, q_heads = _outproj_body(x_ref[...], att_ref[...], y_ref[...], wa_ref, wy_ref, g_ref, wq_ref, gq_ref)
    x1_ref[...] = x1
    for h in range(X_HEADS):
        qx_ref[:, _head_slice(h)] = q_heads[h].astype(qx_ref.dtype)


def _outproj(x, att, y, w_att, w_y, g, wq, gq, tm, act_dtype):
    n = x.shape[0]
    row = lambda w: pl.BlockSpec((tm, w), lambda i: (i, 0))
    return pl.pallas_call(
        _outproj_kernel,
        grid=(n // tm,),
        in_specs=[row(D_MODEL), row(ATT_WIDTH), row(SSM_INNER), _const(w_att.shape), _const(w_y.shape),
                  _const(g.shape), _const(wq.shape), _const(gq.shape)],
        out_specs=[row(D_MODEL), row(X_WIDTH)],
        out_shape=[jax.ShapeDtypeStruct((n, D_MODEL), F32), jax.ShapeDtypeStruct((n, X_WIDTH), act_dtype)],
        compiler_params=_params("parallel"),
        name="outproj",
    )(x, att, y, w_att, w_y, g, wq, gq)


def _xattn_head(q, k, v):
    s = _dot_nt((q.astype(F32) * (X_HEAD_DIM ** -0.5)).astype(BF16), k.astype(BF16))
    p = jnp.exp(s - jnp.max(s, axis=-1, keepdims=True))
    return _dot(p.astype(BF16), v.astype(BF16)) / jnp.sum(p, axis=-1, keepdims=True)


def _xattn_kernel(q_ref, k_ref, v_ref, o_ref):
    for g in range(q_ref.shape[0]):
        for h in range(X_HEADS):
            sl = _head_slice(h)
            o_ref[g, :, sl] = _xattn_head(q_ref[g, :, sl], k_ref[g, :, sl], v_ref[g, :, sl]).astype(o_ref.dtype)


def _xattn(q, mk, mv, tq, act_dtype):
    b, t, _ = q.shape
    m = mk.shape[1]
    gb = XATTN_SEQS_PER_STEP if (t == tq and b % XATTN_SEQS_PER_STEP == 0 and tq < LANES) else 1
    qspec = pl.BlockSpec((gb, tq, X_WIDTH), lambda bi, i: (bi, i, 0))
    kvspec = pl.BlockSpec((gb, m, X_WIDTH), lambda bi, i: (bi, 0, 0))
    return pl.pallas_call(
        _xattn_kernel,
        grid=(b // gb, t // tq),
        in_specs=[qspec, kvspec, kvspec],
        out_specs=qspec,
        out_shape=jax.ShapeDtypeStruct((b, t, X_WIDTH), act_dtype),
        compiler_params=_params("parallel", "parallel"),
        name="xattn",
    )(q, mk, mv)


def _mlp_body(x1, o, wo_ref, g_ref, wu_ref, wd_ref):
    x2 = x1 + _dot(o.astype(BF16), wo_ref[...])
    hm = _rms(x2, g_ref[...]).astype(BF16)
    acc = x2
    for cidx in range(D_FF // FF_CHUNK):
        sl = slice(cidx * FF_CHUNK, (cidx + 1) * FF_CHUNK)
        up = jnp.maximum(_dot(hm, wu_ref[:, sl]), 0.0)
        acc = acc + _dot((up * up).astype(BF16), wd_ref[sl, :])
    return acc


def _mlp_kernel(x1_ref, o_ref, wo_ref, g_ref, wu_ref, wd_ref, out_ref):
    out_ref[...] = _mlp_body(x1_ref[...], o_ref[...], wo_ref, g_ref, wu_ref, wd_ref)


def _mlp(x1, o, wo, g, wu, wd, tm):
    n = x1.shape[0]
    row = lambda w: pl.BlockSpec((tm, w), lambda i: (i, 0))
    return pl.pallas_call(
        _mlp_kernel,
        grid=(n // tm,),
        in_specs=[row(D_MODEL), row(X_WIDTH), _const_single(wo.shape), _const(g.shape),
                  _const_single(wu.shape), _const_single(wd.shape)],
        out_specs=row(D_MODEL),
        out_shape=jax.ShapeDtypeStruct((n, D_MODEL), F32),
        compiler_params=_params("parallel"),
        name="mlp",
    )(x1, o, wo, g, wu, wd)


def _post_kernel(x_ref, att_ref, y_ref, mk_ref, mv_ref, wa_ref, wy_ref, gx_ref, wq_ref, gq_ref,
                 wo_ref, gm_ref, wu_ref, wd_ref, out_ref):
    x1, q_heads = _outproj_body(x_ref[...], att_ref[...], y_ref[...], wa_ref, wy_ref, gx_ref, wq_ref, gq_ref)
    o = jnp.concatenate([_xattn_head(q_heads[h], mk_ref[0, :, _head_slice(h)], mv_ref[0, :, _head_slice(h)])
                         for h in range(X_HEADS)], axis=1)
    out_ref[...] = _mlp_body(x1, o, wo_ref, gm_ref, wu_ref, wd_ref)


def _post(x, att, y, mk, mv, wts, tm, t):
    n = x.shape[0]
    tiles_per_seq = t // tm
    m = mk.shape[1]
    row = lambda w: pl.BlockSpec((tm, w), lambda i: (i, 0))
    mem = pl.BlockSpec((1, m, X_WIDTH), lambda i: (i // tiles_per_seq, 0, 0))
    names = ("w_out_att", "w_out_ssm", "ln_x_g", "wq_x", "qx_norm_g", "wo_x", "ln_mlp_g", "w_up", "w_down")
    big = ("w_up", "w_down")
    weights = [wts[k] for k in names]
    return pl.pallas_call(
        _post_kernel,
        grid=(n // tm,),
        in_specs=[row(D_MODEL), row(ATT_WIDTH), row(SSM_INNER), mem, mem]
                 + [(_const_single if k in big else _const)(wts[k].shape) for k in names],
        out_specs=row(D_MODEL),
        out_shape=jax.ShapeDtypeStruct((n, D_MODEL), F32),
        compiler_params=_params("parallel"),
        name="post",
    )(x, att, y, mk, mv, *weights)


def _cast_kernel(w_ref, o_ref):
    o_ref[...] = w_ref[...].astype(o_ref.dtype)


def _to_bf16(w):
    rows, cols = w.shape
    tr = _row_tile(rows, CAST_BLOCK_BYTES // (cols * 4))
    spec = pl.BlockSpec((tr, cols), lambda i: (i, 0))
    return pl.pallas_call(
        _cast_kernel,
        grid=(rows // tr,),
        in_specs=[spec],
        out_specs=spec,
        out_shape=jax.ShapeDtypeStruct(w.shape, BF16),
        compiler_params=_params("parallel"),
        name="cast_bf16",
    )(w)


def _memkv_kernel(m_ref, g_ref, wk_ref, wv_ref, gk_ref, mk_ref, mv_ref):
    mn = _rms(m_ref[...], g_ref[...]).astype(BF16)
    mk = _dot(mn, wk_ref[...])
    gk = gk_ref[...]
    for h in range(X_HEADS):
        sl = slice(h * X_HEAD_DIM, (h + 1) * X_HEAD_DIM)
        mk_ref[:, sl] = _rms(mk[:, sl], gk)
    mv_ref[...] = _dot(mn, wv_ref[...])


def _memkv(mem, g, wk, wv, gk, tm):
    n = mem.shape[0]
    row = lambda w: pl.BlockSpec((tm, w), lambda i: (i, 0))
    return pl.pallas_call(
        _memkv_kernel,
        grid=(n // tm,),
        in_specs=[row(D_MODEL), _const(g.shape), _const(wk.shape), _const(wv.shape), _const(gk.shape)],
        out_specs=[row(X_WIDTH), row(X_WIDTH)],
        out_shape=[jax.ShapeDtypeStruct((n, X_WIDTH), F32)] * 2,
        compiler_params=_params("parallel"),
        name="memkv",
    )(mem, g, wk, wv, gk)


def _rope_tables(pos):
    half = HEAD_DIM // 2
    inv_freq = ROPE_THETA ** (-jnp.arange(half, dtype=F32) / half)
    ang = pos.astype(F32)[:, None] * inv_freq[None, :]
    return jnp.cos(ang).T, jnp.sin(ang).T


def _row_tile(n, want):
    tm = min(n, want)
    assert n % tm == 0
    return tm


def _layer(x, nb, cos_t, sin_t, conv0, h0, mem_k, mem_v, wts, attend, tm, act_dtype):
    b, t, _ = x.shape
    n = b * t
    xf = x.reshape(n, D_MODEL)
    qt, kt, vt, z, xbc, dtr = _inproj(xf, nb, wts["ln_mix_g"], wts["w_qkv_t"], wts["w_rest"], wts["w_dt"],
                                      wts["q_norm_g"], wts["k_norm_g"], cos_t, sin_t, tm)
    att, extras = attend(qt, kt, vt)

    cl = SSD_CHUNK if t >= SSD_CHUNK else -(-t // SSD_MIN_CHUNK) * SSD_MIN_CHUNK
    tp = -(-t // cl) * cl
    pad3 = lambda a: jnp.pad(a.reshape(b, t, -1), ((0, 0), (0, tp - t), (0, 0)))
    y, new_conv, new_h = _ssd(pad3(xbc), pad3(dtr), pad3(z), conv0, h0, wts["conv_w"], wts["conv_b"],
                              wts["dt_bias"], wts["a_log"], wts["d_skip"], wts["ssm_norm_g"],
                              cl, min(t, cl) if tp != t else cl, act_dtype)
    y = y[:, :t].reshape(n, SSM_INNER)

    if t % tm == 0:
        out = _post(xf, att, y, mem_k, mem_v, wts, tm, t)
    else:
        x1, qx = _outproj(xf, att, y, wts["w_out_att"], wts["w_out_ssm"], wts["ln_x_g"],
                          wts["wq_x"], wts["qx_norm_g"], tm, act_dtype)
        o = _xattn(qx.reshape(b, t, X_WIDTH), mem_k, mem_v, _row_tile(t, 512), act_dtype)
        out = _mlp(x1, o.reshape(n, X_WIDTH), wts["wo_x"], wts["ln_mlp_g"], wts["w_up"], wts["w_down"], tm)
    return out.reshape(b, t, D_MODEL), extras, new_conv, new_h


def kernel(x_prompt, x_sample, mem_prompt, cache_k, cache_v, page_table, state_conv, state_ssm,
           cache_mem_k, cache_mem_v, ln_mix_g, w_in, q_norm_g, k_norm_g, conv_w, conv_b, dt_bias,
           a_log, d_skip, ssm_norm_g, w_out, ln_x_g, ln_mem_g, wq_x, wk_x, wv_x, qx_norm_g,
           kx_norm_g, wo_x, ln_mlp_g, w_up, w_down):
    depth = w_in.shape[0]
    assert depth == 1, "single-layer step"
    bp, tp, _ = x_prompt.shape
    bs, ts, _ = x_sample.shape
    page = cache_k.shape[2]
    past_len = page_table.shape[1] * page
    assert tp % MOBA_BLOCK == 0 and past_len % MOBA_BLOCK == 0 and MOBA_BLOCK % page == 0
    assert ts <= MOBA_BLOCK and ts <= SSD_CHUNK and ts >= CONV_W - 1
    l = 0
    qkv_cols = 3 * ATT_WIDTH
    main_cols = qkv_cols + SSM_INNER + CONV_DIM

    def lanes(vec, width=LANES):
        return jnp.pad(vec.astype(F32), (0, width - vec.shape[0])).reshape(1, width)

    def dim_gain(vec):
        return jnp.broadcast_to(vec.astype(F32)[:, None], (HEAD_DIM, LANES))

    wts = {
        "ln_mix_g": ln_mix_g[l].reshape(1, D_MODEL),
        "w_qkv_t": w_in[l][:, :qkv_cols].T.astype(BF16),
        "w_rest": w_in[l][:, qkv_cols:main_cols].astype(BF16),
        "w_dt": jnp.pad(w_in[l][:, main_cols:], ((0, 0), (0, LANES - SSM_HEADS))).astype(BF16),
        "q_norm_g": dim_gain(q_norm_g[l]),
        "k_norm_g": dim_gain(k_norm_g[l]),
        "conv_w": conv_w[l],
        "conv_b": conv_b[l].reshape(1, CONV_DIM),
        "dt_bias": lanes(dt_bias[l]),
        "a_log": lanes(a_log[l]),
        "d_skip": jnp.repeat(d_skip[l].astype(F32), SSM_HEAD_DIM).reshape(1, SSM_INNER),
        "ssm_norm_g": ssm_norm_g[l].reshape(1, SSM_INNER),
        "w_out_att": w_out[l][:ATT_WIDTH].astype(BF16),
        "w_out_ssm": w_out[l][ATT_WIDTH:].astype(BF16),
        "ln_x_g": ln_x_g[l].reshape(1, D_MODEL),
        "wq_x": wq_x[l].astype(BF16),
        "qx_norm_g": qx_norm_g[l].reshape(1, X_HEAD_DIM),
        "wo_x": wo_x[l].astype(BF16),
        "ln_mlp_g": ln_mlp_g[l].reshape(1, D_MODEL),
        "w_up": _to_bf16(w_up[l]),
        "w_down": _to_bf16(w_down[l]),
    }

    mem_rows = mem_prompt.reshape(-1, D_MODEL)
    mk_p, mv_p = _memkv(mem_rows, ln_mem_g[l].reshape(1, D_MODEL), wk_x[l].astype(BF16),
                        wv_x[l].astype(BF16), kx_norm_g[l].reshape(1, X_HEAD_DIM),
                        _row_tile(mem_rows.shape[0], 512))
    mem_len = mem_prompt.shape[1]
    mk_p = mk_p.reshape(bp, mem_len, X_WIDTH)
    mv_p = mv_p.reshape(bp, mem_len, X_WIDTH)
    cos_p, sin_p = _rope_tables(jnp.arange(tp, dtype=jnp.int32))
    conv0_p = jnp.zeros((bp, CONV_PAD, CONV_DIM), F32)
    h0_p = jnp.zeros((bp, SSM_HEADS * SSM_HEAD_DIM, SSM_STATE), F32)

    def attend_prompt(qt, kt, vt):
        heads_t = lambda a: a.reshape(1, bp, ATT_HEADS, HEAD_DIM, tp).transpose(0, 1, 4, 2, 3)
        return _moba_prompt(qt, kt, vt).reshape(bp * tp, ATT_WIDTH), (heads_t(kt), heads_t(vt))

    y_p, (k_p, v_p), conv_p, ssm_p = _layer(x_prompt, bp, cos_p, sin_p, conv0_p, h0_p, mk_p, mv_p, wts,
                                            attend_prompt, _row_tile(tp, 512), BF16)

    cos_s, sin_s = _rope_tables(past_len + jnp.arange(ts, dtype=jnp.int32))
    cos_s = jnp.tile(cos_s, (1, bs))
    sin_s = jnp.tile(sin_s, (1, bs))
    conv0_s = jnp.pad(state_conv[l], ((0, 0), (CONV_PAD - (CONV_W - 1), 0), (0, 0)))
    h0_s = state_ssm[l].reshape(bs, SSM_HEADS * SSM_HEAD_DIM, SSM_STATE)
    cache_kt = cache_k.transpose(0, 1, 3, 4, 2)
    cache_vt = cache_v.transpose(0, 1, 3, 4, 2)

    def attend_sample(qt, kt, vt):
        rows = lambda a: a[0].T.reshape(bs, ts, ATT_WIDTH)
        k_new, v_new = rows(kt), rows(vt)
        o = _sample_moba(page_table, rows(qt), k_new, v_new, cache_kt, cache_vt)
        heads = lambda a: a.reshape(1, bs, ts, ATT_HEADS, HEAD_DIM)
        return o.reshape(bs * ts, ATT_WIDTH), (heads(k_new), heads(v_new))

    y_s, (k_s, v_s), conv_s, ssm_s = _layer(
        x_sample, 1, cos_s, sin_s, conv0_s, h0_s, cache_mem_k[l].reshape(bs, -1, X_WIDTH),
        cache_mem_v[l].reshape(bs, -1, X_WIDTH), wts, attend_sample, bs * ts, F32)

    state = lambda a, b: a.reshape(1, b, SSM_HEADS, SSM_HEAD_DIM, SSM_STATE)
    return (y_p, y_s, k_p, v_p, conv_p[None], state(ssm_p, bp),
            mk_p.reshape(1, bp, mem_len, X_HEADS, X_HEAD_DIM), mv_p.reshape(1, bp, mem_len, X_HEADS, X_HEAD_DIM),
            k_s, v_s, conv_s[None], state(ssm_s, bs))
```

```python
import functools
import math

import jax
import jax.numpy as jnp
from jax import lax
from jax.experimental import pallas as pl
from jax.experimental.pallas import tpu as pltpu

F32 = jnp.float32
BF16 = jnp.bfloat16

D_MODEL = 1024
ATT_HEADS = 8
HEAD_DIM = 64
ATT_WIDTH = ATT_HEADS * HEAD_DIM
MOBA_BLOCK = 256
TOPK = 3
ROPE_THETA = 10000.0
SSM_HEADS = 8
SSM_HEAD_DIM = 64
SSM_INNER = SSM_HEADS * SSM_HEAD_DIM
SSM_GROUPS = 2
SSM_STATE = 128
CONV_W = 4
CONV_DIM = SSM_INNER + 2 * SSM_GROUPS * SSM_STATE
SSD_CHUNK = 128
X_HEADS = 4
X_HEAD_DIM = 128
X_WIDTH = X_HEADS * X_HEAD_DIM
D_FF = 4 * D_MODEL
EPS = 1e-6

LANES = 128
CONV_PAD = 8
VMEM_LIMIT = 56 * 1024 * 1024
NEG = -1e30
LOG2E = 1.4426950408889634
MOBA_GROUP = 4
ONES_ROWS = 16
FF_CHUNK = 1024
XATTN_SEQS_PER_STEP = 8
SSD_MIN_CHUNK = 16
SSD_SEQS_PER_STEP = 8
PAGES_PER_STEP = 32


def _dot(a, b):
    return jnp.dot(a, b, preferred_element_type=F32)


def _dot_nt(a, b):
    return lax.dot_general(a, b, (((1,), (1,)), ((), ())), preferred_element_type=F32)


def _dot_tn(a, b):
    return lax.dot_general(a, b, (((0,), (0,)), ((), ())), preferred_element_type=F32)


def _split3(x):
    hi = x.astype(BF16)
    r = x - hi.astype(F32)
    mid = r.astype(BF16)
    lo = (r - mid.astype(F32)).astype(BF16)
    return hi, mid, lo


def _dot_f32(a, b, dot=None):
    dot = dot or _dot
    a0, a1, a2 = _split3(a)
    b0, b1, b2 = _split3(b)
    return (dot(a0, b0) + (dot(a0, b1) + dot(a1, b0)) + (dot(a1, b1) + dot(a0, b2) + dot(a2, b0)))


def _rms(x, g):
    ms = jnp.mean(x * x, axis=-1, keepdims=True)
    return x * lax.rsqrt(ms + EPS) * g


def _sigmoid(x):
    return 1.0 / (1.0 + jnp.exp(-x))


def _params(*sem):
    return pltpu.CompilerParams(dimension_semantics=sem, vmem_limit_bytes=VMEM_LIMIT)


def _const(shape):
    return pl.BlockSpec(shape, lambda *_: (0,) * len(shape))


def _const_single(shape):
    return pl.BlockSpec(shape, lambda *_: (0,) * len(shape), pipeline_mode=pl.Buffered(1))


def _inproj_kernel(x_ref, g_ref, wqkv_ref, wrest_ref, wdt_ref, gq_ref, gk_ref, cos_ref, sin_ref,
                   q_ref, k_ref, v_ref, z_ref, xbc_ref, dt_ref):
    hn = _rms(x_ref[...], g_ref[...]).astype(BF16)
    tm = hn.shape[0]
    cos = cos_ref[...]
    sin = sin_ref[...]
    half = HEAD_DIM // 2

    def proj_t(idx):
        return _dot_nt(wqkv_ref[idx * ATT_WIDTH:(idx + 1) * ATT_WIDTH, :], hn)

    def head_norm_rot(p, g_ref_):
        p3 = p.reshape(ATT_HEADS, HEAD_DIM, tm)
        gain = g_ref_[:, 0:1]
        pn = p3 * lax.rsqrt(jnp.mean(p3 * p3, axis=1, keepdims=True) + EPS) * gain
        x1 = pn[:, :half]
        x2 = pn[:, half:]
        rot = jnp.concatenate([x1 * cos - x2 * sin, x2 * cos + x1 * sin], axis=1)
        return rot.reshape(ATT_WIDTH, tm)

    q_ref[0] = head_norm_rot(proj_t(0), gq_ref)
    k_ref[0] = head_norm_rot(proj_t(1), gk_ref)
    v_ref[0] = proj_t(2)
    z_ref[...] = _dot_nt(hn, wrest_ref[0:SSM_INNER, :])
    xbc_ref[...] = _dot_nt(hn, wrest_ref[SSM_INNER:SSM_INNER + CONV_DIM, :])
    dt_ref[...] = _dot_nt(hn, wdt_ref[...])


def _inproj(x, nb, g, w_qkv_t, w_rest, w_dt, gq, gk, cos_t, sin_t, tm):
    n = x.shape[0]
    cols = n // nb
    tpb = cols // tm
    row = lambda w: pl.BlockSpec((tm, w), lambda i: (i, 0))
    tab = pl.BlockSpec((HEAD_DIM // 2, tm), lambda i: (0, i % tpb))
    tspec = pl.BlockSpec((1, ATT_WIDTH, tm), lambda i: (i // tpb, 0, i % tpb))
    outs = ([jax.ShapeDtypeStruct((nb, ATT_WIDTH, cols), F32)] * 3
            + [jax.ShapeDtypeStruct((n, w), F32) for w in (SSM_INNER, CONV_DIM, LANES)])
    return pl.pallas_call(
        _inproj_kernel,
        grid=(n // tm,),
        in_specs=[row(D_MODEL), _const(g.shape), _const_single(w_qkv_t.shape), _const_single(w_rest.shape),
                  _const(w_dt.shape), _const(gq.shape), _const(gk.shape), tab, tab],
        out_specs=[tspec, tspec, tspec, row(SSM_INNER), row(CONV_DIM), row(LANES)],
        out_shape=outs,
        compiler_params=_params("parallel"),
        name="inproj",
    )(x, g, w_qkv_t, w_rest, w_dt, gq, gk, cos_t, sin_t)


def _topk_keep(gate, n_valid, axis):
    idx = lax.broadcasted_iota(jnp.int32, gate.shape, axis)
    cnt = jnp.zeros(gate.shape, jnp.int32)
    for jp in range(gate.shape[axis]):
        gj = lax.slice_in_dim(gate, jp, jp + 1, axis=axis)
        beats = (gj > gate) | ((gj == gate) & (jp < idx))
        cnt = cnt + jnp.where(beats, jnp.where(jp < n_valid, 1, 0), 0)
    return (idx < n_valid) & (cnt < TOPK)


def _moba_kernel(qt_ref, kt_ref, vt_ref, o_ref, kb_ref, vb_ref, km_ref, *, nblk):
    bq = MOBA_BLOCK
    km_ref[...] = jnp.zeros(km_ref.shape, F32)
    for j in range(nblk):
        kj = kt_ref[0, :, j * bq:(j + 1) * bq].T
        kb_ref[j] = kj.astype(BF16)
        km_ref[j:j + 1, :] = jnp.mean(kj, axis=0, keepdims=True)
        vj = vt_ref[0, :, j * bq:(j + 1) * bq].astype(BF16)
        for h in range(2):
            vb_ref[j, h, 0:HEAD_DIM, :] = vj[h * HEAD_DIM:(h + 1) * HEAD_DIM, :]
            vb_ref[j, h, HEAD_DIM:, :] = jnp.ones((ONES_ROWS, bq), BF16)
    kmean = km_ref[...]

    keyi = lax.broadcasted_iota(jnp.int32, (bq, bq), 0)
    qryi = lax.broadcasted_iota(jnp.int32, (bq, bq), 1)
    causal = jnp.where(keyi <= qryi, 0.0, NEG)
    row_lo = lax.broadcasted_iota(jnp.int32, (LANES, bq), 0) < HEAD_DIM
    qscale = (HEAD_DIM ** -0.5) * LOG2E

    order = [blk for pair in zip(range(nblk), reversed(range(nblk))) for blk in pair][:nblk]
    for g0 in range(0, nblk, MOBA_GROUP):
        owns = order[g0:g0 + MOBA_GROUP]
        work = []
        for own in owns:
            qt = qt_ref[0, :, own * bq:(own + 1) * bq]
            for h in range(2):
                qh = jnp.where(row_lo, 0.0, qt) if h else jnp.where(row_lo, qt, 0.0)
                qsb = (qh * qscale).astype(BF16)
                bias = None
                if own > TOPK:
                    keep = _topk_keep(_dot_f32(kmean, qh), own, 0)
                    bias = jnp.where(keep, 0.0, NEG)
                tiles = []
                m = None
                for j in range(own + 1):
                    s = _dot(kb_ref[j], qsb)
                    if j == own:
                        s = s + causal
                    elif bias is not None:
                        s = s + bias[j:j + 1, :]
                    tiles.append(s)
                    mj = jnp.max(s, axis=0, keepdims=True)
                    m = mj if m is None else jnp.maximum(m, mj)
                work.append((own, h, tiles, m))
        outs = {}
        for own, h, tiles, m in work:
            acc = jnp.zeros((HEAD_DIM + ONES_ROWS, bq), F32)
            for j in range(own + 1):
                p = jnp.exp2(tiles[j] - m)
                acc = acc + _dot(vb_ref[j, h], p.astype(BF16))
            outs[(own, h)] = acc[0:HEAD_DIM] / acc[HEAD_DIM:HEAD_DIM + 1]
        for own in owns:
            o_t = jnp.concatenate([outs[(own, 0)], outs[(own, 1)]], axis=0)
            o_ref[0, own * bq:(own + 1) * bq, :] = o_t.T.astype(o_ref.dtype)


def _moba_prompt(qt, kt, vt):
    b, _, t = qt.shape
    nblk = t // MOBA_BLOCK
    npad = max(16, nblk)
    npair = ATT_WIDTH // LANES
    spec = pl.BlockSpec((1, LANES, t), lambda bi, p: (bi, p, 0))
    return pl.pallas_call(
        functools.partial(_moba_kernel, nblk=nblk),
        grid=(b, npair),
        in_specs=[spec, spec, spec],
        out_specs=pl.BlockSpec((1, t, LANES), lambda bi, p: (bi, 0, p)),
        out_shape=jax.ShapeDtypeStruct((b, t, ATT_WIDTH), BF16),
        scratch_shapes=[pltpu.VMEM((nblk, MOBA_BLOCK, LANES), BF16),
                        pltpu.VMEM((nblk, 2, HEAD_DIM + ONES_ROWS, MOBA_BLOCK), BF16),
                        pltpu.VMEM((npad, LANES), F32)],
        compiler_params=_params("parallel", "parallel"),
        name="moba_prompt",
    )(qt, kt, vt)


def _expand_heads(q):
    t = q.shape[0]
    tiled = jnp.concatenate([q] * ATT_HEADS, axis=0)
    rowh = lax.broadcasted_iota(jnp.int32, tiled.shape, 0) // t
    laneh = lax.broadcasted_iota(jnp.int32, tiled.shape, 1) // HEAD_DIM
    return jnp.where(rowh == laneh, tiled, 0.0)


def _sample_moba_kernel(pt_ref, q_ref, kn_ref, vn_ref, *rest, page, npg, nblk):
    del pt_ref
    k_refs = rest[:npg]
    v_refs = rest[npg:2 * npg]
    o_ref, km_ref, m_ref, l_ref, oblk_ref = rest[2 * npg:]
    c = pl.program_id(1)
    t = q_ref.shape[1]
    rows = ATT_HEADS * t
    pages_per_block = MOBA_BLOCK // page
    blocks_per_step = npg // pages_per_block

    qe = _expand_heads(q_ref[0])
    qb = (qe * ((HEAD_DIM ** -0.5) * LOG2E)).astype(BF16)
    blk_lane = lax.broadcasted_iota(jnp.int32, (ATT_WIDTH, LANES), 1)

    @pl.when(c == 0)
    def _():
        km_ref[...] = jnp.zeros(km_ref.shape, F32)

    tiles = []
    for jj in range(blocks_per_step):
        ksum = None
        for u in range(pages_per_block):
            kt = k_refs[jj * pages_per_block + u][...].reshape(ATT_WIDTH, page)
            ksum = kt if ksum is None else ksum + kt
            tiles.append(_dot(qb, kt.astype(BF16)))
        kmean = jnp.sum(ksum, axis=1, keepdims=True) * (1.0 / MOBA_BLOCK)
        km_ref[...] = jnp.where(blk_lane == c * blocks_per_step + jj, kmean, km_ref[...])
    probs = []
    for jj in range(blocks_per_step):
        mine = tiles[jj * pages_per_block:(jj + 1) * pages_per_block]
        m = jnp.max(mine[0], axis=1, keepdims=True)
        for s in mine[1:]:
            m = jnp.maximum(m, jnp.max(s, axis=1, keepdims=True))
        l = jnp.zeros((rows, 1), F32)
        for s in mine:
            p = jnp.exp2(s - m)
            l = l + jnp.sum(p, axis=1, keepdims=True)
            probs.append(p.astype(BF16))
        m_ref[c * blocks_per_step + jj] = jnp.broadcast_to(m, (rows, LANES))
        l_ref[c * blocks_per_step + jj] = jnp.broadcast_to(l, (rows, LANES))
    for jj in range(blocks_per_step):
        o = jnp.zeros((rows, ATT_WIDTH), F32)
        for u in range(pages_per_block):
            vt = v_refs[jj * pages_per_block + u][...].reshape(ATT_WIDTH, page).astype(BF16)
            o = o + _dot_nt(probs[jj * pages_per_block + u], vt)
        oblk_ref[c * blocks_per_step + jj] = o

    @pl.when(c == pl.num_programs(1) - 1)
    def _():
        keep = _topk_keep(_dot_f32(qe, km_ref[:, 0:nblk]), nblk, 1)
        s_own = _dot_nt(qb, kn_ref[0].astype(BF16))
        qpos = lax.broadcasted_iota(jnp.int32, s_own.shape, 0) % t
        kpos = lax.broadcasted_iota(jnp.int32, s_own.shape, 1)
        s_own = jnp.where(kpos <= qpos, s_own, NEG)
        lane_j = lax.broadcasted_iota(jnp.int32, (rows, nblk), 1)
        m_all = jnp.zeros((rows, nblk), F32)
        l_all = jnp.zeros((rows, nblk), F32)
        for j in range(nblk):
            m_all = jnp.where(lane_j == j, m_ref[j][:, 0:nblk], m_all)
            l_all = jnp.where(lane_j == j, l_ref[j][:, 0:nblk], l_all)
        mx = jnp.maximum(jnp.max(s_own, axis=1, keepdims=True),
                         jnp.max(jnp.where(keep, m_all, NEG), axis=1, keepdims=True))
        p_own = jnp.exp2(s_own - mx)
        w = jnp.where(keep, jnp.exp2(jnp.minimum(m_all - mx, 0.0)), 0.0)
        den = jnp.sum(p_own, axis=1, keepdims=True) + jnp.sum(w * l_all, axis=1, keepdims=True)
        vn = vn_ref[0]
        num = jnp.zeros((rows, ATT_WIDTH), F32)
        for tt in range(t):
            num = num + p_own[:, tt:tt + 1] * vn[tt:tt + 1, :]
        for j in range(nblk):
            num = num + w[:, j:j + 1] * oblk_ref[j]
        full = num / den
        laneh = lax.broadcasted_iota(jnp.int32, (t, ATT_WIDTH), 1) // HEAD_DIM
        out = jnp.zeros((t, ATT_WIDTH), F32)
        for h in range(ATT_HEADS):
            out = out + jnp.where(laneh == h, full[h * t:(h + 1) * t, :], 0.0)
        o_ref[0] = out


def _sample_moba(page_table, q, k_new, v_new, cache_kt, cache_vt):
    b, t, _ = q.shape
    page = cache_kt.shape[4]
    n_pages = page_table.shape[1]
    npg = math.gcd(n_pages, PAGES_PER_STEP)
    nsteps = n_pages // npg
    nblk = n_pages * page // MOBA_BLOCK
    assert nblk <= LANES and npg % (MOBA_BLOCK // page) == 0
    rows = ATT_HEADS * t
    per_b = pl.BlockSpec((1, t, ATT_WIDTH), lambda bi, c, pt: (bi, 0, 0))

    def pspec(pi):
        return pl.BlockSpec((None, None, ATT_HEADS, HEAD_DIM, page),
                            lambda bi, c, pt: (0, pt[bi, c * npg + pi], 0, 0, 0))

    grid_spec = pltpu.PrefetchScalarGridSpec(
        num_scalar_prefetch=1,
        grid=(b, nsteps),
        in_specs=[per_b, per_b, per_b] + [pspec(pi) for pi in range(npg)] * 2,
        out_specs=per_b,
        scratch_shapes=[pltpu.VMEM((ATT_WIDTH, LANES), F32),
                        pltpu.VMEM((nblk, rows, LANES), F32),
                        pltpu.VMEM((nblk, rows, LANES), F32),
                        pltpu.VMEM((nblk, rows, ATT_WIDTH), F32)],
    )
    return pl.pallas_call(
        functools.partial(_sample_moba_kernel, page=page, npg=npg, nblk=nblk),
        grid_spec=grid_spec,
        out_shape=jax.ShapeDtypeStruct((b, t, ATT_WIDTH), F32),
        compiler_params=_params("parallel", "arbitrary"),
        name="sample_moba",
    )(page_table, q, k_new, v_new, *([cache_kt] * npg), *([cache_vt] * npg))


def _cumsum_rows(x):
    n = x.shape[0]
    row = lax.broadcasted_iota(jnp.int32, x.shape, 0)
    step = 1
    while step < n:
        x = x + jnp.where(row >= step, pltpu.roll(x, step, 0), 0.0)
        step *= 2
    return x


def _ssd_kernel(xbc_ref, dtr_ref, z_ref, conv0_ref, h0_ref, cw_ref, cb_ref, dtb_ref, alog_ref,
                dsk_ref, gn_ref, y_ref, convo_ref, ho_ref, xpad_ref, h_ref, *, cl, valid, nbat):
    c = pl.program_id(1)
    nc = pl.num_programs(1)

    @pl.when(c == 0)
    def _():
        for bb in range(nbat):
            xpad_ref[bb, 0:CONV_PAD, :] = conv0_ref[bb]
            h_ref[bb] = h0_ref[bb]

    for bb in range(nbat):
        _ssd_chunk(bb, xbc_ref, dtr_ref, z_ref, cw_ref, cb_ref, dtb_ref, alog_ref, dsk_ref, gn_ref,
                   y_ref, xpad_ref, h_ref, cl, valid)

    @pl.when(c == nc - 1)
    def _():
        for bb in range(nbat):
            convo_ref[bb] = xpad_ref[bb, CONV_PAD + valid - (CONV_W - 1):CONV_PAD + valid, :]
            ho_ref[bb] = h_ref[bb]


def _ssd_chunk(bb, xbc_ref, dtr_ref, z_ref, cw_ref, cb_ref, dtb_ref, alog_ref, dsk_ref, gn_ref,
               y_ref, xpad_ref, h_ref, cl, valid):
    x = xbc_ref[bb]
    xpad_ref[bb, CONV_PAD:CONV_PAD + cl, :] = x
    w = cw_ref[...]
    conv = cb_ref[...] + x * w[CONV_W - 1:CONV_W, :]
    for back in range(1, CONV_W):
        conv = conv + xpad_ref[bb, pl.ds(CONV_PAD - back, cl), :] * w[CONV_W - 1 - back:CONV_W - back, :]

    xpad_ref[bb, 0:CONV_PAD, :] = xpad_ref[bb, cl:cl + CONV_PAD, :]
    xc = conv * _sigmoid(conv)
    xs = xc[:, :SSM_INNER]
    gw = SSM_STATE
    bm = [xc[:, SSM_INNER + g * gw:SSM_INNER + (g + 1) * gw].astype(BF16) for g in range(SSM_GROUPS)]
    cm = [xc[:, SSM_INNER + (SSM_GROUPS + g) * gw:SSM_INNER + (SSM_GROUPS + g + 1) * gw].astype(BF16)
          for g in range(SSM_GROUPS)]

    pre = dtr_ref[bb] + dtb_ref[...]
    dt = jnp.maximum(pre, 0.0) + jnp.log1p(jnp.exp(-jnp.abs(pre)))
    rowl = lax.broadcasted_iota(jnp.int32, dt.shape, 0)
    if valid < cl:
        dt = jnp.where(rowl < valid, dt, 0.0)
    da = dt * (-jnp.exp(alog_ref[...]))
    cum = _cumsum_rows(da)
    if cl < LANES:
        cum_t = jnp.concatenate([cum, jnp.zeros((LANES - cl, LANES), F32)], axis=0).T[:, 0:cl]
    else:
        cum_t = cum.T
    cum_last = cum[cl - 1:cl, :]
    ecum = jnp.exp(cum)
    tail = jnp.exp(cum_last - cum)
    elast = jnp.exp(cum_last)

    ri = lax.broadcasted_iota(jnp.int32, (cl, cl), 0)
    ci = lax.broadcasted_iota(jnp.int32, (cl, cl), 1)
    causal = ci <= ri
    lane = lax.broadcasted_iota(jnp.int32, (cl, LANES), 1)
    lo = lane < SSM_HEAD_DIM
    row_lo = lax.broadcasted_iota(jnp.int32, (LANES, 1), 0) < SSM_HEAD_DIM
    heads_per_group = SSM_HEADS // SSM_GROUPS
    gmat = [_dot_nt(cm[g], bm[g]) for g in range(SSM_GROUPS)]

    def pick(cols, h0):
        return jnp.where(lo, cols[:, h0:h0 + 1], cols[:, h0 + 1:h0 + 2])

    ys = []
    for p in range(SSM_HEADS // 2):
        h0 = 2 * p
        g = h0 // heads_per_group
        xs_p = xs[:, p * LANES:(p + 1) * LANES]
        xdt = xs_p * pick(dt, h0)
        xdt_b = xdt.astype(BF16)
        parts = []
        for h in (h0, h0 + 1):
            seg = cum[:, h:h + 1] - cum_t[h:h + 1, :]
            dec = jnp.where(causal, jnp.exp(jnp.minimum(seg, 0.0)), 0.0)
            parts.append(_dot((gmat[g] * dec).astype(BF16), xdt_b))
        hp = h_ref[bb, p * LANES:(p + 1) * LANES, :]
        y_p = jnp.where(lo, parts[0], parts[1])
        y_p = y_p + _dot_nt(cm[g], hp.astype(BF16)) * pick(ecum, h0)
        y_p = y_p + dsk_ref[:, p * LANES:(p + 1) * LANES] * xs_p
        xw = (xdt * pick(tail, h0)).astype(BF16)
        decay_rows = jnp.where(row_lo, elast[:, h0:h0 + 1], elast[:, h0 + 1:h0 + 2])
        h_ref[bb, p * LANES:(p + 1) * LANES, :] = hp * decay_rows + _dot_tn(xw, bm[g])
        ys.append(y_p)

    y = jnp.concatenate(ys, axis=1)
    zz = z_ref[bb]
    yz = y * (zz * _sigmoid(zz))
    gwid = SSM_INNER // SSM_GROUPS
    normed = []
    for g in range(SSM_GROUPS):
        part = yz[:, g * gwid:(g + 1) * gwid]
        normed.append(part * lax.rsqrt(jnp.mean(part * part, axis=-1, keepdims=True) + EPS))
    y_ref[bb] = (jnp.concatenate(normed, axis=1) * gn_ref[...]).astype(y_ref.dtype)


def _ssd(xbc, dtr, z, conv0, h0, cw, cb, dtb, alog, dsk, gn, cl, valid, out_dtype):
    b, t, _ = xbc.shape
    nc = t // cl
    nbat = math.gcd(b, SSD_SEQS_PER_STEP)
    rows = lambda w: pl.BlockSpec((nbat, cl, w), lambda bi, c: (bi, c, 0))
    per_b = lambda shape: pl.BlockSpec((nbat,) + shape, lambda bi, c: (bi, 0, 0))
    hshape = (SSM_HEADS * SSM_HEAD_DIM, SSM_STATE)
    return pl.pallas_call(
        functools.partial(_ssd_kernel, cl=cl, valid=valid, nbat=nbat),
        grid=(b // nbat, nc),
        in_specs=[rows(CONV_DIM), rows(LANES), rows(SSM_INNER), per_b((CONV_PAD, CONV_DIM)),
                  per_b(hshape), _const(cw.shape), _const(cb.shape), _const(dtb.shape),
                  _const(alog.shape), _const(dsk.shape), _const(gn.shape)],
        out_specs=[rows(SSM_INNER), per_b((CONV_W - 1, CONV_DIM)), per_b(hshape)],
        out_shape=[jax.ShapeDtypeStruct((b, t, SSM_INNER), out_dtype),
                   jax.ShapeDtypeStruct((b, CONV_W - 1, CONV_DIM), F32),
                   jax.ShapeDtypeStruct((b,) + hshape, F32)],
        scratch_shapes=[pltpu.VMEM((nbat, CONV_PAD + cl, CONV_DIM), F32), pltpu.VMEM((nbat,) + hshape, F32)],
        compiler_params=_params("parallel", "arbitrary"),
        name="ssd",
    )(xbc, dtr, z, conv0, h0, cw, cb, dtb, alog, dsk, gn)


def _head_slice(h):
    return slice(h * X_HEAD_DIM, (h + 1) * X_HEAD_DIM)


def _outproj_body(x, att, y, wa_ref, wy_ref, g_ref, wq_ref, gq_ref):
    x1 = x + _dot(att.astype(BF16), wa_ref[...]) + _dot(y.astype(BF16), wy_ref[...])
    qx = _dot(_rms(x1, g_ref[...]).astype(BF16), wq_ref[...])
    gq = gq_ref[...]
    return x1, [_rms(qx[:, _head_slice(h)], gq) for h in range(X_HEADS)]


def _outproj_kernel(x_ref, att_ref, y_ref, wa_ref, wy_ref, g_ref, wq_ref, gq_ref, x1_ref, qx_ref):
    x1, q_heads = _outproj_body(x_ref[...], att_ref[...], y_ref[...], wa_ref, wy_ref, g_ref, wq_ref, gq_ref)
    x1_ref[...] = x1
    for h in range(X_HEADS):
        qx_ref[:, _head_slice(h)] = q_heads[h].astype(qx_ref.dtype)


def _outproj(x, att, y, w_att, w_y, g, wq, gq, tm, act_dtype):
    n = x.shape[0]
    row = lambda w: pl.BlockSpec((tm, w), lambda i: (i, 0))
    return pl.pallas_call(
        _outproj_kernel,
        grid=(n // tm,),
        in_specs=[row(D_MODEL), row(ATT_WIDTH), row(SSM_INNER), _const(w_att.shape), _const(w_y.shape),
                  _const(g.shape), _const(wq.shape), _const(gq.shape)],
        out_specs=[row(D_MODEL), row(X_WIDTH)],
        out_shape=[jax.ShapeDtypeStruct((n, D_MODEL), F32), jax.ShapeDtypeStruct((n, X_WIDTH), act_dtype)],
        compiler_params=_params("parallel"),
        name="outproj",
    )(x, att, y, w_att, w_y, g, wq, gq)


def _xattn_head(q, k, v):
    s = _dot_nt((q.astype(F32) * (X_HEAD_DIM ** -0.5)).astype(BF16), k.astype(BF16))
    p = jnp.exp(s - jnp.max(s, axis=-1, keepdims=True))
    return _dot(p.astype(BF16), v.astype(BF16)) / jnp.sum(p, axis=-1, keepdims=True)


def _xattn_rows_kernel(q_ref, k_ref, v_ref, o_ref):
    gb, t, _ = q_ref.shape
    flat = k_ref.shape[1]
    rowh = lax.broadcasted_iota(jnp.int32, (X_HEADS * t, flat), 0) // t
    colh = lax.broadcasted_iota(jnp.int32, (X_HEADS * t, flat), 1) % X_HEADS
    hmask = jnp.where(rowh == colh, 0.0, NEG)
    for g in range(gb):
        q = q_ref[g].astype(F32)
        q_rows = jnp.concatenate([q[:, _head_slice(h)] for h in range(X_HEADS)], axis=0)
        s = _dot_nt((q_rows * (X_HEAD_DIM ** -0.5)).astype(BF16), k_ref[g].astype(BF16)) + hmask
        p = jnp.exp(s - jnp.max(s, axis=-1, keepdims=True))
        o_rows = _dot(p.astype(BF16), v_ref[g].astype(BF16)) / jnp.sum(p, axis=-1, keepdims=True)
        for h in range(X_HEADS):
            o_ref[g, :, _head_slice(h)] = o_rows[h * t:(h + 1) * t].astype(o_ref.dtype)


def _xattn_rows(q, mk, mv, act_dtype):
    b, t, _ = q.shape
    flat = mk.shape[1]
    gb = math.gcd(b, XATTN_SEQS_PER_STEP)
    qspec = pl.BlockSpec((gb, t, X_WIDTH), lambda bi: (bi, 0, 0))
    kvspec = pl.BlockSpec((gb, flat, X_HEAD_DIM), lambda bi: (bi, 0, 0))
    return pl.pallas_call(
        _xattn_rows_kernel,
        grid=(b // gb,),
        in_specs=[qspec, kvspec, kvspec],
        out_specs=qspec,
        out_shape=jax.ShapeDtypeStruct((b, t, X_WIDTH), act_dtype),
        compiler_params=_params("parallel"),
        name="xattn",
    )(q, mk, mv)


def _mlp_body(x1, o, wo_ref, g_ref, wu_ref, wd_ref):
    x2 = x1 + _dot(o.astype(BF16), wo_ref[...])
    hm = _rms(x2, g_ref[...]).astype(BF16)
    acc = x2
    for cidx in range(D_FF // FF_CHUNK):
        sl = slice(cidx * FF_CHUNK, (cidx + 1) * FF_CHUNK)
        up = jnp.maximum(_dot(hm, wu_ref[:, sl]), 0.0)
        acc = acc + _dot((up * up).astype(BF16), wd_ref[sl, :])
    return acc


def _mlp_kernel(x1_ref, o_ref, wo_ref, g_ref, wu_ref, wd_ref, out_ref):
    out_ref[...] = _mlp_body(x1_ref[...], o_ref[...], wo_ref, g_ref, wu_ref, wd_ref)


def _mlp(x1, o, wo, g, wu, wd, tm):
    n = x1.shape[0]
    row = lambda w: pl.BlockSpec((tm, w), lambda i: (i, 0))
    return pl.pallas_call(
        _mlp_kernel,
        grid=(n // tm,),
        in_specs=[row(D_MODEL), row(X_WIDTH), _const_single(wo.shape), _const(g.shape),
                  _const_single(wu.shape), _const_single(wd.shape)],
        out_specs=row(D_MODEL),
        out_shape=jax.ShapeDtypeStruct((n, D_MODEL), F32),
        compiler_params=_params("parallel"),
        name="mlp",
    )(x1, o, wo, g, wu, wd)


def _post_kernel(x_ref, att_ref, y_ref, mk_ref, mv_ref, wa_ref, wy_ref, gx_ref, wq_ref, gq_ref,
                 wo_ref, gm_ref, wu_ref, wd_ref, out_ref):
    x1, q_heads = _outproj_body(x_ref[...], att_ref[...], y_ref[...], wa_ref, wy_ref, gx_ref, wq_ref, gq_ref)
    o = jnp.concatenate([_xattn_head(q_heads[h], mk_ref[0, :, _head_slice(h)], mv_ref[0, :, _head_slice(h)])
                         for h in range(X_HEADS)], axis=1)
    out_ref[...] = _mlp_body(x1, o, wo_ref, gm_ref, wu_ref, wd_ref)


def _post(x, att, y, mk, mv, wts, tm, t):
    n = x.shape[0]
    tiles_per_seq = t // tm
    m = mk.shape[1]
    row = lambda w: pl.BlockSpec((tm, w), lambda i: (i, 0))
    mem = pl.BlockSpec((1, m, X_WIDTH), lambda i: (i // tiles_per_seq, 0, 0))
    names = ("w_out_att", "w_out_ssm", "ln_x_g", "wq_x", "qx_norm_g", "wo_x", "ln_mlp_g", "w_up", "w_down")
    big = ("w_up", "w_down")
    weights = [wts[k] for k in names]
    return pl.pallas_call(
        _post_kernel,
        grid=(n // tm,),
        in_specs=[row(D_MODEL), row(ATT_WIDTH), row(SSM_INNER), mem, mem]
                 + [(_const_single if k in big else _const)(wts[k].shape) for k in names],
        out_specs=row(D_MODEL),
        out_shape=jax.ShapeDtypeStruct((n, D_MODEL), F32),
        compiler_params=_params("parallel"),
        name="post",
    )(x, att, y, mk, mv, *weights)


def _memkv_kernel(m_ref, g_ref, wk_ref, wv_ref, gk_ref, mk_ref, mv_ref):
    mn = _rms(m_ref[...], g_ref[...]).astype(BF16)
    mk = _dot(mn, wk_ref[...])
    gk = gk_ref[...]
    for h in range(X_HEADS):
        sl = slice(h * X_HEAD_DIM, (h + 1) * X_HEAD_DIM)
        mk_ref[:, sl] = _rms(mk[:, sl], gk)
    mv_ref[...] = _dot(mn, wv_ref[...])


def _memkv(mem, g, wk, wv, gk, tm):
    n = mem.shape[0]
    row = lambda w: pl.BlockSpec((tm, w), lambda i: (i, 0))
    return pl.pallas_call(
        _memkv_kernel,
        grid=(n // tm,),
        in_specs=[row(D_MODEL), _const(g.shape), _const(wk.shape), _const(wv.shape), _const(gk.shape)],
        out_specs=[row(X_WIDTH), row(X_WIDTH)],
        out_shape=[jax.ShapeDtypeStruct((n, X_WIDTH), F32)] * 2,
        compiler_params=_params("parallel"),
        name="memkv",
    )(mem, g, wk, wv, gk)


def _rope_tables(pos):
    half = HEAD_DIM // 2
    inv_freq = ROPE_THETA ** (-jnp.arange(half, dtype=F32) / half)
    ang = pos.astype(F32)[:, None] * inv_freq[None, :]
    return jnp.cos(ang).T, jnp.sin(ang).T


def _row_tile(n, want):
    tm = min(n, want)
    assert n % tm == 0
    return tm


def _layer(x, nb, cos_t, sin_t, conv0, h0, mem_k, mem_v, wts, attend, tm, act_dtype):
    b, t, _ = x.shape
    n = b * t
    xf = x.reshape(n, D_MODEL)
    qt, kt, vt, z, xbc, dtr = _inproj(xf, nb, wts["ln_mix_g"], wts["w_qkv_t"], wts["w_rest_t"], wts["w_dt_t"],
                                      wts["q_norm_g"], wts["k_norm_g"], cos_t, sin_t, tm)
    att, extras = attend(qt, kt, vt)

    cl = SSD_CHUNK if t >= SSD_CHUNK else -(-t // SSD_MIN_CHUNK) * SSD_MIN_CHUNK
    tp = -(-t // cl) * cl
    pad3 = lambda a: jnp.pad(a.reshape(b, t, -1), ((0, 0), (0, tp - t), (0, 0)))
    y, new_conv, new_h = _ssd(pad3(xbc), pad3(dtr), pad3(z), conv0, h0, wts["conv_w"], wts["conv_b"],
                              wts["dt_bias"], wts["a_log"], wts["d_skip"], wts["ssm_norm_g"],
                              cl, min(t, cl) if tp != t else cl, act_dtype)
    y = y[:, :t].reshape(n, SSM_INNER)

    if t % tm == 0:
        out = _post(xf, att, y, mem_k, mem_v, wts, tm, t)
    else:
        x1, qx = _outproj(xf, att, y, wts["w_out_att"], wts["w_out_ssm"], wts["ln_x_g"],
                          wts["wq_x"], wts["qx_norm_g"], tm, act_dtype)
        o = _xattn_rows(qx.reshape(b, t, X_WIDTH), mem_k, mem_v, act_dtype)
        out = _mlp(x1, o.reshape(n, X_WIDTH), wts["wo_x"], wts["ln_mlp_g"], wts["w_up"], wts["w_down"], tm)
    return out.reshape(b, t, D_MODEL), extras, new_conv, new_h


def kernel(x_prompt, x_sample, mem_prompt, cache_k, cache_v, page_table, state_conv, state_ssm,
           cache_mem_k, cache_mem_v, ln_mix_g, w_in, q_norm_g, k_norm_g, conv_w, conv_b, dt_bias,
           a_log, d_skip, ssm_norm_g, w_out, ln_x_g, ln_mem_g, wq_x, wk_x, wv_x, qx_norm_g,
           kx_norm_g, wo_x, ln_mlp_g, w_up, w_down):
    depth = w_in.shape[0]
    assert depth == 1, "single-layer step"
    bp, tp, _ = x_prompt.shape
    bs, ts, _ = x_sample.shape
    page = cache_k.shape[2]
    past_len = page_table.shape[1] * page
    assert tp % MOBA_BLOCK == 0 and past_len % MOBA_BLOCK == 0 and MOBA_BLOCK % page == 0
    assert ts <= MOBA_BLOCK and ts <= SSD_CHUNK and ts >= CONV_W - 1
    l = 0
    qkv_cols = 3 * ATT_WIDTH
    main_cols = qkv_cols + SSM_INNER + CONV_DIM

    def lanes(vec, width=LANES):
        return jnp.pad(vec.astype(F32), (0, width - vec.shape[0])).reshape(1, width)

    def dim_gain(vec):
        return jnp.broadcast_to(vec.astype(F32)[:, None], (HEAD_DIM, LANES))

    wts = {
        "ln_mix_g": ln_mix_g[l].reshape(1, D_MODEL),
        "w_qkv_t": w_in[l][:, :qkv_cols].T.astype(BF16),
        "w_rest_t": w_in[l][:, qkv_cols:main_cols].T.astype(BF16),
        "w_dt_t": jnp.pad(w_in[l][:, main_cols:].T, ((0, LANES - SSM_HEADS), (0, 0))).astype(BF16),
        "q_norm_g": dim_gain(q_norm_g[l]),
        "k_norm_g": dim_gain(k_norm_g[l]),
        "conv_w": conv_w[l],
        "conv_b": conv_b[l].reshape(1, CONV_DIM),
        "dt_bias": lanes(dt_bias[l]),
        "a_log": lanes(a_log[l]),
        "d_skip": jnp.repeat(d_skip[l].astype(F32), SSM_HEAD_DIM).reshape(1, SSM_INNER),
        "ssm_norm_g": ssm_norm_g[l].reshape(1, SSM_INNER),
        "w_out_att": w_out[l][:ATT_WIDTH].astype(BF16),
        "w_out_ssm": w_out[l][ATT_WIDTH:].astype(BF16),
        "ln_x_g": ln_x_g[l].reshape(1, D_MODEL),
        "wq_x": wq_x[l].astype(BF16),
        "qx_norm_g": qx_norm_g[l].reshape(1, X_HEAD_DIM),
        "wo_x": wo_x[l].astype(BF16),
        "ln_mlp_g": ln_mlp_g[l].reshape(1, D_MODEL),
        "w_up": w_up[l].astype(BF16),
        "w_down": w_down[l].astype(BF16),
    }

    mem_rows = mem_prompt.reshape(-1, D_MODEL)
    mk_p, mv_p = _memkv(mem_rows, ln_mem_g[l].reshape(1, D_MODEL), wk_x[l].astype(BF16),
                        wv_x[l].astype(BF16), kx_norm_g[l].reshape(1, X_HEAD_DIM),
                        _row_tile(mem_rows.shape[0], 512))
    mem_len = mem_prompt.shape[1]
    mk_p = mk_p.reshape(bp, mem_len, X_WIDTH)
    mv_p = mv_p.reshape(bp, mem_len, X_WIDTH)
    cos_p, sin_p = _rope_tables(jnp.arange(tp, dtype=jnp.int32))
    conv0_p = jnp.zeros((bp, CONV_PAD, CONV_DIM), F32)
    h0_p = jnp.zeros((bp, SSM_HEADS * SSM_HEAD_DIM, SSM_STATE), F32)

    def attend_prompt(qt, kt, vt):
        heads_t = lambda a: a.reshape(1, bp, ATT_HEADS, HEAD_DIM, tp).transpose(0, 1, 4, 2, 3)
        return _moba_prompt(qt, kt, vt).reshape(bp * tp, ATT_WIDTH), (heads_t(kt), heads_t(vt))

    y_p, (k_p, v_p), conv_p, ssm_p = _layer(x_prompt, bp, cos_p, sin_p, conv0_p, h0_p, mk_p, mv_p, wts,
                                            attend_prompt, _row_tile(tp, 512), BF16)

    cos_s, sin_s = _rope_tables(past_len + jnp.arange(ts, dtype=jnp.int32))
    cos_s = jnp.tile(cos_s, (1, bs))
    sin_s = jnp.tile(sin_s, (1, bs))
    conv0_s = jnp.pad(state_conv[l], ((0, 0), (CONV_PAD - (CONV_W - 1), 0), (0, 0)))
    h0_s = state_ssm[l].reshape(bs, SSM_HEADS * SSM_HEAD_DIM, SSM_STATE)
    cache_kt = cache_k.transpose(0, 1, 3, 4, 2)
    cache_vt = cache_v.transpose(0, 1, 3, 4, 2)

    def attend_sample(qt, kt, vt):
        rows = lambda a: a[0].T.reshape(bs, ts, ATT_WIDTH)
        k_new, v_new = rows(kt), rows(vt)
        o = _sample_moba(page_table, rows(qt), k_new, v_new, cache_kt, cache_vt)
        heads = lambda a: a.reshape(1, bs, ts, ATT_HEADS, HEAD_DIM)
        return o.reshape(bs * ts, ATT_WIDTH), (heads(k_new), heads(v_new))

    y_s, (k_s, v_s), conv_s, ssm_s = _layer(
        x_sample, 1, cos_s, sin_s, conv0_s, h0_s, cache_mem_k[l].reshape(bs, -1, X_HEAD_DIM),
        cache_mem_v[l].reshape(bs, -1, X_HEAD_DIM), wts, attend_sample, bs * ts, F32)

    state = lambda a, b: a.reshape(1, b, SSM_HEADS, SSM_HEAD_DIM, SSM_STATE)
    return (y_p, y_s, k_p, v_p, conv_p[None], state(ssm_p, bp),
            mk_p.reshape(1, bp, mem_len, X_HEADS, X_HEAD_DIM), mv_p.reshape(1, bp, mem_len, X_HEADS, X_HEAD_DIM),
            k_s, v_s, conv_s[None], state(ssm_s, bs))
```

```python
import functools
import math

import jax
import jax.numpy as jnp
from jax import lax
from jax.experimental import pallas as pl
from jax.experimental.pallas import tpu as pltpu

F32 = jnp.float32
BF16 = jnp.bfloat16

D_MODEL = 1024
ATT_HEADS = 8
HEAD_DIM = 64
ATT_WIDTH = ATT_HEADS * HEAD_DIM
MOBA_BLOCK = 256
TOPK = 3
ROPE_THETA = 10000.0
SSM_HEADS = 8
SSM_HEAD_DIM = 64
SSM_INNER = SSM_HEADS * SSM_HEAD_DIM
SSM_GROUPS = 2
SSM_STATE = 128
CONV_W = 4
CONV_DIM = SSM_INNER + 2 * SSM_GROUPS * SSM_STATE
SSD_CHUNK = 128
X_HEADS = 4
X_HEAD_DIM = 128
X_WIDTH = X_HEADS * X_HEAD_DIM
D_FF = 4 * D_MODEL
EPS = 1e-6

LANES = 128
CONV_PAD = 8
VMEM_LIMIT = 56 * 1024 * 1024
NEG = -1e30
LOG2E = 1.4426950408889634
MOBA_GROUP_PAIRED = 2
MOBA_GROUP = 4
ONES_ROWS = 16
FF_CHUNK = 1024
XATTN_SEQS_PER_STEP = 8
SSD_MIN_CHUNK = 16
SSD_SEQS_PER_STEP = 8
PAGES_PER_STEP = 32


def _dot(a, b):
    return jnp.dot(a, b, preferred_element_type=F32)


def _dot_nt(a, b):
    return lax.dot_general(a, b, (((1,), (1,)), ((), ())), preferred_element_type=F32)


def _dot_tn(a, b):
    return lax.dot_general(a, b, (((0,), (0,)), ((), ())), preferred_element_type=F32)


def _split3(x):
    hi = x.astype(BF16)
    r = x - hi.astype(F32)
    mid = r.astype(BF16)
    lo = (r - mid.astype(F32)).astype(BF16)
    return hi, mid, lo


def _dot_f32(a, b, dot=None):
    dot = dot or _dot
    a0, a1, a2 = _split3(a)
    b0, b1, b2 = _split3(b)
    return (dot(a0, b0) + (dot(a0, b1) + dot(a1, b0)) + (dot(a1, b1) + dot(a0, b2) + dot(a2, b0)))


def _rms(x, g):
    ms = jnp.mean(x * x, axis=-1, keepdims=True)
    return x * lax.rsqrt(ms + EPS) * g


def _sigmoid(x):
    return 1.0 / (1.0 + jnp.exp(-x))


def _params(*sem):
    return pltpu.CompilerParams(dimension_semantics=sem, vmem_limit_bytes=VMEM_LIMIT)


def _const(shape):
    return pl.BlockSpec(shape, lambda *_: (0,) * len(shape))


def _const_single(shape):
    return pl.BlockSpec(shape, lambda *_: (0,) * len(shape), pipeline_mode=pl.Buffered(1))


def _inproj_kernel(x_ref, g_ref, wqkv_ref, wrest_ref, wdt_ref, gq_ref, gk_ref, cos_ref, sin_ref,
                   q_ref, k_ref, v_ref, z_ref, xbc_ref, dt_ref):
    hn = _rms(x_ref[...], g_ref[...]).astype(BF16)
    tm = hn.shape[0]
    cos = cos_ref[...]
    sin = sin_ref[...]
    half = HEAD_DIM // 2

    def proj_t(idx):
        return _dot_nt(wqkv_ref[idx * ATT_WIDTH:(idx + 1) * ATT_WIDTH, :], hn)

    def head_norm_rot(p, g_ref_):
        p3 = p.reshape(ATT_HEADS, HEAD_DIM, tm)
        gain = g_ref_[:, 0:1]
        pn = p3 * lax.rsqrt(jnp.mean(p3 * p3, axis=1, keepdims=True) + EPS) * gain
        x1 = pn[:, :half]
        x2 = pn[:, half:]
        rot = jnp.concatenate([x1 * cos - x2 * sin, x2 * cos + x1 * sin], axis=1)
        return rot.reshape(ATT_WIDTH, tm)

    q_ref[0] = head_norm_rot(proj_t(0), gq_ref)
    k_ref[0] = head_norm_rot(proj_t(1), gk_ref)
    v_ref[0] = proj_t(2)
    z_ref[...] = _dot_nt(hn, wrest_ref[0:SSM_INNER, :])
    xbc_ref[...] = _dot_nt(hn, wrest_ref[SSM_INNER:SSM_INNER + CONV_DIM, :])
    dt_ref[...] = _dot_nt(hn, wdt_ref[...])


def _inproj(x, nb, g, w_qkv_t, w_rest, w_dt, gq, gk, cos_t, sin_t, tm):
    n = x.shape[0]
    cols = n // nb
    tpb = cols // tm
    row = lambda w: pl.BlockSpec((tm, w), lambda i: (i, 0))
    tab = pl.BlockSpec((HEAD_DIM // 2, tm), lambda i: (0, i % tpb))
    tspec = pl.BlockSpec((1, ATT_WIDTH, tm), lambda i: (i // tpb, 0, i % tpb))
    outs = ([jax.ShapeDtypeStruct((nb, ATT_WIDTH, cols), F32)] * 3
            + [jax.ShapeDtypeStruct((n, w), F32) for w in (SSM_INNER, CONV_DIM, LANES)])
    return pl.pallas_call(
        _inproj_kernel,
        grid=(n // tm,),
        in_specs=[row(D_MODEL), _const(g.shape), _const_single(w_qkv_t.shape), _const_single(w_rest.shape),
                  _const(w_dt.shape), _const(gq.shape), _const(gk.shape), tab, tab],
        out_specs=[tspec, tspec, tspec, row(SSM_INNER), row(CONV_DIM), row(LANES)],
        out_shape=outs,
        compiler_params=_params("parallel"),
        name="inproj",
    )(x, g, w_qkv_t, w_rest, w_dt, gq, gk, cos_t, sin_t)


def _topk_keep(gate, n_valid, axis):
    idx = lax.broadcasted_iota(jnp.int32, gate.shape, axis)
    cnt = jnp.zeros(gate.shape, jnp.int32)
    for jp in range(gate.shape[axis]):
        gj = lax.slice_in_dim(gate, jp, jp + 1, axis=axis)
        beats = (gj > gate) | ((gj == gate) & (jp < idx))
        cnt = cnt + jnp.where(beats, jnp.where(jp < n_valid, 1, 0), 0)
    return (idx < n_valid) & (cnt < TOPK)


def _moba_groups(nblk, group):
    order = [blk for pair in zip(range(nblk), reversed(range(nblk))) for blk in pair][:nblk]
    return [order[g0:g0 + group] for g0 in range(0, nblk, group)]


def _moba_kernel(qt_ref, kt_ref, vt_ref, o_ref, kb_ref, vb_ref, km_ref, *, nblk):
    _moba_prep(kt_ref, vt_ref, kb_ref, vb_ref, km_ref, nblk)
    for owns in _moba_groups(nblk, MOBA_GROUP):
        _moba_group(owns, qt_ref, o_ref, kb_ref, vb_ref, km_ref)


def _moba_prep(kt_ref, vt_ref, kb_ref, vb_ref, km_ref, nblk):
    bq = MOBA_BLOCK
    km_ref[...] = jnp.zeros(km_ref.shape, F32)
    for j in range(nblk):
        kj = kt_ref[0, :, j * bq:(j + 1) * bq].T
        kb_ref[j] = kj.astype(BF16)
        km_ref[j:j + 1, :] = jnp.mean(kj, axis=0, keepdims=True)
        vj = vt_ref[0, :, j * bq:(j + 1) * bq].astype(BF16)
        for h in range(2):
            vb_ref[j, h, 0:HEAD_DIM, :] = vj[h * HEAD_DIM:(h + 1) * HEAD_DIM, :]
            vb_ref[j, h, HEAD_DIM:, :] = jnp.ones((ONES_ROWS, bq), BF16)


def _moba_group(owns, qt_ref, o_ref, kb_ref, vb_ref, km_ref):
    bq = MOBA_BLOCK
    kmean = km_ref[...]
    keyi = lax.broadcasted_iota(jnp.int32, (bq, bq), 0)
    qryi = lax.broadcasted_iota(jnp.int32, (bq, bq), 1)
    causal = jnp.where(keyi <= qryi, 0.0, NEG)
    row_lo = lax.broadcasted_iota(jnp.int32, (LANES, bq), 0) < HEAD_DIM
    qscale = (HEAD_DIM ** -0.5) * LOG2E

    work = []
    for own in owns:
        qt = qt_ref[0, :, own * bq:(own + 1) * bq]
        for h in range(2):
            qh = jnp.where(row_lo, 0.0, qt) if h else jnp.where(row_lo, qt, 0.0)
            qsb = (qh * qscale).astype(BF16)
            bias = None
            if own > TOPK:
                keep = _topk_keep(_dot_f32(kmean, qh), own, 0)
                bias = jnp.where(keep, 0.0, NEG)
            tiles = []
            m = None
            for j in range(own + 1):
                s = _dot(kb_ref[j], qsb)
                if j == own:
                    s = s + causal
                elif bias is not None:
                    s = s + bias[j:j + 1, :]
                tiles.append(s)
                mj = jnp.max(s, axis=0, keepdims=True)
                m = mj if m is None else jnp.maximum(m, mj)
            work.append((own, h, tiles, m))
    outs = {}
    for own, h, tiles, m in work:
        acc = jnp.zeros((HEAD_DIM + ONES_ROWS, bq), F32)
        for j in range(own + 1):
            p = jnp.exp2(tiles[j] - m)
            acc = acc + _dot(vb_ref[j, h], p.astype(BF16))
        outs[(own, h)] = acc[0:HEAD_DIM] / acc[HEAD_DIM:HEAD_DIM + 1]
    for own in owns:
        o_t = jnp.concatenate([outs[(own, 0)], outs[(own, 1)]], axis=0)
        o_ref[0, own * bq:(own + 1) * bq, :] = o_t.T.astype(o_ref.dtype)


def _moba_prompt(qt, kt, vt):
    b, _, t = qt.shape
    nblk = t // MOBA_BLOCK
    npad = max(16, nblk)
    npair = ATT_WIDTH // LANES
    spec = pl.BlockSpec((1, LANES, t), lambda bi, p: (bi, p, 0))
    return pl.pallas_call(
        functools.partial(_moba_kernel, nblk=nblk),
        grid=(b, npair),
        in_specs=[spec, spec, spec],
        out_specs=pl.BlockSpec((1, t, LANES), lambda bi, p: (bi, 0, p)),
        out_shape=jax.ShapeDtypeStruct((b, t, ATT_WIDTH), BF16),
        scratch_shapes=[pltpu.VMEM((nblk, MOBA_BLOCK, LANES), BF16),
                        pltpu.VMEM((nblk, 2, HEAD_DIM + ONES_ROWS, MOBA_BLOCK), BF16),
                        pltpu.VMEM((npad, LANES), F32)],
        compiler_params=_params("parallel", "parallel"),
        name="moba_prompt",
    )(qt, kt, vt)


def _expand_heads(q):
    t = q.shape[0]
    tiled = jnp.concatenate([q] * ATT_HEADS, axis=0)
    rowh = lax.broadcasted_iota(jnp.int32, tiled.shape, 0) // t
    laneh = lax.broadcasted_iota(jnp.int32, tiled.shape, 1) // HEAD_DIM
    return jnp.where(rowh == laneh, tiled, 0.0)


def _sample_moba_kernel(pt_ref, q_ref, kn_ref, vn_ref, *rest, page, npg, nblk):
    del pt_ref
    _sample_moba_step(pl.program_id(1), pl.num_programs(1), q_ref, kn_ref, vn_ref, rest[:npg],
                      rest[npg:2 * npg], *rest[2 * npg:], page=page, nblk=nblk)


def _sample_moba_step(c, nsteps, q_ref, kn_ref, vn_ref, k_refs, v_refs, o_ref, km_ref, m_ref, l_ref, oblk_ref,
                      *, page, nblk):
    npg = len(k_refs)
    t = q_ref.shape[1]
    rows = ATT_HEADS * t
    pages_per_block = MOBA_BLOCK // page
    blocks_per_step = npg // pages_per_block

    qe = _expand_heads(q_ref[0])
    qb = (qe * ((HEAD_DIM ** -0.5) * LOG2E)).astype(BF16)
    blk_lane = lax.broadcasted_iota(jnp.int32, (ATT_WIDTH, LANES), 1)

    @pl.when(c == 0)
    def _():
        km_ref[...] = jnp.zeros(km_ref.shape, F32)

    tiles = []
    for jj in range(blocks_per_step):
        ksum = None
        for u in range(pages_per_block):
            kt = k_refs[jj * pages_per_block + u][...].reshape(ATT_WIDTH, page)
            ksum = kt if ksum is None else ksum + kt
            tiles.append(_dot(qb, kt.astype(BF16)))
        kmean = jnp.sum(ksum, axis=1, keepdims=True) * (1.0 / MOBA_BLOCK)
        km_ref[...] = jnp.where(blk_lane == c * blocks_per_step + jj, kmean, km_ref[...])
    probs = []
    for jj in range(blocks_per_step):
        mine = tiles[jj * pages_per_block:(jj + 1) * pages_per_block]
        m = jnp.max(mine[0], axis=1, keepdims=True)
        for s in mine[1:]:
            m = jnp.maximum(m, jnp.max(s, axis=1, keepdims=True))
        l = jnp.zeros((rows, 1), F32)
        for s in mine:
            p = jnp.exp2(s - m)
            l = l + jnp.sum(p, axis=1, keepdims=True)
            probs.append(p.astype(BF16))
        m_ref[c * blocks_per_step + jj] = jnp.broadcast_to(m, (rows, LANES))
        l_ref[c * blocks_per_step + jj] = jnp.broadcast_to(l, (rows, LANES))
    for jj in range(blocks_per_step):
        o = jnp.zeros((rows, ATT_WIDTH), F32)
        for u in range(pages_per_block):
            vt = v_refs[jj * pages_per_block + u][...].reshape(ATT_WIDTH, page).astype(BF16)
            o = o + _dot_nt(probs[jj * pages_per_block + u], vt)
        oblk_ref[c * blocks_per_step + jj] = o

    @pl.when(c == nsteps - 1)
    def _():
        keep = _topk_keep(_dot_f32(qe, km_ref[:, 0:nblk]), nblk, 1)
        s_own = _dot_nt(qb, kn_ref[0].astype(BF16))
        qpos = lax.broadcasted_iota(jnp.int32, s_own.shape, 0) % t
        kpos = lax.broadcasted_iota(jnp.int32, s_own.shape, 1)
        s_own = jnp.where(kpos <= qpos, s_own, NEG)
        lane_j = lax.broadcasted_iota(jnp.int32, (rows, nblk), 1)
        m_all = jnp.zeros((rows, nblk), F32)
        l_all = jnp.zeros((rows, nblk), F32)
        for j in range(nblk):
            m_all = jnp.where(lane_j == j, m_ref[j][:, 0:nblk], m_all)
            l_all = jnp.where(lane_j == j, l_ref[j][:, 0:nblk], l_all)
        mx = jnp.maximum(jnp.max(s_own, axis=1, keepdims=True),
                         jnp.max(jnp.where(keep, m_all, NEG), axis=1, keepdims=True))
        p_own = jnp.exp2(s_own - mx)
        w = jnp.where(keep, jnp.exp2(jnp.minimum(m_all - mx, 0.0)), 0.0)
        den = jnp.sum(p_own, axis=1, keepdims=True) + jnp.sum(w * l_all, axis=1, keepdims=True)
        vn = vn_ref[0]
        num = jnp.zeros((rows, ATT_WIDTH), F32)
        for tt in range(t):
            num = num + p_own[:, tt:tt + 1] * vn[tt:tt + 1, :]
        for j in range(nblk):
            num = num + w[:, j:j + 1] * oblk_ref[j]
        full = num / den
        laneh = lax.broadcasted_iota(jnp.int32, (t, ATT_WIDTH), 1) // HEAD_DIM
        out = jnp.zeros((t, ATT_WIDTH), F32)
        for h in range(ATT_HEADS):
            out = out + jnp.where(laneh == h, full[h * t:(h + 1) * t, :], 0.0)
        o_ref[0] = out


def _sample_moba(page_table, q, k_new, v_new, cache_kt, cache_vt):
    b, t, _ = q.shape
    page = cache_kt.shape[4]
    n_pages = page_table.shape[1]
    npg = math.gcd(n_pages, PAGES_PER_STEP)
    nsteps = n_pages // npg
    nblk = n_pages * page // MOBA_BLOCK
    assert nblk <= LANES and npg % (MOBA_BLOCK // page) == 0
    rows = ATT_HEADS * t
    per_b = pl.BlockSpec((1, t, ATT_WIDTH), lambda bi, c, pt: (bi, 0, 0))

    def pspec(pi):
        return pl.BlockSpec((None, None, ATT_HEADS, HEAD_DIM, page),
                            lambda bi, c, pt: (0, pt[bi, c * npg + pi], 0, 0, 0))

    grid_spec = pltpu.PrefetchScalarGridSpec(
        num_scalar_prefetch=1,
        grid=(b, nsteps),
        in_specs=[per_b, per_b, per_b] + [pspec(pi) for pi in range(npg)] * 2,
        out_specs=per_b,
        scratch_shapes=[pltpu.VMEM((ATT_WIDTH, LANES), F32),
                        pltpu.VMEM((nblk, rows, LANES), F32),
                        pltpu.VMEM((nblk, rows, LANES), F32),
                        pltpu.VMEM((nblk, rows, ATT_WIDTH), F32)],
    )
    return pl.pallas_call(
        functools.partial(_sample_moba_kernel, page=page, npg=npg, nblk=nblk),
        grid_spec=grid_spec,
        out_shape=jax.ShapeDtypeStruct((b, t, ATT_WIDTH), F32),
        compiler_params=_params("parallel", "arbitrary"),
        name="sample_moba",
    )(page_table, q, k_new, v_new, *([cache_kt] * npg), *([cache_vt] * npg))


def _moba_both_kernel(pt_ref, qt_ref, kt_ref, vt_ref, q_ref, kn_ref, vn_ref, *rest, nblk_p, groups, page, npg,
                      nblk_s):
    del pt_ref
    k_refs = rest[:npg]
    v_refs = rest[npg:2 * npg]
    op_ref, os_ref, kb_ref, vb_ref, kmp_ref, kms_ref, m_ref, l_ref, oblk_ref = rest[2 * npg:]
    g = pl.program_id(0) % len(groups)

    @pl.when(g == 0)
    def _():
        _moba_prep(kt_ref, vt_ref, kb_ref, vb_ref, kmp_ref, nblk_p)

    for gi, owns in enumerate(groups):
        pl.when(g == gi)(functools.partial(_moba_group, owns, qt_ref, op_ref, kb_ref, vb_ref, kmp_ref))
    _sample_moba_step(g, len(groups), q_ref, kn_ref, vn_ref, k_refs, v_refs, os_ref, kms_ref, m_ref, l_ref,
                      oblk_ref, page=page, nblk=nblk_s)


def _moba_both_fits(qt, q, page_table, page):
    nblk_p = qt.shape[2] // MOBA_BLOCK
    ngroups = len(_moba_groups(nblk_p, MOBA_GROUP_PAIRED))
    n_pages = page_table.shape[1]
    return (q.shape[0] == qt.shape[0] * (ATT_WIDTH // LANES) and n_pages % ngroups == 0
            and (n_pages // ngroups) % (MOBA_BLOCK // page) == 0)


def _moba_both(page_table, qt, kt, vt, q, k_new, v_new, cache_kt, cache_vt):
    b, _, t = qt.shape
    bs, ts, _ = q.shape
    page = cache_kt.shape[4]
    nblk_p = t // MOBA_BLOCK
    npad = max(16, nblk_p)
    npair = ATT_WIDTH // LANES
    groups = _moba_groups(nblk_p, MOBA_GROUP_PAIRED)
    ngroups = len(groups)
    n_pages = page_table.shape[1]
    npg = n_pages // ngroups
    nblk_s = n_pages * page // MOBA_BLOCK
    assert nblk_s <= LANES
    rows = ATT_HEADS * ts
    pspec_in = pl.BlockSpec((1, LANES, t), lambda s, pt: (s // ngroups // npair, s // ngroups % npair, 0))
    pspec_out = pl.BlockSpec((1, t, LANES), lambda s, pt: (s // ngroups // npair, 0, s // ngroups % npair))
    per_seq = pl.BlockSpec((1, ts, ATT_WIDTH), lambda s, pt: (s // ngroups, 0, 0))

    def pages(pi):
        return pl.BlockSpec((None, None, ATT_HEADS, HEAD_DIM, page),
                            lambda s, pt: (0, pt[s // ngroups, s % ngroups * npg + pi], 0, 0, 0))

    grid_spec = pltpu.PrefetchScalarGridSpec(
        num_scalar_prefetch=1,
        grid=(bs * ngroups,),
        in_specs=[pspec_in] * 3 + [per_seq] * 3 + [pages(pi) for pi in range(npg)] * 2,
        out_specs=[pspec_out, per_seq],
        scratch_shapes=[pltpu.VMEM((nblk_p, MOBA_BLOCK, LANES), BF16),
                        pltpu.VMEM((nblk_p, 2, HEAD_DIM + ONES_ROWS, MOBA_BLOCK), BF16),
                        pltpu.VMEM((npad, LANES), F32),
                        pltpu.VMEM((ATT_WIDTH, LANES), F32),
                        pltpu.VMEM((nblk_s, rows, LANES), F32),
                        pltpu.VMEM((nblk_s, rows, LANES), F32),
                        pltpu.VMEM((nblk_s, rows, ATT_WIDTH), F32)],
    )
    return pl.pallas_call(
        functools.partial(_moba_both_kernel, nblk_p=nblk_p, groups=groups, page=page, npg=npg, nblk_s=nblk_s),
        grid_spec=grid_spec,
        out_shape=[jax.ShapeDtypeStruct((b, t, ATT_WIDTH), BF16), jax.ShapeDtypeStruct((bs, ts, ATT_WIDTH), F32)],
        compiler_params=_params("arbitrary"),
        name="moba_both",
    )(page_table, qt, kt, vt, q, k_new, v_new, *([cache_kt] * npg), *([cache_vt] * npg))


def _cumsum_rows(x):
    n = x.shape[0]
    row = lax.broadcasted_iota(jnp.int32, x.shape, 0)
    step = 1
    while step < n:
        x = x + jnp.where(row >= step, pltpu.roll(x, step, 0), 0.0)
        step *= 2
    return x


def _ssd_kernel(xbc_ref, dtr_ref, z_ref, conv0_ref, h0_ref, cw_ref, cb_ref, dtb_ref, alog_ref,
                dsk_ref, gn_ref, y_ref, convo_ref, ho_ref, xpad_ref, h_ref, *, cl, valid, nbat):
    c = pl.program_id(1)
    nc = pl.num_programs(1)

    @pl.when(c == 0)
    def _():
        for bb in range(nbat):
            xpad_ref[bb, 0:CONV_PAD, :] = conv0_ref[bb]
            h_ref[bb] = h0_ref[bb]

    for bb in range(nbat):
        _ssd_chunk(bb, xbc_ref, dtr_ref, z_ref, cw_ref, cb_ref, dtb_ref, alog_ref, dsk_ref, gn_ref,
                   y_ref, xpad_ref, h_ref, cl, valid)

    @pl.when(c == nc - 1)
    def _():
        for bb in range(nbat):
            convo_ref[bb] = xpad_ref[bb, CONV_PAD + valid - (CONV_W - 1):CONV_PAD + valid, :]
            ho_ref[bb] = h_ref[bb]


def _ssd_chunk(bb, xbc_ref, dtr_ref, z_ref, cw_ref, cb_ref, dtb_ref, alog_ref, dsk_ref, gn_ref,
               y_ref, xpad_ref, h_ref, cl, valid):
    x = xbc_ref[bb]
    xpad_ref[bb, CONV_PAD:CONV_PAD + cl, :] = x
    w = cw_ref[...]
    conv = cb_ref[...] + x * w[CONV_W - 1:CONV_W, :]
    for back in range(1, CONV_W):
        conv = conv + xpad_ref[bb, pl.ds(CONV_PAD - back, cl), :] * w[CONV_W - 1 - back:CONV_W - back, :]

    xpad_ref[bb, 0:CONV_PAD, :] = xpad_ref[bb, cl:cl + CONV_PAD, :]
    xc = conv * _sigmoid(conv)
    xs = xc[:, :SSM_INNER]
    gw = SSM_STATE
    bm = [xc[:, SSM_INNER + g * gw:SSM_INNER + (g + 1) * gw].astype(BF16) for g in range(SSM_GROUPS)]
    cm = [xc[:, SSM_INNER + (SSM_GROUPS + g) * gw:SSM_INNER + (SSM_GROUPS + g + 1) * gw].astype(BF16)
          for g in range(SSM_GROUPS)]

    pre = dtr_ref[bb] + dtb_ref[...]
    dt = jnp.maximum(pre, 0.0) + jnp.log1p(jnp.exp(-jnp.abs(pre)))
    rowl = lax.broadcasted_iota(jnp.int32, dt.shape, 0)
    if valid < cl:
        dt = jnp.where(rowl < valid, dt, 0.0)
    da = dt * (-jnp.exp(alog_ref[...]))
    cum = _cumsum_rows(da)
    if cl < LANES:
        cum_t = jnp.concatenate([cum, jnp.zeros((LANES - cl, LANES), F32)], axis=0).T[:, 0:cl]
    else:
        cum_t = cum.T
    cum_last = cum[cl - 1:cl, :]
    ecum = jnp.exp(cum)
    tail = jnp.exp(cum_last - cum)
    elast = jnp.exp(cum_last)

    ri = lax.broadcasted_iota(jnp.int32, (cl, cl), 0)
    ci = lax.broadcasted_iota(jnp.int32, (cl, cl), 1)
    causal = ci <= ri
    lane = lax.broadcasted_iota(jnp.int32, (cl, LANES), 1)
    lo = lane < SSM_HEAD_DIM
    row_lo = lax.broadcasted_iota(jnp.int32, (LANES, 1), 0) < SSM_HEAD_DIM
    heads_per_group = SSM_HEADS // SSM_GROUPS
    gmat = [_dot_nt(cm[g], bm[g]) for g in range(SSM_GROUPS)]

    def pick(cols, h0):
        return jnp.where(lo, cols[:, h0:h0 + 1], cols[:, h0 + 1:h0 + 2])

    ys = []
    for p in range(SSM_HEADS // 2):
        h0 = 2 * p
        g = h0 // heads_per_group
        xs_p = xs[:, p * LANES:(p + 1) * LANES]
        xdt = xs_p * pick(dt, h0)
        xdt_b = xdt.astype(BF16)
        parts = []
        for h in (h0, h0 + 1):
            seg = cum[:, h:h + 1] - cum_t[h:h + 1, :]
            dec = jnp.where(causal, jnp.exp(jnp.minimum(seg, 0.0)), 0.0)
            parts.append(_dot((gmat[g] * dec).astype(BF16), xdt_b))
        hp = h_ref[bb, p * LANES:(p + 1) * LANES, :]
        y_p = jnp.where(lo, parts[0], parts[1])
        y_p = y_p + _dot_nt(cm[g], hp.astype(BF16)) * pick(ecum, h0)
        y_p = y_p + dsk_ref[:, p * LANES:(p + 1) * LANES] * xs_p
        xw = (xdt * pick(tail, h0)).astype(BF16)
        decay_rows = jnp.where(row_lo, elast[:, h0:h0 + 1], elast[:, h0 + 1:h0 + 2])
        h_ref[bb, p * LANES:(p + 1) * LANES, :] = hp * decay_rows + _dot_tn(xw, bm[g])
        ys.append(y_p)

    y = jnp.concatenate(ys, axis=1)
    zz = z_ref[bb]
    yz = y * (zz * _sigmoid(zz))
    gwid = SSM_INNER // SSM_GROUPS
    normed = []
    for g in range(SSM_GROUPS):
        part = yz[:, g * gwid:(g + 1) * gwid]
        normed.append(part * lax.rsqrt(jnp.mean(part * part, axis=-1, keepdims=True) + EPS))
    y_ref[bb] = (jnp.concatenate(normed, axis=1) * gn_ref[...]).astype(y_ref.dtype)


def _ssd(xbc, dtr, z, conv0, h0, cw, cb, dtb, alog, dsk, gn, cl, valid, out_dtype):
    b, t, _ = xbc.shape
    nc = t // cl
    nbat = math.gcd(b, SSD_SEQS_PER_STEP)
    rows = lambda w: pl.BlockSpec((nbat, cl, w), lambda bi, c: (bi, c, 0))
    per_b = lambda shape: pl.BlockSpec((nbat,) + shape, lambda bi, c: (bi, 0, 0))
    hshape = (SSM_HEADS * SSM_HEAD_DIM, SSM_STATE)
    return pl.pallas_call(
        functools.partial(_ssd_kernel, cl=cl, valid=valid, nbat=nbat),
        grid=(b // nbat, nc),
        in_specs=[rows(CONV_DIM), rows(LANES), rows(SSM_INNER), per_b((CONV_PAD, CONV_DIM)),
                  per_b(hshape), _const(cw.shape), _const(cb.shape), _const(dtb.shape),
                  _const(alog.shape), _const(dsk.shape), _const(gn.shape)],
        out_specs=[rows(SSM_INNER), per_b((CONV_W - 1, CONV_DIM)), per_b(hshape)],
        out_shape=[jax.ShapeDtypeStruct((b, t, SSM_INNER), out_dtype),
                   jax.ShapeDtypeStruct((b, CONV_W - 1, CONV_DIM), F32),
                   jax.ShapeDtypeStruct((b,) + hshape, F32)],
        scratch_shapes=[pltpu.VMEM((nbat, CONV_PAD + cl, CONV_DIM), F32), pltpu.VMEM((nbat,) + hshape, F32)],
        compiler_params=_params("parallel", "arbitrary"),
        name="ssd",
    )(xbc, dtr, z, conv0, h0, cw, cb, dtb, alog, dsk, gn)


def _head_slice(h):
    return slice(h * X_HEAD_DIM, (h + 1) * X_HEAD_DIM)


def _outproj_body(x, att, y, wa_ref, wy_ref, g_ref, wq_ref, gq_ref):
    x1 = x + _dot(att.astype(BF16), wa_ref[...]) + _dot(y.astype(BF16), wy_ref[...])
    qx = _dot(_rms(x1, g_ref[...]).astype(BF16), wq_ref[...])
    gq = gq_ref[...]
    return x1, [_rms(qx[:, _head_slice(h)], gq) for h in range(X_HEADS)]


def _outproj_kernel(x_ref, att_ref, y_ref, wa_ref, wy_ref, g_ref, wq_ref, gq_ref, x1_ref, qx_ref):
    x1, q_heads = _outproj_body(x_ref[...], att_ref[...], y_ref[...], wa_ref, wy_ref, g_ref, wq_ref, gq_ref)
    x1_ref[...] = x1
    for h in range(X_HEADS):
        qx_ref[:, _head_slice(h)] = q_heads[h].astype(qx_ref.dtype)


def _outproj(x, att, y, w_att, w_y, g, wq, gq, tm, act_dtype):
    n = x.shape[0]
    row = lambda w: pl.BlockSpec((tm, w), lambda i: (i, 0))
    return pl.pallas_call(
        _outproj_kernel,
        grid=(n // tm,),
        in_specs=[row(D_MODEL), row(ATT_WIDTH), row(SSM_INNER), _const(w_att.shape), _const(w_y.shape),
                  _const(g.shape), _const(wq.shape), _const(gq.shape)],
        out_specs=[row(D_MODEL), row(X_WIDTH)],
        out_shape=[jax.ShapeDtypeStruct((n, D_MODEL), F32), jax.ShapeDtypeStruct((n, X_WIDTH), act_dtype)],
        compiler_params=_params("parallel"),
        name="outproj",
    )(x, att, y, w_att, w_y, g, wq, gq)


def _xattn_head(q, k, v):
    s = _dot_nt((q.astype(F32) * (X_HEAD_DIM ** -0.5)).astype(BF16), k.astype(BF16))
    p = jnp.exp(s - jnp.max(s, axis=-1, keepdims=True))
    return _dot(p.astype(BF16), v.astype(BF16)) / jnp.sum(p, axis=-1, keepdims=True)


def _xattn_rows_kernel(q_ref, k_ref, v_ref, o_ref):
    gb, t, _ = q_ref.shape
    flat = k_ref.shape[1]
    rowh = lax.broadcasted_iota(jnp.int32, (X_HEADS * t, flat), 0) // t
    colh = lax.broadcasted_iota(jnp.int32, (X_HEADS * t, flat), 1) % X_HEADS
    hmask = jnp.where(rowh == colh, 0.0, NEG)
    for g in range(gb):
        q = q_ref[g].astype(F32)
        q_rows = jnp.concatenate([q[:, _head_slice(h)] for h in range(X_HEADS)], axis=0)
        s = _dot_nt((q_rows * (X_HEAD_DIM ** -0.5)).astype(BF16), k_ref[g].astype(BF16)) + hmask
        p = jnp.exp(s - jnp.max(s, axis=-1, keepdims=True))
        o_rows = _dot(p.astype(BF16), v_ref[g].astype(BF16)) / jnp.sum(p, axis=-1, keepdims=True)
        for h in range(X_HEADS):
            o_ref[g, :, _head_slice(h)] = o_rows[h * t:(h + 1) * t].astype(o_ref.dtype)


def _xattn_rows(q, mk, mv, act_dtype):
    b, t, _ = q.shape
    flat = mk.shape[1]
    gb = math.gcd(b, XATTN_SEQS_PER_STEP)
    qspec = pl.BlockSpec((gb, t, X_WIDTH), lambda bi: (bi, 0, 0))
    kvspec = pl.BlockSpec((gb, flat, X_HEAD_DIM), lambda bi: (bi, 0, 0))
    return pl.pallas_call(
        _xattn_rows_kernel,
        grid=(b // gb,),
        in_specs=[qspec, kvspec, kvspec],
        out_specs=qspec,
        out_shape=jax.ShapeDtypeStruct((b, t, X_WIDTH), act_dtype),
        compiler_params=_params("parallel"),
        name="xattn",
    )(q, mk, mv)


def _mlp_body(x1, o, wo_ref, g_ref, wu_ref, wd_ref):
    x2 = x1 + _dot(o.astype(BF16), wo_ref[...])
    hm = _rms(x2, g_ref[...]).astype(BF16)
    acc = x2
    for cidx in range(D_FF // FF_CHUNK):
        sl = slice(cidx * FF_CHUNK, (cidx + 1) * FF_CHUNK)
        up = jnp.maximum(_dot(hm, wu_ref[:, sl]), 0.0)
        acc = acc + _dot((up * up).astype(BF16), wd_ref[sl, :])
    return acc


def _mlp_kernel(x1_ref, o_ref, wo_ref, g_ref, wu_ref, wd_ref, out_ref):
    out_ref[...] = _mlp_body(x1_ref[...], o_ref[...], wo_ref, g_ref, wu_ref, wd_ref)


def _mlp(x1, o, wo, g, wu, wd, tm):
    n = x1.shape[0]
    row = lambda w: pl.BlockSpec((tm, w), lambda i: (i, 0))
    return pl.pallas_call(
        _mlp_kernel,
        grid=(n // tm,),
        in_specs=[row(D_MODEL), row(X_WIDTH), _const_single(wo.shape), _const(g.shape),
                  _const_single(wu.shape), _const_single(wd.shape)],
        out_specs=row(D_MODEL),
        out_shape=jax.ShapeDtypeStruct((n, D_MODEL), F32),
        compiler_params=_params("parallel"),
        name="mlp",
    )(x1, o, wo, g, wu, wd)


def _post_kernel(x_ref, att_ref, y_ref, mk_ref, mv_ref, wa_ref, wy_ref, gx_ref, wq_ref, gq_ref,
                 wo_ref, gm_ref, wu_ref, wd_ref, out_ref):
    x1, q_heads = _outproj_body(x_ref[...], att_ref[...], y_ref[...], wa_ref, wy_ref, gx_ref, wq_ref, gq_ref)
    o = jnp.concatenate([_xattn_head(q_heads[h], mk_ref[0, :, _head_slice(h)], mv_ref[0, :, _head_slice(h)])
                         for h in range(X_HEADS)], axis=1)
    out_ref[...] = _mlp_body(x1, o, wo_ref, gm_ref, wu_ref, wd_ref)


def _post(x, att, y, mk, mv, wts, tm, t):
    n = x.shape[0]
    tiles_per_seq = t // tm
    m = mk.shape[1]
    row = lambda w: pl.BlockSpec((tm, w), lambda i: (i, 0))
    mem = pl.BlockSpec((1, m, X_WIDTH), lambda i: (i // tiles_per_seq, 0, 0))
    names = ("w_out_att", "w_out_ssm", "ln_x_g", "wq_x", "qx_norm_g", "wo_x", "ln_mlp_g", "w_up", "w_down")
    big = ("w_up", "w_down")
    weights = [wts[k] for k in names]
    return pl.pallas_call(
        _post_kernel,
        grid=(n // tm,),
        in_specs=[row(D_MODEL), row(ATT_WIDTH), row(SSM_INNER), mem, mem]
                 + [(_const_single if k in big else _const)(wts[k].shape) for k in names],
        out_specs=row(D_MODEL),
        out_shape=jax.ShapeDtypeStruct((n, D_MODEL), F32),
        compiler_params=_params("parallel"),
        name="post",
    )(x, att, y, mk, mv, *weights)


def _memkv_kernel(m_ref, g_ref, wk_ref, wv_ref, gk_ref, mk_ref, mv_ref):
    mn = _rms(m_ref[...], g_ref[...]).astype(BF16)
    mk = _dot(mn, wk_ref[...])
    gk = gk_ref[...]
    for h in range(X_HEADS):
        sl = slice(h * X_HEAD_DIM, (h + 1) * X_HEAD_DIM)
        mk_ref[:, sl] = _rms(mk[:, sl], gk)
    mv_ref[...] = _dot(mn, wv_ref[...])


def _memkv(mem, g, wk, wv, gk, tm):
    n = mem.shape[0]
    row = lambda w: pl.BlockSpec((tm, w), lambda i: (i, 0))
    return pl.pallas_call(
        _memkv_kernel,
        grid=(n // tm,),
        in_specs=[row(D_MODEL), _const(g.shape), _const(wk.shape), _const(wv.shape), _const(gk.shape)],
        out_specs=[row(X_WIDTH), row(X_WIDTH)],
        out_shape=[jax.ShapeDtypeStruct((n, X_WIDTH), F32)] * 2,
        compiler_params=_params("parallel"),
        name="memkv",
    )(mem, g, wk, wv, gk)


def _rope_tables(pos):
    half = HEAD_DIM // 2
    inv_freq = ROPE_THETA ** (-jnp.arange(half, dtype=F32) / half)
    ang = pos.astype(F32)[:, None] * inv_freq[None, :]
    return jnp.cos(ang).T, jnp.sin(ang).T


def _row_tile(n, want):
    tm = min(n, want)
    assert n % tm == 0
    return tm


def _project(x, nb, cos_t, sin_t, wts, tm):
    return _inproj(x.reshape(-1, D_MODEL), nb, wts["ln_mix_g"], wts["w_qkv_t"], wts["w_rest_t"], wts["w_dt_t"],
                   wts["q_norm_g"], wts["k_norm_g"], cos_t, sin_t, tm)


def _finish(x, att, z, xbc, dtr, conv0, h0, mem_k, mem_v, wts, tm, act_dtype):
    b, t, _ = x.shape
    n = b * t
    xf = x.reshape(n, D_MODEL)

    cl = SSD_CHUNK if t >= SSD_CHUNK else -(-t // SSD_MIN_CHUNK) * SSD_MIN_CHUNK
    tp = -(-t // cl) * cl
    pad3 = lambda a: jnp.pad(a.reshape(b, t, -1), ((0, 0), (0, tp - t), (0, 0)))
    y, new_conv, new_h = _ssd(pad3(xbc), pad3(dtr), pad3(z), conv0, h0, wts["conv_w"], wts["conv_b"],
                              wts["dt_bias"], wts["a_log"], wts["d_skip"], wts["ssm_norm_g"],
                              cl, min(t, cl) if tp != t else cl, act_dtype)
    y = y[:, :t].reshape(n, SSM_INNER)

    if t % tm == 0:
        out = _post(xf, att, y, mem_k, mem_v, wts, tm, t)
    else:
        x1, qx = _outproj(xf, att, y, wts["w_out_att"], wts["w_out_ssm"], wts["ln_x_g"],
                          wts["wq_x"], wts["qx_norm_g"], tm, act_dtype)
        o = _xattn_rows(qx.reshape(b, t, X_WIDTH), mem_k, mem_v, act_dtype)
        out = _mlp(x1, o.reshape(n, X_WIDTH), wts["wo_x"], wts["ln_mlp_g"], wts["w_up"], wts["w_down"], tm)
    return out.reshape(b, t, D_MODEL), new_conv, new_h


def kernel(x_prompt, x_sample, mem_prompt, cache_k, cache_v, page_table, state_conv, state_ssm,
           cache_mem_k, cache_mem_v, ln_mix_g, w_in, q_norm_g, k_norm_g, conv_w, conv_b, dt_bias,
           a_log, d_skip, ssm_norm_g, w_out, ln_x_g, ln_mem_g, wq_x, wk_x, wv_x, qx_norm_g,
           kx_norm_g, wo_x, ln_mlp_g, w_up, w_down):
    depth = w_in.shape[0]
    assert depth == 1, "single-layer step"
    bp, tp, _ = x_prompt.shape
    bs, ts, _ = x_sample.shape
    page = cache_k.shape[2]
    past_len = page_table.shape[1] * page
    assert tp % MOBA_BLOCK == 0 and past_len % MOBA_BLOCK == 0 and MOBA_BLOCK % page == 0
    assert ts <= MOBA_BLOCK and ts <= SSD_CHUNK and ts >= CONV_W - 1
    l = 0
    qkv_cols = 3 * ATT_WIDTH
    main_cols = qkv_cols + SSM_INNER + CONV_DIM

    def lanes(vec, width=LANES):
        return jnp.pad(vec.astype(F32), (0, width - vec.shape[0])).reshape(1, width)

    def dim_gain(vec):
        return jnp.broadcast_to(vec.astype(F32)[:, None], (HEAD_DIM, LANES))

    wts = {
        "ln_mix_g": ln_mix_g[l].reshape(1, D_MODEL),
        "w_qkv_t": w_in[l][:, :qkv_cols].T.astype(BF16),
        "w_rest_t": w_in[l][:, qkv_cols:main_cols].T.astype(BF16),
        "w_dt_t": jnp.pad(w_in[l][:, main_cols:].T, ((0, LANES - SSM_HEADS), (0, 0))).astype(BF16),
        "q_norm_g": dim_gain(q_norm_g[l]),
        "k_norm_g": dim_gain(k_norm_g[l]),
        "conv_w": conv_w[l],
        "conv_b": conv_b[l].reshape(1, CONV_DIM),
        "dt_bias": lanes(dt_bias[l]),
        "a_log": lanes(a_log[l]),
        "d_skip": jnp.repeat(d_skip[l].astype(F32), SSM_HEAD_DIM).reshape(1, SSM_INNER),
        "ssm_norm_g": ssm_norm_g[l].reshape(1, SSM_INNER),
        "w_out_att": w_out[l][:ATT_WIDTH].astype(BF16),
        "w_out_ssm": w_out[l][ATT_WIDTH:].astype(BF16),
        "ln_x_g": ln_x_g[l].reshape(1, D_MODEL),
        "wq_x": wq_x[l].astype(BF16),
        "qx_norm_g": qx_norm_g[l].reshape(1, X_HEAD_DIM),
        "wo_x": wo_x[l].astype(BF16),
        "ln_mlp_g": ln_mlp_g[l].reshape(1, D_MODEL),
        "w_up": w_up[l].astype(BF16),
        "w_down": w_down[l].astype(BF16),
    }

    mem_rows = mem_prompt.reshape(-1, D_MODEL)
    mk_p, mv_p = _memkv(mem_rows, ln_mem_g[l].reshape(1, D_MODEL), wk_x[l].astype(BF16),
                        wv_x[l].astype(BF16), kx_norm_g[l].reshape(1, X_HEAD_DIM),
                        _row_tile(mem_rows.shape[0], 512))
    mem_len = mem_prompt.shape[1]
    mk_p = mk_p.reshape(bp, mem_len, X_WIDTH)
    mv_p = mv_p.reshape(bp, mem_len, X_WIDTH)
    cos_p, sin_p = _rope_tables(jnp.arange(tp, dtype=jnp.int32))
    conv0_p = jnp.zeros((bp, CONV_PAD, CONV_DIM), F32)
    h0_p = jnp.zeros((bp, SSM_HEADS * SSM_HEAD_DIM, SSM_STATE), F32)

    tm_p = _row_tile(tp, 512)
    qt_p, kt_p, vt_p, z_p, xbc_p, dtr_p = _project(x_prompt, bp, cos_p, sin_p, wts, tm_p)
    heads_t = lambda a: a.reshape(1, bp, ATT_HEADS, HEAD_DIM, tp).transpose(0, 1, 4, 2, 3)
    k_p, v_p = heads_t(kt_p), heads_t(vt_p)

    cos_s, sin_s = _rope_tables(past_len + jnp.arange(ts, dtype=jnp.int32))
    cos_s = jnp.tile(cos_s, (1, bs))
    sin_s = jnp.tile(sin_s, (1, bs))
    conv0_s = jnp.pad(state_conv[l], ((0, 0), (CONV_PAD - (CONV_W - 1), 0), (0, 0)))
    h0_s = state_ssm[l].reshape(bs, SSM_HEADS * SSM_HEAD_DIM, SSM_STATE)
    tm_s = bs * ts
    qt_s, kt_s, vt_s, z_s, xbc_s, dtr_s = _project(x_sample, 1, cos_s, sin_s, wts, tm_s)
    rows = lambda a: a[0].T.reshape(bs, ts, ATT_WIDTH)
    q_s, k_new, v_new = rows(qt_s), rows(kt_s), rows(vt_s)
    k_s = k_new.reshape(1, bs, ts, ATT_HEADS, HEAD_DIM)
    v_s = v_new.reshape(1, bs, ts, ATT_HEADS, HEAD_DIM)

    cache_kt = cache_k.transpose(0, 1, 3, 4, 2)
    cache_vt = cache_v.transpose(0, 1, 3, 4, 2)
    if _moba_both_fits(qt_p, q_s, page_table, page):
        att_p, att_s = _moba_both(page_table, qt_p, kt_p, vt_p, q_s, k_new, v_new, cache_kt, cache_vt)
    else:
        att_p = _moba_prompt(qt_p, kt_p, vt_p)
        att_s = _sample_moba(page_table, q_s, k_new, v_new, cache_kt, cache_vt)

    y_p, conv_p, ssm_p = _finish(x_prompt, att_p.reshape(bp * tp, ATT_WIDTH), z_p, xbc_p, dtr_p, conv0_p, h0_p,
                                 mk_p, mv_p, wts, tm_p, BF16)
    y_s, conv_s, ssm_s = _finish(x_sample, att_s.reshape(bs * ts, ATT_WIDTH), z_s, xbc_s, dtr_s, conv0_s, h0_s,
                                 cache_mem_k[l].reshape(bs, -1, X_HEAD_DIM),
                                 cache_mem_v[l].reshape(bs, -1, X_HEAD_DIM), wts, tm_s, F32)

    state = lambda a, b: a.reshape(1, b, SSM_HEADS, SSM_HEAD_DIM, SSM_STATE)
    return (y_p, y_s, k_p, v_p, conv_p[None], state(ssm_p, bp),
            mk_p.reshape(1, bp, mem_len, X_HEADS, X_HEAD_DIM), mv_p.reshape(1, bp, mem_len, X_HEADS, X_HEAD_DIM),
            k_s, v_s, conv_s[None], state(ssm_s, bs))
```

```python
import functools
import math

import jax
import jax.numpy as jnp
from jax import lax
from jax.experimental import pallas as pl
from jax.experimental.pallas import tpu as pltpu

F32 = jnp.float32
BF16 = jnp.bfloat16

D_MODEL = 1024
ATT_HEADS = 8
HEAD_DIM = 64
ATT_WIDTH = ATT_HEADS * HEAD_DIM
MOBA_BLOCK = 256
TOPK = 3
ROPE_THETA = 10000.0
SSM_HEADS = 8
SSM_HEAD_DIM = 64
SSM_INNER = SSM_HEADS * SSM_HEAD_DIM
SSM_GROUPS = 2
SSM_STATE = 128
CONV_W = 4
CONV_DIM = SSM_INNER + 2 * SSM_GROUPS * SSM_STATE
SSD_CHUNK = 128
X_HEADS = 4
X_HEAD_DIM = 128
X_WIDTH = X_HEADS * X_HEAD_DIM
D_FF = 4 * D_MODEL
EPS = 1e-6

LANES = 128
CONV_PAD = 8
VMEM_LIMIT = 56 * 1024 * 1024
NEG = -1e30
LOG2E = 1.4426950408889634
MOBA_GROUP = 4
ONES_ROWS = 16
FF_CHUNK = 1024
XATTN_SEQS_PER_STEP = 8
SSD_MIN_CHUNK = 16
SSD_SEQS_PER_STEP = 8
PAGE_BUFFERS = 3
PAGES_PER_STEP = 16


def _dot(a, b):
    return jnp.dot(a, b, preferred_element_type=F32)


def _dot_nt(a, b):
    return lax.dot_general(a, b, (((1,), (1,)), ((), ())), preferred_element_type=F32)


def _dot_tn(a, b):
    return lax.dot_general(a, b, (((0,), (0,)), ((), ())), preferred_element_type=F32)


def _split3(x):
    hi = x.astype(BF16)
    r = x - hi.astype(F32)
    mid = r.astype(BF16)
    lo = (r - mid.astype(F32)).astype(BF16)
    return hi, mid, lo


def _dot_f32(a, b, dot=None):
    dot = dot or _dot
    a0, a1, a2 = _split3(a)
    b0, b1, b2 = _split3(b)
    return (dot(a0, b0) + (dot(a0, b1) + dot(a1, b0)) + (dot(a1, b1) + dot(a0, b2) + dot(a2, b0)))


def _rms(x, g):
    ms = jnp.mean(x * x, axis=-1, keepdims=True)
    return x * lax.rsqrt(ms + EPS) * g


def _sigmoid(x):
    return 1.0 / (1.0 + jnp.exp(-x))


def _params(*sem):
    return pltpu.CompilerParams(dimension_semantics=sem, vmem_limit_bytes=VMEM_LIMIT)


def _const(shape):
    return pl.BlockSpec(shape, lambda *_: (0,) * len(shape))


def _const_single(shape):
    return pl.BlockSpec(shape, lambda *_: (0,) * len(shape), pipeline_mode=pl.Buffered(1))


def _inproj_kernel(x_ref, g_ref, wqkv_ref, wrest_ref, wdt_ref, gq_ref, gk_ref, cos_ref, sin_ref,
                   q_ref, k_ref, v_ref, z_ref, xbc_ref, dt_ref):
    hn = _rms(x_ref[...], g_ref[...]).astype(BF16)
    tm = hn.shape[0]
    cos = cos_ref[...]
    sin = sin_ref[...]
    half = HEAD_DIM // 2

    def proj_t(idx):
        return _dot_nt(wqkv_ref[idx * ATT_WIDTH:(idx + 1) * ATT_WIDTH, :], hn)

    def head_norm_rot(p, g_ref_):
        p3 = p.reshape(ATT_HEADS, HEAD_DIM, tm)
        gain = g_ref_[:, 0:1]
        pn = p3 * lax.rsqrt(jnp.mean(p3 * p3, axis=1, keepdims=True) + EPS) * gain
        x1 = pn[:, :half]
        x2 = pn[:, half:]
        rot = jnp.concatenate([x1 * cos - x2 * sin, x2 * cos + x1 * sin], axis=1)
        return rot.reshape(ATT_WIDTH, tm)

    q_ref[0] = head_norm_rot(proj_t(0), gq_ref)
    k_ref[0] = head_norm_rot(proj_t(1), gk_ref)
    v_ref[0] = proj_t(2)
    z_ref[...] = _dot_nt(hn, wrest_ref[0:SSM_INNER, :])
    xbc_ref[...] = _dot_nt(hn, wrest_ref[SSM_INNER:SSM_INNER + CONV_DIM, :])
    dt_ref[...] = _dot_nt(hn, wdt_ref[...])


def _inproj(x, nb, g, w_qkv_t, w_rest, w_dt, gq, gk, cos_t, sin_t, tm):
    n = x.shape[0]
    cols = n // nb
    tpb = cols // tm
    row = lambda w: pl.BlockSpec((tm, w), lambda i: (i, 0))
    tab = pl.BlockSpec((HEAD_DIM // 2, tm), lambda i: (0, i % tpb))
    tspec = pl.BlockSpec((1, ATT_WIDTH, tm), lambda i: (i // tpb, 0, i % tpb))
    outs = ([jax.ShapeDtypeStruct((nb, ATT_WIDTH, cols), F32)] * 3
            + [jax.ShapeDtypeStruct((n, w), F32) for w in (SSM_INNER, CONV_DIM, LANES)])
    return pl.pallas_call(
        _inproj_kernel,
        grid=(n // tm,),
        in_specs=[row(D_MODEL), _const(g.shape), _const_single(w_qkv_t.shape), _const_single(w_rest.shape),
                  _const(w_dt.shape), _const(gq.shape), _const(gk.shape), tab, tab],
        out_specs=[tspec, tspec, tspec, row(SSM_INNER), row(CONV_DIM), row(LANES)],
        out_shape=outs,
        compiler_params=_params("parallel"),
        name="inproj",
    )(x, g, w_qkv_t, w_rest, w_dt, gq, gk, cos_t, sin_t)


def _topk_keep(gate, n_valid, axis):
    idx = lax.broadcasted_iota(jnp.int32, gate.shape, axis)
    cnt = jnp.zeros(gate.shape, jnp.int32)
    for jp in range(gate.shape[axis]):
        gj = lax.slice_in_dim(gate, jp, jp + 1, axis=axis)
        beats = (gj > gate) | ((gj == gate) & (jp < idx))
        cnt = cnt + jnp.where(beats, jnp.where(jp < n_valid, 1, 0), 0)
    return (idx < n_valid) & (cnt < TOPK)


def _moba_kernel(qt_ref, kt_ref, vt_ref, o_ref, kb_ref, vb_ref, km_ref, *, nblk):
    bq = MOBA_BLOCK
    km_ref[...] = jnp.zeros(km_ref.shape, F32)
    for j in range(nblk):
        kj = kt_ref[0, :, j * bq:(j + 1) * bq].T
        kb_ref[j] = kj.astype(BF16)
        km_ref[j:j + 1, :] = jnp.mean(kj, axis=0, keepdims=True)
        vj = vt_ref[0, :, j * bq:(j + 1) * bq].astype(BF16)
        for h in range(2):
            vb_ref[j, h, 0:HEAD_DIM, :] = vj[h * HEAD_DIM:(h + 1) * HEAD_DIM, :]
            vb_ref[j, h, HEAD_DIM:, :] = jnp.ones((ONES_ROWS, bq), BF16)
    kmean = km_ref[...]

    keyi = lax.broadcasted_iota(jnp.int32, (bq, bq), 0)
    qryi = lax.broadcasted_iota(jnp.int32, (bq, bq), 1)
    causal = jnp.where(keyi <= qryi, 0.0, NEG)
    row_lo = lax.broadcasted_iota(jnp.int32, (LANES, bq), 0) < HEAD_DIM
    qscale = (HEAD_DIM ** -0.5) * LOG2E

    order = [blk for pair in zip(range(nblk), reversed(range(nblk))) for blk in pair][:nblk]
    for g0 in range(0, nblk, MOBA_GROUP):
        owns = order[g0:g0 + MOBA_GROUP]
        work = []
        for own in owns:
            qt = qt_ref[0, :, own * bq:(own + 1) * bq]
            for h in range(2):
                qh = jnp.where(row_lo, 0.0, qt) if h else jnp.where(row_lo, qt, 0.0)
                qsb = (qh * qscale).astype(BF16)
                bias = None
                if own > TOPK:
                    keep = _topk_keep(_dot_f32(kmean, qh), own, 0)
                    bias = jnp.where(keep, 0.0, NEG)
                tiles = []
                m = None
                for j in range(own + 1):
                    s = _dot(kb_ref[j], qsb)
                    if j == own:
                        s = s + causal
                    elif bias is not None:
                        s = s + bias[j:j + 1, :]
                    tiles.append(s)
                    mj = jnp.max(s, axis=0, keepdims=True)
                    m = mj if m is None else jnp.maximum(m, mj)
                work.append((own, h, tiles, m))
        outs = {}
        for own, h, tiles, m in work:
            acc = jnp.zeros((HEAD_DIM + ONES_ROWS, bq), F32)
            for j in range(own + 1):
                p = jnp.exp2(tiles[j] - m)
                acc = acc + _dot(vb_ref[j, h], p.astype(BF16))
            outs[(own, h)] = acc[0:HEAD_DIM] / acc[HEAD_DIM:HEAD_DIM + 1]
        for own in owns:
            o_t = jnp.concatenate([outs[(own, 0)], outs[(own, 1)]], axis=0)
            o_ref[0, own * bq:(own + 1) * bq, :] = o_t.T.astype(o_ref.dtype)


def _moba_prompt(qt, kt, vt):
    b, _, t = qt.shape
    nblk = t // MOBA_BLOCK
    npad = max(16, nblk)
    npair = ATT_WIDTH // LANES
    spec = pl.BlockSpec((1, LANES, t), lambda bi, p: (bi, p, 0))
    return pl.pallas_call(
        functools.partial(_moba_kernel, nblk=nblk),
        grid=(b, npair),
        in_specs=[spec, spec, spec],
        out_specs=pl.BlockSpec((1, t, LANES), lambda bi, p: (bi, 0, p)),
        out_shape=jax.ShapeDtypeStruct((b, t, ATT_WIDTH), BF16),
        scratch_shapes=[pltpu.VMEM((nblk, MOBA_BLOCK, LANES), BF16),
                        pltpu.VMEM((nblk, 2, HEAD_DIM + ONES_ROWS, MOBA_BLOCK), BF16),
                        pltpu.VMEM((npad, LANES), F32)],
        compiler_params=_params("parallel", "parallel"),
        name="moba_prompt",
    )(qt, kt, vt)


def _expand_heads(q):
    t = q.shape[0]
    tiled = jnp.concatenate([q] * ATT_HEADS, axis=0)
    rowh = lax.broadcasted_iota(jnp.int32, tiled.shape, 0) // t
    laneh = lax.broadcasted_iota(jnp.int32, tiled.shape, 1) // HEAD_DIM
    return jnp.where(rowh == laneh, tiled, 0.0)


def _sample_moba_kernel(pt_ref, q_ref, kn_ref, vn_ref, ck_ref, cv_ref, o_ref, km_ref, m_ref, l_ref, oblk_ref,
                        kbuf_ref, vbuf_ref, sem_ref, *, page, npg, nblk):
    c = pl.program_id(1)
    nsteps = pl.num_programs(1)
    step = pl.program_id(0) * nsteps + c
    total = pl.num_programs(0) * nsteps

    def page_copies(st):
        slot = st % PAGE_BUFFERS
        seq = st // nsteps
        first = (st % nsteps) * npg
        out = []
        for pi in range(npg):
            pg = pt_ref[seq, first + pi]
            out.append(pltpu.make_async_copy(ck_ref.at[0, pg], kbuf_ref.at[slot, pi], sem_ref.at[0, slot]))
            out.append(pltpu.make_async_copy(cv_ref.at[0, pg], vbuf_ref.at[slot, pi], sem_ref.at[1, slot]))
        return out

    @pl.when(step == 0)
    def _():
        for ahead in range(PAGE_BUFFERS - 1):
            for cp in page_copies(ahead):
                cp.start()

    @pl.when(step + (PAGE_BUFFERS - 1) < total)
    def _():
        for cp in page_copies(step + (PAGE_BUFFERS - 1)):
            cp.start()

    for cp in page_copies(step):
        cp.wait()
    slot = step % PAGE_BUFFERS
    k_refs = [kbuf_ref.at[slot, pi] for pi in range(npg)]
    v_refs = [vbuf_ref.at[slot, pi] for pi in range(npg)]
    t = q_ref.shape[1]
    rows = ATT_HEADS * t
    pages_per_block = MOBA_BLOCK // page
    blocks_per_step = npg // pages_per_block

    qe = _expand_heads(q_ref[0])
    qb = (qe * ((HEAD_DIM ** -0.5) * LOG2E)).astype(BF16)
    blk_lane = lax.broadcasted_iota(jnp.int32, (ATT_WIDTH, LANES), 1)

    @pl.when(c == 0)
    def _():
        km_ref[...] = jnp.zeros(km_ref.shape, F32)

    tiles = []
    for jj in range(blocks_per_step):
        ksum = None
        for u in range(pages_per_block):
            kt = k_refs[jj * pages_per_block + u][...].reshape(ATT_WIDTH, page)
            ksum = kt if ksum is None else ksum + kt
            tiles.append(_dot(qb, kt.astype(BF16)))
        kmean = jnp.sum(ksum, axis=1, keepdims=True) * (1.0 / MOBA_BLOCK)
        km_ref[...] = jnp.where(blk_lane == c * blocks_per_step + jj, kmean, km_ref[...])
    probs = []
    for jj in range(blocks_per_step):
        mine = tiles[jj * pages_per_block:(jj + 1) * pages_per_block]
        m = jnp.max(mine[0], axis=1, keepdims=True)
        for s in mine[1:]:
            m = jnp.maximum(m, jnp.max(s, axis=1, keepdims=True))
        l = jnp.zeros((rows, 1), F32)
        for s in mine:
            p = jnp.exp2(s - m)
            l = l + jnp.sum(p, axis=1, keepdims=True)
            probs.append(p.astype(BF16))
        m_ref[c * blocks_per_step + jj] = jnp.broadcast_to(m, (rows, LANES))
        l_ref[c * blocks_per_step + jj] = jnp.broadcast_to(l, (rows, LANES))
    for jj in range(blocks_per_step):
        o = jnp.zeros((rows, ATT_WIDTH), F32)
        for u in range(pages_per_block):
            vt = v_refs[jj * pages_per_block + u][...].reshape(ATT_WIDTH, page).astype(BF16)
            o = o + _dot_nt(probs[jj * pages_per_block + u], vt)
        oblk_ref[c * blocks_per_step + jj] = o

    @pl.when(c == pl.num_programs(1) - 1)
    def _():
        keep = _topk_keep(_dot_f32(qe, km_ref[:, 0:nblk]), nblk, 1)
        s_own = _dot_nt(qb, kn_ref[0].astype(BF16))
        qpos = lax.broadcasted_iota(jnp.int32, s_own.shape, 0) % t
        kpos = lax.broadcasted_iota(jnp.int32, s_own.shape, 1)
        s_own = jnp.where(kpos <= qpos, s_own, NEG)
        lane_j = lax.broadcasted_iota(jnp.int32, (rows, nblk), 1)
        m_all = jnp.zeros((rows, nblk), F32)
        l_all = jnp.zeros((rows, nblk), F32)
        for j in range(nblk):
            m_all = jnp.where(lane_j == j, m_ref[j][:, 0:nblk], m_all)
            l_all = jnp.where(lane_j == j, l_ref[j][:, 0:nblk], l_all)
        mx = jnp.maximum(jnp.max(s_own, axis=1, keepdims=True),
                         jnp.max(jnp.where(keep, m_all, NEG), axis=1, keepdims=True))
        p_own = jnp.exp2(s_own - mx)
        w = jnp.where(keep, jnp.exp2(jnp.minimum(m_all - mx, 0.0)), 0.0)
        den = jnp.sum(p_own, axis=1, keepdims=True) + jnp.sum(w * l_all, axis=1, keepdims=True)
        vn = vn_ref[0]
        num = jnp.zeros((rows, ATT_WIDTH), F32)
        for tt in range(t):
            num = num + p_own[:, tt:tt + 1] * vn[tt:tt + 1, :]
        for j in range(nblk):
            num = num + w[:, j:j + 1] * oblk_ref[j]
        full = num / den
        laneh = lax.broadcasted_iota(jnp.int32, (t, ATT_WIDTH), 1) // HEAD_DIM
        out = jnp.zeros((t, ATT_WIDTH), F32)
        for h in range(ATT_HEADS):
            out = out + jnp.where(laneh == h, full[h * t:(h + 1) * t, :], 0.0)
        o_ref[0] = out


def _sample_moba(page_table, q, k_new, v_new, cache_kt, cache_vt):
    b, t, _ = q.shape
    page = cache_kt.shape[4]
    n_pages = page_table.shape[1]
    npg = math.gcd(n_pages, PAGES_PER_STEP)
    nsteps = n_pages // npg
    nblk = n_pages * page // MOBA_BLOCK
    assert nblk <= LANES and npg % (MOBA_BLOCK // page) == 0 and b * nsteps >= PAGE_BUFFERS
    rows = ATT_HEADS * t
    per_b = pl.BlockSpec((1, t, ATT_WIDTH), lambda bi, c, pt: (bi, 0, 0))
    hbm = pl.BlockSpec(memory_space=pl.ANY)
    page_buf = pltpu.VMEM((PAGE_BUFFERS, npg, ATT_HEADS, HEAD_DIM, page), F32)

    grid_spec = pltpu.PrefetchScalarGridSpec(
        num_scalar_prefetch=1,
        grid=(b, nsteps),
        in_specs=[per_b, per_b, per_b, hbm, hbm],
        out_specs=per_b,
        scratch_shapes=[pltpu.VMEM((ATT_WIDTH, LANES), F32),
                        pltpu.VMEM((nblk, rows, LANES), F32),
                        pltpu.VMEM((nblk, rows, LANES), F32),
                        pltpu.VMEM((nblk, rows, ATT_WIDTH), F32),
                        page_buf, page_buf,
                        pltpu.SemaphoreType.DMA((2, PAGE_BUFFERS))],
    )
    return pl.pallas_call(
        functools.partial(_sample_moba_kernel, page=page, npg=npg, nblk=nblk),
        grid_spec=grid_spec,
        out_shape=jax.ShapeDtypeStruct((b, t, ATT_WIDTH), F32),
        compiler_params=_params("arbitrary", "arbitrary"),
        name="sample_moba",
    )(page_table, q, k_new, v_new, cache_kt, cache_vt)


def _cumsum_rows(x):
    n = x.shape[0]
    row = lax.broadcasted_iota(jnp.int32, x.shape, 0)
    step = 1
    while step < n:
        x = x + jnp.where(row >= step, pltpu.roll(x, step, 0), 0.0)
        step *= 2
    return x


def _ssd_kernel(xbc_ref, dtr_ref, z_ref, conv0_ref, h0_ref, cw_ref, cb_ref, dtb_ref, alog_ref,
                dsk_ref, gn_ref, y_ref, convo_ref, ho_ref, xpad_ref, h_ref, *, cl, valid, nbat):
    c = pl.program_id(1)
    nc = pl.num_programs(1)

    @pl.when(c == 0)
    def _():
        for bb in range(nbat):
            xpad_ref[bb, 0:CONV_PAD, :] = conv0_ref[bb]
            h_ref[bb] = h0_ref[bb]

    for bb in range(nbat):
        _ssd_chunk(bb, xbc_ref, dtr_ref, z_ref, cw_ref, cb_ref, dtb_ref, alog_ref, dsk_ref, gn_ref,
                   y_ref, xpad_ref, h_ref, cl, valid)

    @pl.when(c == nc - 1)
    def _():
        for bb in range(nbat):
            convo_ref[bb] = xpad_ref[bb, CONV_PAD + valid - (CONV_W - 1):CONV_PAD + valid, :]
            ho_ref[bb] = h_ref[bb]


def _ssd_chunk(bb, xbc_ref, dtr_ref, z_ref, cw_ref, cb_ref, dtb_ref, alog_ref, dsk_ref, gn_ref,
               y_ref, xpad_ref, h_ref, cl, valid):
    x = xbc_ref[bb]
    xpad_ref[bb, CONV_PAD:CONV_PAD + cl, :] = x
    w = cw_ref[...]
    conv = cb_ref[...] + x * w[CONV_W - 1:CONV_W, :]
    for back in range(1, CONV_W):
        conv = conv + xpad_ref[bb, pl.ds(CONV_PAD - back, cl), :] * w[CONV_W - 1 - back:CONV_W - back, :]

    xpad_ref[bb, 0:CONV_PAD, :] = xpad_ref[bb, cl:cl + CONV_PAD, :]
    xc = conv * _sigmoid(conv)
    xs = xc[:, :SSM_INNER]
    gw = SSM_STATE
    bm = [xc[:, SSM_INNER + g * gw:SSM_INNER + (g + 1) * gw].astype(BF16) for g in range(SSM_GROUPS)]
    cm = [xc[:, SSM_INNER + (SSM_GROUPS + g) * gw:SSM_INNER + (SSM_GROUPS + g + 1) * gw].astype(BF16)
          for g in range(SSM_GROUPS)]

    pre = dtr_ref[bb] + dtb_ref[...]
    dt = jnp.maximum(pre, 0.0) + jnp.log1p(jnp.exp(-jnp.abs(pre)))
    rowl = lax.broadcasted_iota(jnp.int32, dt.shape, 0)
    if valid < cl:
        dt = jnp.where(rowl < valid, dt, 0.0)
    da = dt * (-jnp.exp(alog_ref[...]))
    cum = _cumsum_rows(da)
    if cl < LANES:
        cum_t = jnp.concatenate([cum, jnp.zeros((LANES - cl, LANES), F32)], axis=0).T[:, 0:cl]
    else:
        cum_t = cum.T
    cum_last = cum[cl - 1:cl, :]
    ecum = jnp.exp(cum)
    tail = jnp.exp(cum_last - cum)
    elast = jnp.exp(cum_last)

    ri = lax.broadcasted_iota(jnp.int32, (cl, cl), 0)
    ci = lax.broadcasted_iota(jnp.int32, (cl, cl), 1)
    causal = ci <= ri
    lane = lax.broadcasted_iota(jnp.int32, (cl, LANES), 1)
    lo = lane < SSM_HEAD_DIM
    row_lo = lax.broadcasted_iota(jnp.int32, (LANES, 1), 0) < SSM_HEAD_DIM
    heads_per_group = SSM_HEADS // SSM_GROUPS
    gmat = [_dot_nt(cm[g], bm[g]) for g in range(SSM_GROUPS)]

    def pick(cols, h0):
        return jnp.where(lo, cols[:, h0:h0 + 1], cols[:, h0 + 1:h0 + 2])

    ys = []
    for p in range(SSM_HEADS // 2):
        h0 = 2 * p
        g = h0 // heads_per_group
        xs_p = xs[:, p * LANES:(p + 1) * LANES]
        xdt = xs_p * pick(dt, h0)
        xdt_b = xdt.astype(BF16)
        parts = []
        for h in (h0, h0 + 1):
            seg = cum[:, h:h + 1] - cum_t[h:h + 1, :]
            dec = jnp.where(causal, jnp.exp(jnp.minimum(seg, 0.0)), 0.0)
            parts.append(_dot((gmat[g] * dec).astype(BF16), xdt_b))
        hp = h_ref[bb, p * LANES:(p + 1) * LANES, :]
        y_p = jnp.where(lo, parts[0], parts[1])
        y_p = y_p + _dot_nt(cm[g], hp.astype(BF16)) * pick(ecum, h0)
        y_p = y_p + dsk_ref[:, p * LANES:(p + 1) * LANES] * xs_p
        xw = (xdt * pick(tail, h0)).astype(BF16)
        decay_rows = jnp.where(row_lo, elast[:, h0:h0 + 1], elast[:, h0 + 1:h0 + 2])
        h_ref[bb, p * LANES:(p + 1) * LANES, :] = hp * decay_rows + _dot_tn(xw, bm[g])
        ys.append(y_p)

    y = jnp.concatenate(ys, axis=1)
    zz = z_ref[bb]
    yz = y * (zz * _sigmoid(zz))
    gwid = SSM_INNER // SSM_GROUPS
    normed = []
    for g in range(SSM_GROUPS):
        part = yz[:, g * gwid:(g + 1) * gwid]
        normed.append(part * lax.rsqrt(jnp.mean(part * part, axis=-1, keepdims=True) + EPS))
    y_ref[bb] = (jnp.concatenate(normed, axis=1) * gn_ref[...]).astype(y_ref.dtype)


def _ssd(xbc, dtr, z, conv0, h0, cw, cb, dtb, alog, dsk, gn, cl, valid, out_dtype):
    b, t, _ = xbc.shape
    nc = t // cl
    nbat = math.gcd(b, SSD_SEQS_PER_STEP)
    rows = lambda w: pl.BlockSpec((nbat, cl, w), lambda bi, c: (bi, c, 0))
    per_b = lambda shape: pl.BlockSpec((nbat,) + shape, lambda bi, c: (bi, 0, 0))
    hshape = (SSM_HEADS * SSM_HEAD_DIM, SSM_STATE)
    return pl.pallas_call(
        functools.partial(_ssd_kernel, cl=cl, valid=valid, nbat=nbat),
        grid=(b // nbat, nc),
        in_specs=[rows(CONV_DIM), rows(LANES), rows(SSM_INNER), per_b((CONV_PAD, CONV_DIM)),
                  per_b(hshape), _const(cw.shape), _const(cb.shape), _const(dtb.shape),
                  _const(alog.shape), _const(dsk.shape), _const(gn.shape)],
        out_specs=[rows(SSM_INNER), per_b((CONV_W - 1, CONV_DIM)), per_b(hshape)],
        out_shape=[jax.ShapeDtypeStruct((b, t, SSM_INNER), out_dtype),
                   jax.ShapeDtypeStruct((b, CONV_W - 1, CONV_DIM), F32),
                   jax.ShapeDtypeStruct((b,) + hshape, F32)],
        scratch_shapes=[pltpu.VMEM((nbat, CONV_PAD + cl, CONV_DIM), F32), pltpu.VMEM((nbat,) + hshape, F32)],
        compiler_params=_params("parallel", "arbitrary"),
        name="ssd",
    )(xbc, dtr, z, conv0, h0, cw, cb, dtb, alog, dsk, gn)


def _head_slice(h):
    return slice(h * X_HEAD_DIM, (h + 1) * X_HEAD_DIM)


def _outproj_body(x, att, y, wa_ref, wy_ref, g_ref, wq_ref, gq_ref):
    x1 = x + _dot(att.astype(BF16), wa_ref[...]) + _dot(y.astype(BF16), wy_ref[...])
    qx = _dot(_rms(x1, g_ref[...]).astype(BF16), wq_ref[...])
    gq = gq_ref[...]
    return x1, [_rms(qx[:, _head_slice(h)], gq) for h in range(X_HEADS)]


def _outproj_kernel(x_ref, att_ref, y_ref, wa_ref, wy_ref, g_ref, wq_ref, gq_ref, x1_ref, qx_ref):
    x1, q_heads = _outproj_body(x_ref[...], att_ref[...], y_ref[...], wa_ref, wy_ref, g_ref, wq_ref, gq_ref)
    x1_ref[...] = x1
    for h in range(X_HEADS):
        qx_ref[:, _head_slice(h)] = q_heads[h].astype(qx_ref.dtype)


def _outproj(x, att, y, w_att, w_y, g, wq, gq, tm, act_dtype):
    n = x.shape[0]
    row = lambda w: pl.BlockSpec((tm, w), lambda i: (i, 0))
    return pl.pallas_call(
        _outproj_kernel,
        grid=(n // tm,),
        in_specs=[row(D_MODEL), row(ATT_WIDTH), row(SSM_INNER), _const(w_att.shape), _const(w_y.shape),
                  _const(g.shape), _const(wq.shape), _const(gq.shape)],
        out_specs=[row(D_MODEL), row(X_WIDTH)],
        out_shape=[jax.ShapeDtypeStruct((n, D_MODEL), F32), jax.ShapeDtypeStruct((n, X_WIDTH), act_dtype)],
        compiler_params=_params("parallel"),
        name="outproj",
    )(x, att, y, w_att, w_y, g, wq, gq)


def _xattn_head(q, k, v):
    s = _dot_nt((q.astype(F32) * (X_HEAD_DIM ** -0.5)).astype(BF16), k.astype(BF16))
    p = jnp.exp(s - jnp.max(s, axis=-1, keepdims=True))
    return _dot(p.astype(BF16), v.astype(BF16)) / jnp.sum(p, axis=-1, keepdims=True)


def _xattn_rows_kernel(q_ref, k_ref, v_ref, o_ref):
    gb, t, _ = q_ref.shape
    flat = k_ref.shape[1]
    rowh = lax.broadcasted_iota(jnp.int32, (X_HEADS * t, flat), 0) // t
    colh = lax.broadcasted_iota(jnp.int32, (X_HEADS * t, flat), 1) % X_HEADS
    hmask = jnp.where(rowh == colh, 0.0, NEG)
    for g in range(gb):
        q = q_ref[g].astype(F32)
        q_rows = jnp.concatenate([q[:, _head_slice(h)] for h in range(X_HEADS)], axis=0)
        s = _dot_nt((q_rows * (X_HEAD_DIM ** -0.5)).astype(BF16), k_ref[g].astype(BF16)) + hmask
        p = jnp.exp(s - jnp.max(s, axis=-1, keepdims=True))
        o_rows = _dot(p.astype(BF16), v_ref[g].astype(BF16)) / jnp.sum(p, axis=-1, keepdims=True)
        for h in range(X_HEADS):
            o_ref[g, :, _head_slice(h)] = o_rows[h * t:(h + 1) * t].astype(o_ref.dtype)


def _xattn_rows(q, mk, mv, act_dtype):
    b, t, _ = q.shape
    flat = mk.shape[1]
    gb = math.gcd(b, XATTN_SEQS_PER_STEP)
    qspec = pl.BlockSpec((gb, t, X_WIDTH), lambda bi: (bi, 0, 0))
    kvspec = pl.BlockSpec((gb, flat, X_HEAD_DIM), lambda bi: (bi, 0, 0))
    return pl.pallas_call(
        _xattn_rows_kernel,
        grid=(b // gb,),
        in_specs=[qspec, kvspec, kvspec],
        out_specs=qspec,
        out_shape=jax.ShapeDtypeStruct((b, t, X_WIDTH), act_dtype),
        compiler_params=_params("parallel"),
        name="xattn",
    )(q, mk, mv)


def _mlp_body(x1, o, wo_ref, g_ref, wu_ref, wd_ref):
    x2 = x1 + _dot(o.astype(BF16), wo_ref[...])
    hm = _rms(x2, g_ref[...]).astype(BF16)
    acc = x2
    for cidx in range(D_FF // FF_CHUNK):
        sl = slice(cidx * FF_CHUNK, (cidx + 1) * FF_CHUNK)
        up = jnp.maximum(_dot(hm, wu_ref[:, sl]), 0.0)
        acc = acc + _dot((up * up).astype(BF16), wd_ref[sl, :])
    return acc


def _mlp_kernel(x1_ref, o_ref, wo_ref, g_ref, wu_ref, wd_ref, out_ref):
    out_ref[...] = _mlp_body(x1_ref[...], o_ref[...], wo_ref, g_ref, wu_ref, wd_ref)


def _mlp(x1, o, wo, g, wu, wd, tm):
    n = x1.shape[0]
    row = lambda w: pl.BlockSpec((tm, w), lambda i: (i, 0))
    return pl.pallas_call(
        _mlp_kernel,
        grid=(n // tm,),
        in_specs=[row(D_MODEL), row(X_WIDTH), _const_single(wo.shape), _const(g.shape),
                  _const_single(wu.shape), _const_single(wd.shape)],
        out_specs=row(D_MODEL),
        out_shape=jax.ShapeDtypeStruct((n, D_MODEL), F32),
        compiler_params=_params("parallel"),
        name="mlp",
    )(x1, o, wo, g, wu, wd)


def _post_kernel(x_ref, att_ref, y_ref, mk_ref, mv_ref, wa_ref, wy_ref, gx_ref, wq_ref, gq_ref,
                 wo_ref, gm_ref, wu_ref, wd_ref, out_ref):
    x1, q_heads = _outproj_body(x_ref[...], att_ref[...], y_ref[...], wa_ref, wy_ref, gx_ref, wq_ref, gq_ref)
    o = jnp.concatenate([_xattn_head(q_heads[h], mk_ref[0, :, _head_slice(h)], mv_ref[0, :, _head_slice(h)])
                         for h in range(X_HEADS)], axis=1)
    out_ref[...] = _mlp_body(x1, o, wo_ref, gm_ref, wu_ref, wd_ref)


def _post(x, att, y, mk, mv, wts, tm, t):
    n = x.shape[0]
    tiles_per_seq = t // tm
    m = mk.shape[1]
    row = lambda w: pl.BlockSpec((tm, w), lambda i: (i, 0))
    mem = pl.BlockSpec((1, m, X_WIDTH), lambda i: (i // tiles_per_seq, 0, 0))
    names = ("w_out_att", "w_out_ssm", "ln_x_g", "wq_x", "qx_norm_g", "wo_x", "ln_mlp_g", "w_up", "w_down")
    big = ("w_up", "w_down")
    weights = [wts[k] for k in names]
    return pl.pallas_call(
        _post_kernel,
        grid=(n // tm,),
        in_specs=[row(D_MODEL), row(ATT_WIDTH), row(SSM_INNER), mem, mem]
                 + [(_const_single if k in big else _const)(wts[k].shape) for k in names],
        out_specs=row(D_MODEL),
        out_shape=jax.ShapeDtypeStruct((n, D_MODEL), F32),
        compiler_params=_params("parallel"),
        name="post",
    )(x, att, y, mk, mv, *weights)


def _memkv_kernel(m_ref, g_ref, wk_ref, wv_ref, gk_ref, mk_ref, mv_ref):
    mn = _rms(m_ref[...], g_ref[...]).astype(BF16)
    mk = _dot(mn, wk_ref[...])
    gk = gk_ref[...]
    for h in range(X_HEADS):
        sl = slice(h * X_HEAD_DIM, (h + 1) * X_HEAD_DIM)
        mk_ref[:, sl] = _rms(mk[:, sl], gk)
    mv_ref[...] = _dot(mn, wv_ref[...])


def _memkv(mem, g, wk, wv, gk, tm):
    n = mem.shape[0]
    row = lambda w: pl.BlockSpec((tm, w), lambda i: (i, 0))
    return pl.pallas_call(
        _memkv_kernel,
        grid=(n // tm,),
        in_specs=[row(D_MODEL), _const(g.shape), _const(wk.shape), _const(wv.shape), _const(gk.shape)],
        out_specs=[row(X_WIDTH), row(X_WIDTH)],
        out_shape=[jax.ShapeDtypeStruct((n, X_WIDTH), F32)] * 2,
        compiler_params=_params("parallel"),
        name="memkv",
    )(mem, g, wk, wv, gk)


def _rope_tables(pos):
    half = HEAD_DIM // 2
    inv_freq = ROPE_THETA ** (-jnp.arange(half, dtype=F32) / half)
    ang = pos.astype(F32)[:, None] * inv_freq[None, :]
    return jnp.cos(ang).T, jnp.sin(ang).T


def _row_tile(n, want):
    tm = min(n, want)
    assert n % tm == 0
    return tm


def _layer(x, nb, cos_t, sin_t, conv0, h0, mem_k, mem_v, wts, attend, tm, act_dtype):
    b, t, _ = x.shape
    n = b * t
    xf = x.reshape(n, D_MODEL)
    qt, kt, vt, z, xbc, dtr = _inproj(xf, nb, wts["ln_mix_g"], wts["w_qkv_t"], wts["w_rest_t"], wts["w_dt_t"],
                                      wts["q_norm_g"], wts["k_norm_g"], cos_t, sin_t, tm)
    att, extras = attend(qt, kt, vt)

    cl = SSD_CHUNK if t >= SSD_CHUNK else -(-t // SSD_MIN_CHUNK) * SSD_MIN_CHUNK
    tp = -(-t // cl) * cl
    pad3 = lambda a: jnp.pad(a.reshape(b, t, -1), ((0, 0), (0, tp - t), (0, 0)))
    y, new_conv, new_h = _ssd(pad3(xbc), pad3(dtr), pad3(z), conv0, h0, wts["conv_w"], wts["conv_b"],
                              wts["dt_bias"], wts["a_log"], wts["d_skip"], wts["ssm_norm_g"],
                              cl, min(t, cl) if tp != t else cl, act_dtype)
    y = y[:, :t].reshape(n, SSM_INNER)

    if t % tm == 0:
        out = _post(xf, att, y, mem_k, mem_v, wts, tm, t)
    else:
        x1, qx = _outproj(xf, att, y, wts["w_out_att"], wts["w_out_ssm"], wts["ln_x_g"],
                          wts["wq_x"], wts["qx_norm_g"], tm, act_dtype)
        o = _xattn_rows(qx.reshape(b, t, X_WIDTH), mem_k, mem_v, act_dtype)
        out = _mlp(x1, o.reshape(n, X_WIDTH), wts["wo_x"], wts["ln_mlp_g"], wts["w_up"], wts["w_down"], tm)
    return out.reshape(b, t, D_MODEL), extras, new_conv, new_h


def kernel(x_prompt, x_sample, mem_prompt, cache_k, cache_v, page_table, state_conv, state_ssm,
           cache_mem_k, cache_mem_v, ln_mix_g, w_in, q_norm_g, k_norm_g, conv_w, conv_b, dt_bias,
           a_log, d_skip, ssm_norm_g, w_out, ln_x_g, ln_mem_g, wq_x, wk_x, wv_x, qx_norm_g,
           kx_norm_g, wo_x, ln_mlp_g, w_up, w_down):
    depth = w_in.shape[0]
    assert depth == 1, "single-layer step"
    bp, tp, _ = x_prompt.shape
    bs, ts, _ = x_sample.shape
    page = cache_k.shape[2]
    past_len = page_table.shape[1] * page
    assert tp % MOBA_BLOCK == 0 and past_len % MOBA_BLOCK == 0 and MOBA_BLOCK % page == 0
    assert ts <= MOBA_BLOCK and ts <= SSD_CHUNK and ts >= CONV_W - 1
    l = 0
    qkv_cols = 3 * ATT_WIDTH
    main_cols = qkv_cols + SSM_INNER + CONV_DIM

    def lanes(vec, width=LANES):
        return jnp.pad(vec.astype(F32), (0, width - vec.shape[0])).reshape(1, width)

    def dim_gain(vec):
        return jnp.broadcast_to(vec.astype(F32)[:, None], (HEAD_DIM, LANES))

    wts = {
        "ln_mix_g": ln_mix_g[l].reshape(1, D_MODEL),
        "w_qkv_t": w_in[l][:, :qkv_cols].T.astype(BF16),
        "w_rest_t": w_in[l][:, qkv_cols:main_cols].T.astype(BF16),
        "w_dt_t": jnp.pad(w_in[l][:, main_cols:].T, ((0, LANES - SSM_HEADS), (0, 0))).astype(BF16),
        "q_norm_g": dim_gain(q_norm_g[l]),
        "k_norm_g": dim_gain(k_norm_g[l]),
        "conv_w": conv_w[l],
        "conv_b": conv_b[l].reshape(1, CONV_DIM),
        "dt_bias": lanes(dt_bias[l]),
        "a_log": lanes(a_log[l]),
        "d_skip": jnp.repeat(d_skip[l].astype(F32), SSM_HEAD_DIM).reshape(1, SSM_INNER),
        "ssm_norm_g": ssm_norm_g[l].reshape(1, SSM_INNER),
        "w_out_att": w_out[l][:ATT_WIDTH].astype(BF16),
        "w_out_ssm": w_out[l][ATT_WIDTH:].astype(BF16),
        "ln_x_g": ln_x_g[l].reshape(1, D_MODEL),
        "wq_x": wq_x[l].astype(BF16),
        "qx_norm_g": qx_norm_g[l].reshape(1, X_HEAD_DIM),
        "wo_x": wo_x[l].astype(BF16),
        "ln_mlp_g": ln_mlp_g[l].reshape(1, D_MODEL),
        "w_up": w_up[l].astype(BF16),
        "w_down": w_down[l].astype(BF16),
    }

    mem_rows = mem_prompt.reshape(-1, D_MODEL)
    mk_p, mv_p = _memkv(mem_rows, ln_mem_g[l].reshape(1, D_MODEL), wk_x[l].astype(BF16),
                        wv_x[l].astype(BF16), kx_norm_g[l].reshape(1, X_HEAD_DIM),
                        _row_tile(mem_rows.shape[0], 512))
    mem_len = mem_prompt.shape[1]
    mk_p = mk_p.reshape(bp, mem_len, X_WIDTH)
    mv_p = mv_p.reshape(bp, mem_len, X_WIDTH)
    cos_p, sin_p = _rope_tables(jnp.arange(tp, dtype=jnp.int32))
    conv0_p = jnp.zeros((bp, CONV_PAD, CONV_DIM), F32)
    h0_p = jnp.zeros((bp, SSM_HEADS * SSM_HEAD_DIM, SSM_STATE), F32)

    def attend_prompt(qt, kt, vt):
        heads_t = lambda a: a.reshape(1, bp, ATT_HEADS, HEAD_DIM, tp).transpose(0, 1, 4, 2, 3)
        return _moba_prompt(qt, kt, vt).reshape(bp * tp, ATT_WIDTH), (heads_t(kt), heads_t(vt))

    y_p, (k_p, v_p), conv_p, ssm_p = _layer(x_prompt, bp, cos_p, sin_p, conv0_p, h0_p, mk_p, mv_p, wts,
                                            attend_prompt, _row_tile(tp, 512), BF16)

    cos_s, sin_s = _rope_tables(past_len + jnp.arange(ts, dtype=jnp.int32))
    cos_s = jnp.tile(cos_s, (1, bs))
    sin_s = jnp.tile(sin_s, (1, bs))
    conv0_s = jnp.pad(state_conv[l], ((0, 0), (CONV_PAD - (CONV_W - 1), 0), (0, 0)))
    h0_s = state_ssm[l].reshape(bs, SSM_HEADS * SSM_HEAD_DIM, SSM_STATE)
    cache_kt = cache_k.transpose(0, 1, 3, 4, 2)
    cache_vt = cache_v.transpose(0, 1, 3, 4, 2)

    def attend_sample(qt, kt, vt):
        rows = lambda a: a[0].T.reshape(bs, ts, ATT_WIDTH)
        k_new, v_new = rows(kt), rows(vt)
        o = _sample_moba(page_table, rows(qt), k_new, v_new, cache_kt, cache_vt)
        heads = lambda a: a.reshape(1, bs, ts, ATT_HEADS, HEAD_DIM)
        return o.reshape(bs * ts, ATT_WIDTH), (heads(k_new), heads(v_new))

    y_s, (k_s, v_s), conv_s, ssm_s = _layer(
        x_sample, 1, cos_s, sin_s, conv0_s, h0_s, cache_mem_k[l].reshape(bs, -1, X_HEAD_DIM),
        cache_mem_v[l].reshape(bs, -1, X_HEAD_DIM), wts, attend_sample, bs * ts, F32)

    state = lambda a, b: a.reshape(1, b, SSM_HEADS, SSM_HEAD_DIM, SSM_STATE)
    return (y_p, y_s, k_p, v_p, conv_p[None], state(ssm_p, bp),
            mk_p.reshape(1, bp, mem_len, X_HEADS, X_HEAD_DIM), mv_p.reshape(1, bp, mem_len, X_HEADS, X_HEAD_DIM),
            k_s, v_s, conv_s[None], state(ssm_s, bs))
```

```python
import functools
import math

import jax
import jax.numpy as jnp
from jax import lax
from jax.experimental import pallas as pl
from jax.experimental.pallas import tpu as pltpu

F32 = jnp.float32
BF16 = jnp.bfloat16

D_MODEL = 1024
ATT_HEADS = 8
HEAD_DIM = 64
ATT_WIDTH = ATT_HEADS * HEAD_DIM
MOBA_BLOCK = 256
TOPK = 3
ROPE_THETA = 10000.0
SSM_HEADS = 8
SSM_HEAD_DIM = 64
SSM_INNER = SSM_HEADS * SSM_HEAD_DIM
SSM_GROUPS = 2
SSM_STATE = 128
CONV_W = 4
CONV_DIM = SSM_INNER + 2 * SSM_GROUPS * SSM_STATE
SSD_CHUNK = 128
X_HEADS = 4
X_HEAD_DIM = 128
X_WIDTH = X_HEADS * X_HEAD_DIM
D_FF = 4 * D_MODEL
EPS = 1e-6

LANES = 128
CONV_PAD = 8
VMEM_LIMIT = 56 * 1024 * 1024
NEG = -1e30
LOG2E = 1.4426950408889634
MOBA_GROUP = 4
ONES_ROWS = 16
FF_CHUNK = 1024
XATTN_SEQS_PER_STEP = 8
SSD_MIN_CHUNK = 16
SSD_SEQS_PER_STEP = 8
PAGE_BUFFERS = 4
PAGES_PER_STEP = 16


def _dot(a, b):
    return jnp.dot(a, b, preferred_element_type=F32)


def _dot_nt(a, b):
    return lax.dot_general(a, b, (((1,), (1,)), ((), ())), preferred_element_type=F32)


def _dot_tn(a, b):
    return lax.dot_general(a, b, (((0,), (0,)), ((), ())), preferred_element_type=F32)


def _split3(x):
    hi = x.astype(BF16)
    r = x - hi.astype(F32)
    mid = r.astype(BF16)
    lo = (r - mid.astype(F32)).astype(BF16)
    return hi, mid, lo


def _dot_f32(a, b, dot=None):
    dot = dot or _dot
    a0, a1, a2 = _split3(a)
    b0, b1, b2 = _split3(b)
    return (dot(a0, b0) + (dot(a0, b1) + dot(a1, b0)) + (dot(a1, b1) + dot(a0, b2) + dot(a2, b0)))


def _rms(x, g):
    ms = jnp.mean(x * x, axis=-1, keepdims=True)
    return x * lax.rsqrt(ms + EPS) * g


def _sigmoid(x):
    return 1.0 / (1.0 + jnp.exp(-x))


def _params(*sem):
    return pltpu.CompilerParams(dimension_semantics=sem, vmem_limit_bytes=VMEM_LIMIT)


def _const(shape):
    return pl.BlockSpec(shape, lambda *_: (0,) * len(shape))


def _const_single(shape):
    return pl.BlockSpec(shape, lambda *_: (0,) * len(shape), pipeline_mode=pl.Buffered(1))


def _inproj_kernel(x_ref, g_ref, wqkv_ref, wrest_ref, wdt_ref, gq_ref, gk_ref, cos_ref, sin_ref,
                   q_ref, k_ref, v_ref, z_ref, xbc_ref, dt_ref):
    hn = _rms(x_ref[...], g_ref[...]).astype(BF16)
    tm = hn.shape[0]
    cos = cos_ref[...]
    sin = sin_ref[...]
    half = HEAD_DIM // 2

    def proj_t(idx):
        return _dot_nt(wqkv_ref[idx * ATT_WIDTH:(idx + 1) * ATT_WIDTH, :], hn)

    def head_norm_rot(p, g_ref_):
        p3 = p.reshape(ATT_HEADS, HEAD_DIM, tm)
        gain = g_ref_[:, 0:1]
        pn = p3 * lax.rsqrt(jnp.mean(p3 * p3, axis=1, keepdims=True) + EPS) * gain
        x1 = pn[:, :half]
        x2 = pn[:, half:]
        rot = jnp.concatenate([x1 * cos - x2 * sin, x2 * cos + x1 * sin], axis=1)
        return rot.reshape(ATT_WIDTH, tm)

    q_ref[0] = head_norm_rot(proj_t(0), gq_ref)
    k_ref[0] = head_norm_rot(proj_t(1), gk_ref)
    v_ref[0] = proj_t(2)
    z_ref[...] = _dot_nt(hn, wrest_ref[0:SSM_INNER, :])
    xbc_ref[...] = _dot_nt(hn, wrest_ref[SSM_INNER:SSM_INNER + CONV_DIM, :])
    dt_ref[...] = _dot_nt(hn, wdt_ref[...])


def _inproj(x, nb, g, w_qkv_t, w_rest, w_dt, gq, gk, cos_t, sin_t, tm):
    n = x.shape[0]
    cols = n // nb
    tpb = cols // tm
    row = lambda w: pl.BlockSpec((tm, w), lambda i: (i, 0))
    tab = pl.BlockSpec((HEAD_DIM // 2, tm), lambda i: (0, i % tpb))
    tspec = pl.BlockSpec((1, ATT_WIDTH, tm), lambda i: (i // tpb, 0, i % tpb))
    outs = ([jax.ShapeDtypeStruct((nb, ATT_WIDTH, cols), F32)] * 3
            + [jax.ShapeDtypeStruct((n, w), F32) for w in (SSM_INNER, CONV_DIM, LANES)])
    return pl.pallas_call(
        _inproj_kernel,
        grid=(n // tm,),
        in_specs=[row(D_MODEL), _const(g.shape), _const_single(w_qkv_t.shape), _const_single(w_rest.shape),
                  _const(w_dt.shape), _const(gq.shape), _const(gk.shape), tab, tab],
        out_specs=[tspec, tspec, tspec, row(SSM_INNER), row(CONV_DIM), row(LANES)],
        out_shape=outs,
        compiler_params=_params("parallel"),
        name="inproj",
    )(x, g, w_qkv_t, w_rest, w_dt, gq, gk, cos_t, sin_t)


def _topk_keep(gate, n_valid, axis):
    idx = lax.broadcasted_iota(jnp.int32, gate.shape, axis)
    cnt = jnp.zeros(gate.shape, jnp.int32)
    for jp in range(gate.shape[axis]):
        gj = lax.slice_in_dim(gate, jp, jp + 1, axis=axis)
        beats = (gj > gate) | ((gj == gate) & (jp < idx))
        cnt = cnt + jnp.where(beats, jnp.where(jp < n_valid, 1, 0), 0)
    return (idx < n_valid) & (cnt < TOPK)


def _moba_kernel(qt_ref, kt_ref, vt_ref, o_ref, kb_ref, vb_ref, km_ref, *, nblk):
    bq = MOBA_BLOCK
    km_ref[...] = jnp.zeros(km_ref.shape, F32)
    for j in range(nblk):
        kj = kt_ref[0, :, j * bq:(j + 1) * bq].T
        kb_ref[j] = kj.astype(BF16)
        km_ref[j:j + 1, :] = jnp.mean(kj, axis=0, keepdims=True)
        vj = vt_ref[0, :, j * bq:(j + 1) * bq].astype(BF16)
        for h in range(2):
            vb_ref[j, h, 0:HEAD_DIM, :] = vj[h * HEAD_DIM:(h + 1) * HEAD_DIM, :]
            vb_ref[j, h, HEAD_DIM:, :] = jnp.ones((ONES_ROWS, bq), BF16)
    kmean = km_ref[...]

    keyi = lax.broadcasted_iota(jnp.int32, (bq, bq), 0)
    qryi = lax.broadcasted_iota(jnp.int32, (bq, bq), 1)
    causal = jnp.where(keyi <= qryi, 0.0, NEG)
    row_lo = lax.broadcasted_iota(jnp.int32, (LANES, bq), 0) < HEAD_DIM
    qscale = (HEAD_DIM ** -0.5) * LOG2E

    order = [blk for pair in zip(range(nblk), reversed(range(nblk))) for blk in pair][:nblk]
    for g0 in range(0, nblk, MOBA_GROUP):
        owns = order[g0:g0 + MOBA_GROUP]
        work = []
        for own in owns:
            qt = qt_ref[0, :, own * bq:(own + 1) * bq]
            for h in range(2):
                qh = jnp.where(row_lo, 0.0, qt) if h else jnp.where(row_lo, qt, 0.0)
                qsb = (qh * qscale).astype(BF16)
                bias = None
                if own > TOPK:
                    keep = _topk_keep(_dot_f32(kmean, qh), own, 0)
                    bias = jnp.where(keep, 0.0, NEG)
                tiles = []
                m = None
                for j in range(own + 1):
                    s = _dot(kb_ref[j], qsb)
                    if j == own:
                        s = s + causal
                    elif bias is not None:
                        s = s + bias[j:j + 1, :]
                    tiles.append(s)
                    mj = jnp.max(s, axis=0, keepdims=True)
                    m = mj if m is None else jnp.maximum(m, mj)
                work.append((own, h, tiles, m))
        outs = {}
        for own, h, tiles, m in work:
            acc = jnp.zeros((HEAD_DIM + ONES_ROWS, bq), F32)
            for j in range(own + 1):
                p = jnp.exp2(tiles[j] - m)
                acc = acc + _dot(vb_ref[j, h], p.astype(BF16))
            outs[(own, h)] = acc[0:HEAD_DIM] / acc[HEAD_DIM:HEAD_DIM + 1]
        for own in owns:
            o_t = jnp.concatenate([outs[(own, 0)], outs[(own, 1)]], axis=0)
            o_ref[0, own * bq:(own + 1) * bq, :] = o_t.T.astype(o_ref.dtype)


def _moba_prompt(qt, kt, vt):
    b, _, t = qt.shape
    nblk = t // MOBA_BLOCK
    npad = max(16, nblk)
    npair = ATT_WIDTH // LANES
    spec = pl.BlockSpec((1, LANES, t), lambda bi, p: (bi, p, 0))
    return pl.pallas_call(
        functools.partial(_moba_kernel, nblk=nblk),
        grid=(b, npair),
        in_specs=[spec, spec, spec],
        out_specs=pl.BlockSpec((1, t, LANES), lambda bi, p: (bi, 0, p)),
        out_shape=jax.ShapeDtypeStruct((b, t, ATT_WIDTH), BF16),
        scratch_shapes=[pltpu.VMEM((nblk, MOBA_BLOCK, LANES), BF16),
                        pltpu.VMEM((nblk, 2, HEAD_DIM + ONES_ROWS, MOBA_BLOCK), BF16),
                        pltpu.VMEM((npad, LANES), F32)],
        compiler_params=_params("parallel", "parallel"),
        name="moba_prompt",
    )(qt, kt, vt)


def _expand_heads(q):
    t = q.shape[0]
    tiled = jnp.concatenate([q] * ATT_HEADS, axis=0)
    rowh = lax.broadcasted_iota(jnp.int32, tiled.shape, 0) // t
    laneh = lax.broadcasted_iota(jnp.int32, tiled.shape, 1) // HEAD_DIM
    return jnp.where(rowh == laneh, tiled, 0.0)


def _sample_moba_kernel(pt_ref, q_ref, kn_ref, vn_ref, ck_ref, cv_ref, o_ref, km_ref, m_ref, l_ref, oblk_ref,
                        kbuf_ref, vbuf_ref, sem_ref, *, page, npg, nblk):
    c = pl.program_id(1)
    nsteps = pl.num_programs(1)
    step = pl.program_id(0) * nsteps + c
    total = pl.num_programs(0) * nsteps

    def page_copies(st):
        slot = st % PAGE_BUFFERS
        seq = st // nsteps
        first = (st % nsteps) * npg
        out = []
        for pi in range(npg):
            pg = pt_ref[seq, first + pi]
            out.append(pltpu.make_async_copy(ck_ref.at[0, pg], kbuf_ref.at[slot, pi], sem_ref.at[0, slot]))
            out.append(pltpu.make_async_copy(cv_ref.at[0, pg], vbuf_ref.at[slot, pi], sem_ref.at[1, slot]))
        return out

    @pl.when(step == 0)
    def _():
        for ahead in range(PAGE_BUFFERS - 1):
            for cp in page_copies(ahead):
                cp.start()

    @pl.when(step + (PAGE_BUFFERS - 1) < total)
    def _():
        for cp in page_copies(step + (PAGE_BUFFERS - 1)):
            cp.start()

    for cp in page_copies(step):
        cp.wait()
    slot = step % PAGE_BUFFERS
    k_refs = [kbuf_ref.at[slot, pi] for pi in range(npg)]
    v_refs = [vbuf_ref.at[slot, pi] for pi in range(npg)]
    t = q_ref.shape[1]
    rows = ATT_HEADS * t
    pages_per_block = MOBA_BLOCK // page
    blocks_per_step = npg // pages_per_block

    qe = _expand_heads(q_ref[0])
    qb = (qe * ((HEAD_DIM ** -0.5) * LOG2E)).astype(BF16)
    blk_lane = lax.broadcasted_iota(jnp.int32, (ATT_WIDTH, LANES), 1)

    @pl.when(c == 0)
    def _():
        km_ref[...] = jnp.zeros(km_ref.shape, F32)

    tiles = []
    for jj in range(blocks_per_step):
        ksum = None
        for u in range(pages_per_block):
            kt = k_refs[jj * pages_per_block + u][...].reshape(ATT_WIDTH, page)
            ksum = kt if ksum is None else ksum + kt
            tiles.append(_dot(qb, kt.astype(BF16)))
        kmean = jnp.sum(ksum, axis=1, keepdims=True) * (1.0 / MOBA_BLOCK)
        km_ref[...] = jnp.where(blk_lane == c * blocks_per_step + jj, kmean, km_ref[...])
    probs = []
    for jj in range(blocks_per_step):
        mine = tiles[jj * pages_per_block:(jj + 1) * pages_per_block]
        m = jnp.max(mine[0], axis=1, keepdims=True)
        for s in mine[1:]:
            m = jnp.maximum(m, jnp.max(s, axis=1, keepdims=True))
        l = jnp.zeros((rows, 1), F32)
        for s in mine:
            p = jnp.exp2(s - m)
            l = l + jnp.sum(p, axis=1, keepdims=True)
            probs.append(p.astype(BF16))
        m_ref[c * blocks_per_step + jj] = jnp.broadcast_to(m, (rows, LANES))
        l_ref[c * blocks_per_step + jj] = jnp.broadcast_to(l, (rows, LANES))
    for jj in range(blocks_per_step):
        o = jnp.zeros((rows, ATT_WIDTH), F32)
        for u in range(pages_per_block):
            vt = v_refs[jj * pages_per_block + u][...].reshape(ATT_WIDTH, page).astype(BF16)
            o = o + _dot_nt(probs[jj * pages_per_block + u], vt)
        oblk_ref[c * blocks_per_step + jj] = o

    @pl.when(c == pl.num_programs(1) - 1)
    def _():
        keep = _topk_keep(_dot_f32(qe, km_ref[:, 0:nblk]), nblk, 1)
        s_own = _dot_nt(qb, kn_ref[0].astype(BF16))
        qpos = lax.broadcasted_iota(jnp.int32, s_own.shape, 0) % t
        kpos = lax.broadcasted_iota(jnp.int32, s_own.shape, 1)
        s_own = jnp.where(kpos <= qpos, s_own, NEG)
        lane_j = lax.broadcasted_iota(jnp.int32, (rows, nblk), 1)
        m_all = jnp.zeros((rows, nblk), F32)
        l_all = jnp.zeros((rows, nblk), F32)
        for j in range(nblk):
            m_all = jnp.where(lane_j == j, m_ref[j][:, 0:nblk], m_all)
            l_all = jnp.where(lane_j == j, l_ref[j][:, 0:nblk], l_all)
        mx = jnp.maximum(jnp.max(s_own, axis=1, keepdims=True),
                         jnp.max(jnp.where(keep, m_all, NEG), axis=1, keepdims=True))
        p_own = jnp.exp2(s_own - mx)
        w = jnp.where(keep, jnp.exp2(jnp.minimum(m_all - mx, 0.0)), 0.0)
        den = jnp.sum(p_own, axis=1, keepdims=True) + jnp.sum(w * l_all, axis=1, keepdims=True)
        vn = vn_ref[0]
        num = jnp.zeros((rows, ATT_WIDTH), F32)
        for tt in range(t):
            num = num + p_own[:, tt:tt + 1] * vn[tt:tt + 1, :]
        for j in range(nblk):
            num = num + w[:, j:j + 1] * oblk_ref[j]
        full = num / den
        laneh = lax.broadcasted_iota(jnp.int32, (t, ATT_WIDTH), 1) // HEAD_DIM
        out = jnp.zeros((t, ATT_WIDTH), F32)
        for h in range(ATT_HEADS):
            out = out + jnp.where(laneh == h, full[h * t:(h + 1) * t, :], 0.0)
        o_ref[0] = out


def _sample_moba(page_table, q, k_new, v_new, cache_kt, cache_vt):
    b, t, _ = q.shape
    page = cache_kt.shape[4]
    n_pages = page_table.shape[1]
    npg = math.gcd(n_pages, PAGES_PER_STEP)
    nsteps = n_pages // npg
    nblk = n_pages * page // MOBA_BLOCK
    assert nblk <= LANES and npg % (MOBA_BLOCK // page) == 0 and b * nsteps >= PAGE_BUFFERS
    rows = ATT_HEADS * t
    per_b = pl.BlockSpec((1, t, ATT_WIDTH), lambda bi, c, pt: (bi, 0, 0))
    hbm = pl.BlockSpec(memory_space=pl.ANY)
    page_buf = pltpu.VMEM((PAGE_BUFFERS, npg, ATT_HEADS, HEAD_DIM, page), F32)

    grid_spec = pltpu.PrefetchScalarGridSpec(
        num_scalar_prefetch=1,
        grid=(b, nsteps),
        in_specs=[per_b, per_b, per_b, hbm, hbm],
        out_specs=per_b,
        scratch_shapes=[pltpu.VMEM((ATT_WIDTH, LANES), F32),
                        pltpu.VMEM((nblk, rows, LANES), F32),
                        pltpu.VMEM((nblk, rows, LANES), F32),
                        pltpu.VMEM((nblk, rows, ATT_WIDTH), F32),
                        page_buf, page_buf,
                        pltpu.SemaphoreType.DMA((2, PAGE_BUFFERS))],
    )
    return pl.pallas_call(
        functools.partial(_sample_moba_kernel, page=page, npg=npg, nblk=nblk),
        grid_spec=grid_spec,
        out_shape=jax.ShapeDtypeStruct((b, t, ATT_WIDTH), F32),
        compiler_params=_params("arbitrary", "arbitrary"),
        name="sample_moba",
    )(page_table, q, k_new, v_new, cache_kt, cache_vt)


def _cumsum_rows(x):
    n = x.shape[0]
    row = lax.broadcasted_iota(jnp.int32, x.shape, 0)
    step = 1
    while step < n:
        x = x + jnp.where(row >= step, pltpu.roll(x, step, 0), 0.0)
        step *= 2
    return x


def _ssd_kernel(xbc_ref, dtr_ref, z_ref, conv0_ref, h0_ref, cw_ref, cb_ref, dtb_ref, alog_ref,
                dsk_ref, gn_ref, y_ref, convo_ref, ho_ref, xpad_ref, h_ref, *, cl, valid, nbat):
    c = pl.program_id(1)
    nc = pl.num_programs(1)

    @pl.when(c == 0)
    def _():
        for bb in range(nbat):
            xpad_ref[bb, 0:CONV_PAD, :] = conv0_ref[bb]
            h_ref[bb] = h0_ref[bb]

    for bb in range(nbat):
        _ssd_chunk(bb, xbc_ref, dtr_ref, z_ref, cw_ref, cb_ref, dtb_ref, alog_ref, dsk_ref, gn_ref,
                   y_ref, xpad_ref, h_ref, cl, valid)

    @pl.when(c == nc - 1)
    def _():
        for bb in range(nbat):
            convo_ref[bb] = xpad_ref[bb, CONV_PAD + valid - (CONV_W - 1):CONV_PAD + valid, :]
            ho_ref[bb] = h_ref[bb]


def _ssd_chunk(bb, xbc_ref, dtr_ref, z_ref, cw_ref, cb_ref, dtb_ref, alog_ref, dsk_ref, gn_ref,
               y_ref, xpad_ref, h_ref, cl, valid):
    x = xbc_ref[bb]
    xpad_ref[bb, CONV_PAD:CONV_PAD + cl, :] = x
    w = cw_ref[...]
    conv = cb_ref[...] + x * w[CONV_W - 1:CONV_W, :]
    for back in range(1, CONV_W):
        conv = conv + xpad_ref[bb, pl.ds(CONV_PAD - back, cl), :] * w[CONV_W - 1 - back:CONV_W - back, :]

    xpad_ref[bb, 0:CONV_PAD, :] = xpad_ref[bb, cl:cl + CONV_PAD, :]
    xc = conv * _sigmoid(conv)
    xs = xc[:, :SSM_INNER]
    gw = SSM_STATE
    bm = [xc[:, SSM_INNER + g * gw:SSM_INNER + (g + 1) * gw].astype(BF16) for g in range(SSM_GROUPS)]
    cm = [xc[:, SSM_INNER + (SSM_GROUPS + g) * gw:SSM_INNER + (SSM_GROUPS + g + 1) * gw].astype(BF16)
          for g in range(SSM_GROUPS)]

    pre = dtr_ref[bb] + dtb_ref[...]
    dt = jnp.maximum(pre, 0.0) + jnp.log1p(jnp.exp(-jnp.abs(pre)))
    rowl = lax.broadcasted_iota(jnp.int32, dt.shape, 0)
    if valid < cl:
        dt = jnp.where(rowl < valid, dt, 0.0)
    da = dt * (-jnp.exp(alog_ref[...]))
    cum = _cumsum_rows(da)
    if cl < LANES:
        cum_t = jnp.concatenate([cum, jnp.zeros((LANES - cl, LANES), F32)], axis=0).T[:, 0:cl]
    else:
        cum_t = cum.T
    cum_last = cum[cl - 1:cl, :]
    ecum = jnp.exp(cum)
    tail = jnp.exp(cum_last - cum)
    elast = jnp.exp(cum_last)

    ri = lax.broadcasted_iota(jnp.int32, (cl, cl), 0)
    ci = lax.broadcasted_iota(jnp.int32, (cl, cl), 1)
    causal = ci <= ri
    lane = lax.broadcasted_iota(jnp.int32, (cl, LANES), 1)
    lo = lane < SSM_HEAD_DIM
    row_lo = lax.broadcasted_iota(jnp.int32, (LANES, 1), 0) < SSM_HEAD_DIM
    heads_per_group = SSM_HEADS // SSM_GROUPS
    gmat = [_dot_nt(cm[g], bm[g]) for g in range(SSM_GROUPS)]

    def pick(cols, h0):
        return jnp.where(lo, cols[:, h0:h0 + 1], cols[:, h0 + 1:h0 + 2])

    ys = []
    for p in range(SSM_HEADS // 2):
        h0 = 2 * p
        g = h0 // heads_per_group
        xs_p = xs[:, p * LANES:(p + 1) * LANES]
        xdt = xs_p * pick(dt, h0)
        xdt_b = xdt.astype(BF16)
        parts = []
        for h in (h0, h0 + 1):
            seg = cum[:, h:h + 1] - cum_t[h:h + 1, :]
            dec = jnp.where(causal, jnp.exp(jnp.minimum(seg, 0.0)), 0.0)
            parts.append(_dot((gmat[g] * dec).astype(BF16), xdt_b))
        hp = h_ref[bb, p * LANES:(p + 1) * LANES, :]
        y_p = jnp.where(lo, parts[0], parts[1])
        y_p = y_p + _dot_nt(cm[g], hp.astype(BF16)) * pick(ecum, h0)
        y_p = y_p + dsk_ref[:, p * LANES:(p + 1) * LANES] * xs_p
        xw = (xdt * pick(tail, h0)).astype(BF16)
        decay_rows = jnp.where(row_lo, elast[:, h0:h0 + 1], elast[:, h0 + 1:h0 + 2])
        h_ref[bb, p * LANES:(p + 1) * LANES, :] = hp * decay_rows + _dot_tn(xw, bm[g])
        ys.append(y_p)

    y = jnp.concatenate(ys, axis=1)
    zz = z_ref[bb]
    yz = y * (zz * _sigmoid(zz))
    gwid = SSM_INNER // SSM_GROUPS
    normed = []
    for g in range(SSM_GROUPS):
        part = yz[:, g * gwid:(g + 1) * gwid]
        normed.append(part * lax.rsqrt(jnp.mean(part * part, axis=-1, keepdims=True) + EPS))
    y_ref[bb] = (jnp.concatenate(normed, axis=1) * gn_ref[...]).astype(y_ref.dtype)


def _ssd(xbc, dtr, z, conv0, h0, cw, cb, dtb, alog, dsk, gn, cl, valid, out_dtype):
    b, t, _ = xbc.shape
    nc = t // cl
    nbat = math.gcd(b, SSD_SEQS_PER_STEP)
    rows = lambda w: pl.BlockSpec((nbat, cl, w), lambda bi, c: (bi, c, 0))
    per_b = lambda shape: pl.BlockSpec((nbat,) + shape, lambda bi, c: (bi, 0, 0))
    hshape = (SSM_HEADS * SSM_HEAD_DIM, SSM_STATE)
    return pl.pallas_call(
        functools.partial(_ssd_kernel, cl=cl, valid=valid, nbat=nbat),
        grid=(b // nbat, nc),
        in_specs=[rows(CONV_DIM), rows(LANES), rows(SSM_INNER), per_b((CONV_PAD, CONV_DIM)),
                  per_b(hshape), _const(cw.shape), _const(cb.shape), _const(dtb.shape),
                  _const(alog.shape), _const(dsk.shape), _const(gn.shape)],
        out_specs=[rows(SSM_INNER), per_b((CONV_W - 1, CONV_DIM)), per_b(hshape)],
        out_shape=[jax.ShapeDtypeStruct((b, t, SSM_INNER), out_dtype),
                   jax.ShapeDtypeStruct((b, CONV_W - 1, CONV_DIM), F32),
                   jax.ShapeDtypeStruct((b,) + hshape, F32)],
        scratch_shapes=[pltpu.VMEM((nbat, CONV_PAD + cl, CONV_DIM), F32), pltpu.VMEM((nbat,) + hshape, F32)],
        compiler_params=_params("parallel", "arbitrary"),
        name="ssd",
    )(xbc, dtr, z, conv0, h0, cw, cb, dtb, alog, dsk, gn)


def _head_slice(h):
    return slice(h * X_HEAD_DIM, (h + 1) * X_HEAD_DIM)


def _outproj_body(x, att, y, wa_ref, wy_ref, g_ref, wq_ref, gq_ref):
    x1 = x + _dot(att.astype(BF16), wa_ref[...]) + _dot(y.astype(BF16), wy_ref[...])
    qx = _dot(_rms(x1, g_ref[...]).astype(BF16), wq_ref[...])
    gq = gq_ref[...]
    return x1, [_rms(qx[:, _head_slice(h)], gq) for h in range(X_HEADS)]


def _outproj_kernel(x_ref, att_ref, y_ref, wa_ref, wy_ref, g_ref, wq_ref, gq_ref, x1_ref, qx_ref):
    x1, q_heads = _outproj_body(x_ref[...], att_ref[...], y_ref[...], wa_ref, wy_ref, g_ref, wq_ref, gq_ref)
    x1_ref[...] = x1
    for h in range(X_HEADS):
        qx_ref[:, _head_slice(h)] = q_heads[h].astype(qx_ref.dtype)


def _outproj(x, att, y, w_att, w_y, g, wq, gq, tm, act_dtype):
    n = x.shape[0]
    row = lambda w: pl.BlockSpec((tm, w), lambda i: (i, 0))
    return pl.pallas_call(
        _outproj_kernel,
        grid=(n // tm,),
        in_specs=[row(D_MODEL), row(ATT_WIDTH), row(SSM_INNER), _const(w_att.shape), _const(w_y.shape),
                  _const(g.shape), _const(wq.shape), _const(gq.shape)],
        out_specs=[row(D_MODEL), row(X_WIDTH)],
        out_shape=[jax.ShapeDtypeStruct((n, D_MODEL), F32), jax.ShapeDtypeStruct((n, X_WIDTH), act_dtype)],
        compiler_params=_params("parallel"),
        name="outproj",
    )(x, att, y, w_att, w_y, g, wq, gq)


def _xattn_head(q, k, v):
    s = _dot_nt((q.astype(F32) * (X_HEAD_DIM ** -0.5)).astype(BF16), k.astype(BF16))
    p = jnp.exp(s - jnp.max(s, axis=-1, keepdims=True))
    return _dot(p.astype(BF16), v.astype(BF16)) / jnp.sum(p, axis=-1, keepdims=True)


def _xattn_rows_kernel(q_ref, k_ref, v_ref, o_ref):
    gb, t, _ = q_ref.shape
    flat = k_ref.shape[1]
    rowh = lax.broadcasted_iota(jnp.int32, (X_HEADS * t, flat), 0) // t
    colh = lax.broadcasted_iota(jnp.int32, (X_HEADS * t, flat), 1) % X_HEADS
    hmask = jnp.where(rowh == colh, 0.0, NEG)
    for g in range(gb):
        q = q_ref[g].astype(F32)
        q_rows = jnp.concatenate([q[:, _head_slice(h)] for h in range(X_HEADS)], axis=0)
        s = _dot_nt((q_rows * (X_HEAD_DIM ** -0.5)).astype(BF16), k_ref[g].astype(BF16)) + hmask
        p = jnp.exp(s - jnp.max(s, axis=-1, keepdims=True))
        o_rows = _dot(p.astype(BF16), v_ref[g].astype(BF16)) / jnp.sum(p, axis=-1, keepdims=True)
        for h in range(X_HEADS):
            o_ref[g, :, _head_slice(h)] = o_rows[h * t:(h + 1) * t].astype(o_ref.dtype)


def _xattn_rows(q, mk, mv, act_dtype):
    b, t, _ = q.shape
    flat = mk.shape[1]
    gb = math.gcd(b, XATTN_SEQS_PER_STEP)
    qspec = pl.BlockSpec((gb, t, X_WIDTH), lambda bi: (bi, 0, 0))
    kvspec = pl.BlockSpec((gb, flat, X_HEAD_DIM), lambda bi: (bi, 0, 0))
    return pl.pallas_call(
        _xattn_rows_kernel,
        grid=(b // gb,),
        in_specs=[qspec, kvspec, kvspec],
        out_specs=qspec,
        out_shape=jax.ShapeDtypeStruct((b, t, X_WIDTH), act_dtype),
        compiler_params=_params("parallel"),
        name="xattn",
    )(q, mk, mv)


def _mlp_body(x1, o, wo_ref, g_ref, wu_ref, wd_ref):
    x2 = x1 + _dot(o.astype(BF16), wo_ref[...])
    hm = _rms(x2, g_ref[...]).astype(BF16)
    acc = x2
    for cidx in range(D_FF // FF_CHUNK):
        sl = slice(cidx * FF_CHUNK, (cidx + 1) * FF_CHUNK)
        up = jnp.maximum(_dot(hm, wu_ref[:, sl]), 0.0)
        acc = acc + _dot((up * up).astype(BF16), wd_ref[sl, :])
    return acc


def _mlp_kernel(x1_ref, o_ref, wo_ref, g_ref, wu_ref, wd_ref, out_ref):
    out_ref[...] = _mlp_body(x1_ref[...], o_ref[...], wo_ref, g_ref, wu_ref, wd_ref)


def _mlp(x1, o, wo, g, wu, wd, tm):
    n = x1.shape[0]
    row = lambda w: pl.BlockSpec((tm, w), lambda i: (i, 0))
    return pl.pallas_call(
        _mlp_kernel,
        grid=(n // tm,),
        in_specs=[row(D_MODEL), row(X_WIDTH), _const_single(wo.shape), _const(g.shape),
                  _const_single(wu.shape), _const_single(wd.shape)],
        out_specs=row(D_MODEL),
        out_shape=jax.ShapeDtypeStruct((n, D_MODEL), F32),
        compiler_params=_params("parallel"),
        name="mlp",
    )(x1, o, wo, g, wu, wd)


def _post_kernel(x_ref, att_ref, y_ref, mk_ref, mv_ref, wa_ref, wy_ref, gx_ref, wq_ref, gq_ref,
                 wo_ref, gm_ref, wu_ref, wd_ref, out_ref):
    x1, q_heads = _outproj_body(x_ref[...], att_ref[...], y_ref[...], wa_ref, wy_ref, gx_ref, wq_ref, gq_ref)
    o = jnp.concatenate([_xattn_head(q_heads[h], mk_ref[0, :, _head_slice(h)], mv_ref[0, :, _head_slice(h)])
                         for h in range(X_HEADS)], axis=1)
    out_ref[...] = _mlp_body(x1, o, wo_ref, gm_ref, wu_ref, wd_ref)


def _post(x, att, y, mk, mv, wts, tm, t):
    n = x.shape[0]
    tiles_per_seq = t // tm
    m = mk.shape[1]
    row = lambda w: pl.BlockSpec((tm, w), lambda i: (i, 0))
    mem = pl.BlockSpec((1, m, X_WIDTH), lambda i: (i // tiles_per_seq, 0, 0))
    names = ("w_out_att", "w_out_ssm", "ln_x_g", "wq_x", "qx_norm_g", "wo_x", "ln_mlp_g", "w_up", "w_down")
    big = ("w_up", "w_down")
    weights = [wts[k] for k in names]
    return pl.pallas_call(
        _post_kernel,
        grid=(n // tm,),
        in_specs=[row(D_MODEL), row(ATT_WIDTH), row(SSM_INNER), mem, mem]
                 + [(_const_single if k in big else _const)(wts[k].shape) for k in names],
        out_specs=row(D_MODEL),
        out_shape=jax.ShapeDtypeStruct((n, D_MODEL), F32),
        compiler_params=_params("parallel"),
        name="post",
    )(x, att, y, mk, mv, *weights)


def _memkv_kernel(m_ref, g_ref, wk_ref, wv_ref, gk_ref, mk_ref, mv_ref):
    mn = _rms(m_ref[...], g_ref[...]).astype(BF16)
    mk = _dot(mn, wk_ref[...])
    gk = gk_ref[...]
    for h in range(X_HEADS):
        sl = slice(h * X_HEAD_DIM, (h + 1) * X_HEAD_DIM)
        mk_ref[:, sl] = _rms(mk[:, sl], gk)
    mv_ref[...] = _dot(mn, wv_ref[...])


def _memkv(mem, g, wk, wv, gk, tm):
    n = mem.shape[0]
    row = lambda w: pl.BlockSpec((tm, w), lambda i: (i, 0))
    return pl.pallas_call(
        _memkv_kernel,
        grid=(n // tm,),
        in_specs=[row(D_MODEL), _const(g.shape), _const(wk.shape), _const(wv.shape), _const(gk.shape)],
        out_specs=[row(X_WIDTH), row(X_WIDTH)],
        out_shape=[jax.ShapeDtypeStruct((n, X_WIDTH), F32)] * 2,
        compiler_params=_params("parallel"),
        name="memkv",
    )(mem, g, wk, wv, gk)


def _rope_tables(pos):
    half = HEAD_DIM // 2
    inv_freq = ROPE_THETA ** (-jnp.arange(half, dtype=F32) / half)
    ang = pos.astype(F32)[:, None] * inv_freq[None, :]
    return jnp.cos(ang).T, jnp.sin(ang).T


def _row_tile(n, want):
    tm = min(n, want)
    assert n % tm == 0
    return tm


def _layer(x, nb, cos_t, sin_t, conv0, h0, mem_k, mem_v, wts, attend, tm, act_dtype):
    b, t, _ = x.shape
    n = b * t
    xf = x.reshape(n, D_MODEL)
    qt, kt, vt, z, xbc, dtr = _inproj(xf, nb, wts["ln_mix_g"], wts["w_qkv_t"], wts["w_rest_t"], wts["w_dt_t"],
                                      wts["q_norm_g"], wts["k_norm_g"], cos_t, sin_t, tm)
    att, extras = attend(qt, kt, vt)

    cl = SSD_CHUNK if t >= SSD_CHUNK else -(-t // SSD_MIN_CHUNK) * SSD_MIN_CHUNK
    tp = -(-t // cl) * cl
    pad3 = lambda a: jnp.pad(a.reshape(b, t, -1), ((0, 0), (0, tp - t), (0, 0)))
    y, new_conv, new_h = _ssd(pad3(xbc), pad3(dtr), pad3(z), conv0, h0, wts["conv_w"], wts["conv_b"],
                              wts["dt_bias"], wts["a_log"], wts["d_skip"], wts["ssm_norm_g"],
                              cl, min(t, cl) if tp != t else cl, act_dtype)
    y = y[:, :t].reshape(n, SSM_INNER)

    if t % tm == 0:
        out = _post(xf, att, y, mem_k, mem_v, wts, tm, t)
    else:
        x1, qx = _outproj(xf, att, y, wts["w_out_att"], wts["w_out_ssm"], wts["ln_x_g"],
                          wts["wq_x"], wts["qx_norm_g"], tm, act_dtype)
        o = _xattn_rows(qx.reshape(b, t, X_WIDTH), mem_k, mem_v, act_dtype)
        out = _mlp(x1, o.reshape(n, X_WIDTH), wts["wo_x"], wts["ln_mlp_g"], wts["w_up"], wts["w_down"], tm)
    return out.reshape(b, t, D_MODEL), extras, new_conv, new_h


def kernel(x_prompt, x_sample, mem_prompt, cache_k, cache_v, page_table, state_conv, state_ssm,
           cache_mem_k, cache_mem_v, ln_mix_g, w_in, q_norm_g, k_norm_g, conv_w, conv_b, dt_bias,
           a_log, d_skip, ssm_norm_g, w_out, ln_x_g, ln_mem_g, wq_x, wk_x, wv_x, qx_norm_g,
           kx_norm_g, wo_x, ln_mlp_g, w_up, w_down):
    depth = w_in.shape[0]
    assert depth == 1, "single-layer step"
    bp, tp, _ = x_prompt.shape
    bs, ts, _ = x_sample.shape
    page = cache_k.shape[2]
    past_len = page_table.shape[1] * page
    assert tp % MOBA_BLOCK == 0 and past_len % MOBA_BLOCK == 0 and MOBA_BLOCK % page == 0
    assert ts <= MOBA_BLOCK and ts <= SSD_CHUNK and ts >= CONV_W - 1
    l = 0
    qkv_cols = 3 * ATT_WIDTH
    main_cols = qkv_cols + SSM_INNER + CONV_DIM

    def lanes(vec, width=LANES):
        return jnp.pad(vec.astype(F32), (0, width - vec.shape[0])).reshape(1, width)

    def dim_gain(vec):
        return jnp.broadcast_to(vec.astype(F32)[:, None], (HEAD_DIM, LANES))

    wts = {
        "ln_mix_g": ln_mix_g[l].reshape(1, D_MODEL),
        "w_qkv_t": w_in[l][:, :qkv_cols].T.astype(BF16),
        "w_rest_t": w_in[l][:, qkv_cols:main_cols].T.astype(BF16),
        "w_dt_t": jnp.pad(w_in[l][:, main_cols:].T, ((0, LANES - SSM_HEADS), (0, 0))).astype(BF16),
        "q_norm_g": dim_gain(q_norm_g[l]),
        "k_norm_g": dim_gain(k_norm_g[l]),
        "conv_w": conv_w[l],
        "conv_b": conv_b[l].reshape(1, CONV_DIM),
        "dt_bias": lanes(dt_bias[l]),
        "a_log": lanes(a_log[l]),
        "d_skip": jnp.repeat(d_skip[l].astype(F32), SSM_HEAD_DIM).reshape(1, SSM_INNER),
        "ssm_norm_g": ssm_norm_g[l].reshape(1, SSM_INNER),
        "w_out_att": w_out[l][:ATT_WIDTH].astype(BF16),
        "w_out_ssm": w_out[l][ATT_WIDTH:].astype(BF16),
        "ln_x_g": ln_x_g[l].reshape(1, D_MODEL),
        "wq_x": wq_x[l].astype(BF16),
        "qx_norm_g": qx_norm_g[l].reshape(1, X_HEAD_DIM),
        "wo_x": wo_x[l].astype(BF16),
        "ln_mlp_g": ln_mlp_g[l].reshape(1, D_MODEL),
        "w_up": w_up[l].astype(BF16),
        "w_down": w_down[l].astype(BF16),
    }

    mem_rows = mem_prompt.reshape(-1, D_MODEL)
    mk_p, mv_p = _memkv(mem_rows, ln_mem_g[l].reshape(1, D_MODEL), wk_x[l].astype(BF16),
                        wv_x[l].astype(BF16), kx_norm_g[l].reshape(1, X_HEAD_DIM),
                        _row_tile(mem_rows.shape[0], 512))
    mem_len = mem_prompt.shape[1]
    mk_p = mk_p.reshape(bp, mem_len, X_WIDTH)
    mv_p = mv_p.reshape(bp, mem_len, X_WIDTH)
    cos_p, sin_p = _rope_tables(jnp.arange(tp, dtype=jnp.int32))
    conv0_p = jnp.zeros((bp, CONV_PAD, CONV_DIM), F32)
    h0_p = jnp.zeros((bp, SSM_HEADS * SSM_HEAD_DIM, SSM_STATE), F32)

    def attend_prompt(qt, kt, vt):
        heads_t = lambda a: a.reshape(1, bp, ATT_HEADS, HEAD_DIM, tp).transpose(0, 1, 4, 2, 3)
        return _moba_prompt(qt, kt, vt).reshape(bp * tp, ATT_WIDTH), (heads_t(kt), heads_t(vt))

    y_p, (k_p, v_p), conv_p, ssm_p = _layer(x_prompt, bp, cos_p, sin_p, conv0_p, h0_p, mk_p, mv_p, wts,
                                            attend_prompt, _row_tile(tp, 512), BF16)

    cos_s, sin_s = _rope_tables(past_len + jnp.arange(ts, dtype=jnp.int32))
    cos_s = jnp.tile(cos_s, (1, bs))
    sin_s = jnp.tile(sin_s, (1, bs))
    conv0_s = jnp.pad(state_conv[l], ((0, 0), (CONV_PAD - (CONV_W - 1), 0), (0, 0)))
    h0_s = state_ssm[l].reshape(bs, SSM_HEADS * SSM_HEAD_DIM, SSM_STATE)
    cache_kt = cache_k.transpose(0, 1, 3, 4, 2)
    cache_vt = cache_v.transpose(0, 1, 3, 4, 2)

    def attend_sample(qt, kt, vt):
        rows = lambda a: a[0].T.reshape(bs, ts, ATT_WIDTH)
        k_new, v_new = rows(kt), rows(vt)
        o = _sample_moba(page_table, rows(qt), k_new, v_new, cache_kt, cache_vt)
        heads = lambda a: a.reshape(1, bs, ts, ATT_HEADS, HEAD_DIM)
        return o.reshape(bs * ts, ATT_WIDTH), (heads(k_new), heads(v_new))

    y_s, (k_s, v_s), conv_s, ssm_s = _layer(
        x_sample, 1, cos_s, sin_s, conv0_s, h0_s, cache_mem_k[l].reshape(bs, -1, X_HEAD_DIM),
        cache_mem_v[l].reshape(bs, -1, X_HEAD_DIM), wts, attend_sample, bs * ts, F32)

    state = lambda a, b: a.reshape(1, b, SSM_HEADS, SSM_HEAD_DIM, SSM_STATE)
    return (y_p, y_s, k_p, v_p, conv_p[None], state(ssm_p, bp),
            mk_p.reshape(1, bp, mem_len, X_HEADS, X_HEAD_DIM), mv_p.reshape(1, bp, mem_len, X_HEADS, X_HEAD_DIM),
            k_s, v_s, conv_s[None], state(ssm_s, bs))
```

```python
import functools
import math

import jax
import jax.numpy as jnp
from jax import lax
from jax.experimental import pallas as pl
from jax.experimental.pallas import tpu as pltpu

F32 = jnp.float32
BF16 = jnp.bfloat16

D_MODEL = 1024
ATT_HEADS = 8
HEAD_DIM = 64
ATT_WIDTH = ATT_HEADS * HEAD_DIM
MOBA_BLOCK = 256
TOPK = 3
ROPE_THETA = 10000.0
SSM_HEADS = 8
SSM_HEAD_DIM = 64
SSM_INNER = SSM_HEADS * SSM_HEAD_DIM
SSM_GROUPS = 2
SSM_STATE = 128
CONV_W = 4
CONV_DIM = SSM_INNER + 2 * SSM_GROUPS * SSM_STATE
SSD_CHUNK = 128
X_HEADS = 4
X_HEAD_DIM = 128
X_WIDTH = X_HEADS * X_HEAD_DIM
D_FF = 4 * D_MODEL
EPS = 1e-6

LANES = 128
CONV_PAD = 8
VMEM_LIMIT = 56 * 1024 * 1024
NEG = -1e30
LOG2E = 1.4426950408889634
MOBA_GROUP = 4
ONES_ROWS = 16
FF_CHUNK = 1024
XATTN_SEQS_PER_STEP = 8
SSD_MIN_CHUNK = 16
SSD_SEQS_PER_STEP = 8
PAGE_BUFFERS = 3
PAGES_PER_STEP = 16


def _dot(a, b):
    return jnp.dot(a, b, preferred_element_type=F32)


def _dot_nt(a, b):
    return lax.dot_general(a, b, (((1,), (1,)), ((), ())), preferred_element_type=F32)


def _dot_tn(a, b):
    return lax.dot_general(a, b, (((0,), (0,)), ((), ())), preferred_element_type=F32)


def _split3(x):
    hi = x.astype(BF16)
    r = x - hi.astype(F32)
    mid = r.astype(BF16)
    lo = (r - mid.astype(F32)).astype(BF16)
    return hi, mid, lo


def _dot_f32(a, b, dot=None):
    dot = dot or _dot
    a0, a1, a2 = _split3(a)
    b0, b1, b2 = _split3(b)
    return (dot(a0, b0) + (dot(a0, b1) + dot(a1, b0)) + (dot(a1, b1) + dot(a0, b2) + dot(a2, b0)))


def _rms(x, g):
    ms = jnp.mean(x * x, axis=-1, keepdims=True)
    return x * lax.rsqrt(ms + EPS) * g


def _sigmoid(x):
    return 1.0 / (1.0 + jnp.exp(-x))


def _params(*sem):
    return pltpu.CompilerParams(dimension_semantics=sem, vmem_limit_bytes=VMEM_LIMIT)


def _const(shape):
    return pl.BlockSpec(shape, lambda *_: (0,) * len(shape))


def _const_single(shape):
    return pl.BlockSpec(shape, lambda *_: (0,) * len(shape), pipeline_mode=pl.Buffered(1))


def _inproj_kernel(x_ref, g_ref, wqkv_ref, wrest_ref, wdt_ref, gq_ref, gk_ref, cos_ref, sin_ref,
                   q_ref, k_ref, v_ref, z_ref, xbc_ref, dt_ref):
    hn = _rms(x_ref[...], g_ref[...]).astype(BF16)
    tm = hn.shape[0]
    cos = cos_ref[...]
    sin = sin_ref[...]
    half = HEAD_DIM // 2

    def proj_t(idx):
        return _dot_nt(wqkv_ref[idx * ATT_WIDTH:(idx + 1) * ATT_WIDTH, :], hn)

    def head_norm_rot(p, g_ref_):
        p3 = p.reshape(ATT_HEADS, HEAD_DIM, tm)
        gain = g_ref_[:, 0:1]
        pn = p3 * lax.rsqrt(jnp.mean(p3 * p3, axis=1, keepdims=True) + EPS) * gain
        x1 = pn[:, :half]
        x2 = pn[:, half:]
        rot = jnp.concatenate([x1 * cos - x2 * sin, x2 * cos + x1 * sin], axis=1)
        return rot.reshape(ATT_WIDTH, tm)

    q_ref[0] = head_norm_rot(proj_t(0), gq_ref)
    k_ref[0] = head_norm_rot(proj_t(1), gk_ref)
    v_ref[0] = proj_t(2)
    z_ref[...] = _dot_nt(hn, wrest_ref[0:SSM_INNER, :])
    xbc_ref[...] = _dot_nt(hn, wrest_ref[SSM_INNER:SSM_INNER + CONV_DIM, :])
    dt_ref[...] = _dot_nt(hn, wdt_ref[...])


def _inproj(x, nb, g, w_qkv_t, w_rest, w_dt, gq, gk, cos_t, sin_t, tm):
    n = x.shape[0]
    cols = n // nb
    tpb = cols // tm
    row = lambda w: pl.BlockSpec((tm, w), lambda i: (i, 0))
    tab = pl.BlockSpec((HEAD_DIM // 2, tm), lambda i: (0, i % tpb))
    tspec = pl.BlockSpec((1, ATT_WIDTH, tm), lambda i: (i // tpb, 0, i % tpb))
    outs = ([jax.ShapeDtypeStruct((nb, ATT_WIDTH, cols), F32)] * 3
            + [jax.ShapeDtypeStruct((n, w), F32) for w in (SSM_INNER, CONV_DIM, LANES)])
    return pl.pallas_call(
        _inproj_kernel,
        grid=(n // tm,),
        in_specs=[row(D_MODEL), _const(g.shape), _const_single(w_qkv_t.shape), _const_single(w_rest.shape),
                  _const(w_dt.shape), _const(gq.shape), _const(gk.shape), tab, tab],
        out_specs=[tspec, tspec, tspec, row(SSM_INNER), row(CONV_DIM), row(LANES)],
        out_shape=outs,
        compiler_params=_params("parallel"),
        name="inproj",
    )(x, g, w_qkv_t, w_rest, w_dt, gq, gk, cos_t, sin_t)


def _topk_keep(gate, n_valid, axis):
    idx = lax.broadcasted_iota(jnp.int32, gate.shape, axis)
    cnt = jnp.zeros(gate.shape, jnp.int32)
    for jp in range(gate.shape[axis]):
        gj = lax.slice_in_dim(gate, jp, jp + 1, axis=axis)
        beats = (gj > gate) | ((gj == gate) & (jp < idx))
        cnt = cnt + jnp.where(beats, jnp.where(jp < n_valid, 1, 0), 0)
    return (idx < n_valid) & (cnt < TOPK)


def _moba_kernel(qt_ref, kt_ref, vt_ref, o_ref, kb_ref, vb_ref, km_ref, *, nblk):
    bq = MOBA_BLOCK
    km_ref[...] = jnp.zeros(km_ref.shape, F32)
    for j in range(nblk):
        kj = kt_ref[0, :, j * bq:(j + 1) * bq].T
        kb_ref[j] = kj.astype(BF16)
        km_ref[j:j + 1, :] = jnp.mean(kj, axis=0, keepdims=True)
        vj = vt_ref[0, :, j * bq:(j + 1) * bq].astype(BF16)
        for h in range(2):
            vb_ref[j, h, 0:HEAD_DIM, :] = vj[h * HEAD_DIM:(h + 1) * HEAD_DIM, :]
            vb_ref[j, h, HEAD_DIM:, :] = jnp.ones((ONES_ROWS, bq), BF16)
    kmean = km_ref[...]

    keyi = lax.broadcasted_iota(jnp.int32, (bq, bq), 0)
    qryi = lax.broadcasted_iota(jnp.int32, (bq, bq), 1)
    causal = jnp.where(keyi <= qryi, 0.0, NEG)
    row_lo = lax.broadcasted_iota(jnp.int32, (LANES, bq), 0) < HEAD_DIM
    qscale = (HEAD_DIM ** -0.5) * LOG2E

    order = [blk for pair in zip(range(nblk), reversed(range(nblk))) for blk in pair][:nblk]
    for g0 in range(0, nblk, MOBA_GROUP):
        owns = order[g0:g0 + MOBA_GROUP]
        work = []
        for own in owns:
            qt = qt_ref[0, :, own * bq:(own + 1) * bq]
            for h in range(2):
                qh = jnp.where(row_lo, 0.0, qt) if h else jnp.where(row_lo, qt, 0.0)
                qsb = (qh * qscale).astype(BF16)
                bias = None
                if own > TOPK:
                    keep = _topk_keep(_dot_f32(kmean, qh), own, 0)
                    bias = jnp.where(keep, 0.0, NEG)
                tiles = []
                m = None
                for j in range(own + 1):
                    s = _dot(kb_ref[j], qsb)
                    if j == own:
                        s = s + causal
                    elif bias is not None:
                        s = s + bias[j:j + 1, :]
                    tiles.append(s)
                    mj = jnp.max(s, axis=0, keepdims=True)
                    m = mj if m is None else jnp.maximum(m, mj)
                work.append((own, h, tiles, m))
        outs = {}
        for own, h, tiles, m in work:
            acc = jnp.zeros((HEAD_DIM + ONES_ROWS, bq), F32)
            for j in range(own + 1):
                p = jnp.exp2(tiles[j] - m)
                acc = acc + _dot(vb_ref[j, h], p.astype(BF16))
            outs[(own, h)] = acc[0:HEAD_DIM] / acc[HEAD_DIM:HEAD_DIM + 1]
        for own in owns:
            o_t = jnp.concatenate([outs[(own, 0)], outs[(own, 1)]], axis=0)
            o_ref[0, own * bq:(own + 1) * bq, :] = o_t.T.astype(o_ref.dtype)


def _moba_prompt(qt, kt, vt):
    b, _, t = qt.shape
    nblk = t // MOBA_BLOCK
    npad = max(16, nblk)
    npair = ATT_WIDTH // LANES
    spec = pl.BlockSpec((1, LANES, t), lambda bi, p: (bi, p, 0))
    return pl.pallas_call(
        functools.partial(_moba_kernel, nblk=nblk),
        grid=(b, npair),
        in_specs=[spec, spec, spec],
        out_specs=pl.BlockSpec((1, t, LANES), lambda bi, p: (bi, 0, p)),
        out_shape=jax.ShapeDtypeStruct((b, t, ATT_WIDTH), BF16),
        scratch_shapes=[pltpu.VMEM((nblk, MOBA_BLOCK, LANES), BF16),
                        pltpu.VMEM((nblk, 2, HEAD_DIM + ONES_ROWS, MOBA_BLOCK), BF16),
                        pltpu.VMEM((npad, LANES), F32)],
        compiler_params=_params("parallel", "parallel"),
        name="moba_prompt",
    )(qt, kt, vt)


def _expand_heads(q):
    t = q.shape[0]
    tiled = jnp.concatenate([q] * ATT_HEADS, axis=0)
    rowh = lax.broadcasted_iota(jnp.int32, tiled.shape, 0) // t
    laneh = lax.broadcasted_iota(jnp.int32, tiled.shape, 1) // HEAD_DIM
    return jnp.where(rowh == laneh, tiled, 0.0)


def _sample_moba_kernel(pt_ref, q_ref, kn_ref, vn_ref, ck_ref, cv_ref, o_ref, km_ref, m_ref, l_ref, oblk_ref,
                        kbuf_ref, vbuf_ref, sem_ref, *, page, npg, nblk):
    c = pl.program_id(1)
    nsteps = pl.num_programs(1)
    step = pl.program_id(0) * nsteps + c
    total = pl.num_programs(0) * nsteps

    def page_copies(st):
        slot = st % PAGE_BUFFERS
        seq = st // nsteps
        first = (st % nsteps) * npg
        out = []
        for pi in range(npg):
            pg = pt_ref[seq, first + pi]
            out.append(pltpu.make_async_copy(ck_ref.at[0, pg], kbuf_ref.at[slot, pi], sem_ref.at[0, slot]))
            out.append(pltpu.make_async_copy(cv_ref.at[0, pg], vbuf_ref.at[slot, pi], sem_ref.at[1, slot]))
        return out

    def start_pages(st):
        for n, cp in enumerate(page_copies(st)):
            cp.start(priority=n % 2)

    @pl.when(step == 0)
    def _():
        for ahead in range(PAGE_BUFFERS - 1):
            start_pages(ahead)

    @pl.when(step + (PAGE_BUFFERS - 1) < total)
    def _():
        start_pages(step + (PAGE_BUFFERS - 1))

    for cp in page_copies(step):
        cp.wait()
    slot = step % PAGE_BUFFERS
    k_refs = [kbuf_ref.at[slot, pi] for pi in range(npg)]
    v_refs = [vbuf_ref.at[slot, pi] for pi in range(npg)]
    t = q_ref.shape[1]
    rows = ATT_HEADS * t
    pages_per_block = MOBA_BLOCK // page
    blocks_per_step = npg // pages_per_block

    qe = _expand_heads(q_ref[0])
    qb = (qe * ((HEAD_DIM ** -0.5) * LOG2E)).astype(BF16)
    blk_lane = lax.broadcasted_iota(jnp.int32, (ATT_WIDTH, LANES), 1)

    @pl.when(c == 0)
    def _():
        km_ref[...] = jnp.zeros(km_ref.shape, F32)

    tiles = []
    for jj in range(blocks_per_step):
        ksum = None
        for u in range(pages_per_block):
            kt = k_refs[jj * pages_per_block + u][...].reshape(ATT_WIDTH, page)
            ksum = kt if ksum is None else ksum + kt
            tiles.append(_dot(qb, kt.astype(BF16)))
        kmean = jnp.sum(ksum, axis=1, keepdims=True) * (1.0 / MOBA_BLOCK)
        km_ref[...] = jnp.where(blk_lane == c * blocks_per_step + jj, kmean, km_ref[...])
    probs = []
    for jj in range(blocks_per_step):
        mine = tiles[jj * pages_per_block:(jj + 1) * pages_per_block]
        m = jnp.max(mine[0], axis=1, keepdims=True)
        for s in mine[1:]:
            m = jnp.maximum(m, jnp.max(s, axis=1, keepdims=True))
        l = jnp.zeros((rows, 1), F32)
        for s in mine:
            p = jnp.exp2(s - m)
            l = l + jnp.sum(p, axis=1, keepdims=True)
            probs.append(p.astype(BF16))
        m_ref[c * blocks_per_step + jj] = jnp.broadcast_to(m, (rows, LANES))
        l_ref[c * blocks_per_step + jj] = jnp.broadcast_to(l, (rows, LANES))
    for jj in range(blocks_per_step):
        o = jnp.zeros((rows, ATT_WIDTH), F32)
        for u in range(pages_per_block):
            vt = v_refs[jj * pages_per_block + u][...].reshape(ATT_WIDTH, page).astype(BF16)
            o = o + _dot_nt(probs[jj * pages_per_block + u], vt)
        oblk_ref[c * blocks_per_step + jj] = o

    @pl.when(c == pl.num_programs(1) - 1)
    def _():
        keep = _topk_keep(_dot_f32(qe, km_ref[:, 0:nblk]), nblk, 1)
        s_own = _dot_nt(qb, kn_ref[0].astype(BF16))
        qpos = lax.broadcasted_iota(jnp.int32, s_own.shape, 0) % t
        kpos = lax.broadcasted_iota(jnp.int32, s_own.shape, 1)
        s_own = jnp.where(kpos <= qpos, s_own, NEG)
        lane_j = lax.broadcasted_iota(jnp.int32, (rows, nblk), 1)
        m_all = jnp.zeros((rows, nblk), F32)
        l_all = jnp.zeros((rows, nblk), F32)
        for j in range(nblk):
            m_all = jnp.where(lane_j == j, m_ref[j][:, 0:nblk], m_all)
            l_all = jnp.where(lane_j == j, l_ref[j][:, 0:nblk], l_all)
        mx = jnp.maximum(jnp.max(s_own, axis=1, keepdims=True),
                         jnp.max(jnp.where(keep, m_all, NEG), axis=1, keepdims=True))
        p_own = jnp.exp2(s_own - mx)
        w = jnp.where(keep, jnp.exp2(jnp.minimum(m_all - mx, 0.0)), 0.0)
        den = jnp.sum(p_own, axis=1, keepdims=True) + jnp.sum(w * l_all, axis=1, keepdims=True)
        vn = vn_ref[0]
        num = jnp.zeros((rows, ATT_WIDTH), F32)
        for tt in range(t):
            num = num + p_own[:, tt:tt + 1] * vn[tt:tt + 1, :]
        for j in range(nblk):
            num = num + w[:, j:j + 1] * oblk_ref[j]
        full = num / den
        laneh = lax.broadcasted_iota(jnp.int32, (t, ATT_WIDTH), 1) // HEAD_DIM
        out = jnp.zeros((t, ATT_WIDTH), F32)
        for h in range(ATT_HEADS):
            out = out + jnp.where(laneh == h, full[h * t:(h + 1) * t, :], 0.0)
        o_ref[0] = out


def _sample_moba(page_table, q, k_new, v_new, cache_kt, cache_vt):
    b, t, _ = q.shape
    page = cache_kt.shape[4]
    n_pages = page_table.shape[1]
    npg = math.gcd(n_pages, PAGES_PER_STEP)
    nsteps = n_pages // npg
    nblk = n_pages * page // MOBA_BLOCK
    assert nblk <= LANES and npg % (MOBA_BLOCK // page) == 0 and b * nsteps >= PAGE_BUFFERS
    rows = ATT_HEADS * t
    per_b = pl.BlockSpec((1, t, ATT_WIDTH), lambda bi, c, pt: (bi, 0, 0))
    hbm = pl.BlockSpec(memory_space=pl.ANY)
    page_buf = pltpu.VMEM((PAGE_BUFFERS, npg, ATT_HEADS, HEAD_DIM, page), F32)

    grid_spec = pltpu.PrefetchScalarGridSpec(
        num_scalar_prefetch=1,
        grid=(b, nsteps),
        in_specs=[per_b, per_b, per_b, hbm, hbm],
        out_specs=per_b,
        scratch_shapes=[pltpu.VMEM((ATT_WIDTH, LANES), F32),
                        pltpu.VMEM((nblk, rows, LANES), F32),
                        pltpu.VMEM((nblk, rows, LANES), F32),
                        pltpu.VMEM((nblk, rows, ATT_WIDTH), F32),
                        page_buf, page_buf,
                        pltpu.SemaphoreType.DMA((2, PAGE_BUFFERS))],
    )
    return pl.pallas_call(
        functools.partial(_sample_moba_kernel, page=page, npg=npg, nblk=nblk),
        grid_spec=grid_spec,
        out_shape=jax.ShapeDtypeStruct((b, t, ATT_WIDTH), F32),
        compiler_params=_params("arbitrary", "arbitrary"),
        name="sample_moba",
    )(page_table, q, k_new, v_new, cache_kt, cache_vt)


def _cumsum_rows(x):
    n = x.shape[0]
    row = lax.broadcasted_iota(jnp.int32, x.shape, 0)
    step = 1
    while step < n:
        x = x + jnp.where(row >= step, pltpu.roll(x, step, 0), 0.0)
        step *= 2
    return x


def _ssd_kernel(xbc_ref, dtr_ref, z_ref, conv0_ref, h0_ref, cw_ref, cb_ref, dtb_ref, alog_ref,
                dsk_ref, gn_ref, y_ref, convo_ref, ho_ref, xpad_ref, h_ref, *, cl, valid, nbat):
    c = pl.program_id(1)
    nc = pl.num_programs(1)

    @pl.when(c == 0)
    def _():
        for bb in range(nbat):
            xpad_ref[bb, 0:CONV_PAD, :] = conv0_ref[bb]
            h_ref[bb] = h0_ref[bb]

    for bb in range(nbat):
        _ssd_chunk(bb, xbc_ref, dtr_ref, z_ref, cw_ref, cb_ref, dtb_ref, alog_ref, dsk_ref, gn_ref,
                   y_ref, xpad_ref, h_ref, cl, valid)

    @pl.when(c == nc - 1)
    def _():
        for bb in range(nbat):
            convo_ref[bb] = xpad_ref[bb, CONV_PAD + valid - (CONV_W - 1):CONV_PAD + valid, :]
            ho_ref[bb] = h_ref[bb]


def _ssd_chunk(bb, xbc_ref, dtr_ref, z_ref, cw_ref, cb_ref, dtb_ref, alog_ref, dsk_ref, gn_ref,
               y_ref, xpad_ref, h_ref, cl, valid):
    x = xbc_ref[bb]
    xpad_ref[bb, CONV_PAD:CONV_PAD + cl, :] = x
    w = cw_ref[...]
    conv = cb_ref[...] + x * w[CONV_W - 1:CONV_W, :]
    for back in range(1, CONV_W):
        conv = conv + xpad_ref[bb, pl.ds(CONV_PAD - back, cl), :] * w[CONV_W - 1 - back:CONV_W - back, :]

    xpad_ref[bb, 0:CONV_PAD, :] = xpad_ref[bb, cl:cl + CONV_PAD, :]
    xc = conv * _sigmoid(conv)
    xs = xc[:, :SSM_INNER]
    gw = SSM_STATE
    bm = [xc[:, SSM_INNER + g * gw:SSM_INNER + (g + 1) * gw].astype(BF16) for g in range(SSM_GROUPS)]
    cm = [xc[:, SSM_INNER + (SSM_GROUPS + g) * gw:SSM_INNER + (SSM_GROUPS + g + 1) * gw].astype(BF16)
          for g in range(SSM_GROUPS)]

    pre = dtr_ref[bb] + dtb_ref[...]
    dt = jnp.maximum(pre, 0.0) + jnp.log1p(jnp.exp(-jnp.abs(pre)))
    rowl = lax.broadcasted_iota(jnp.int32, dt.shape, 0)
    if valid < cl:
        dt = jnp.where(rowl < valid, dt, 0.0)
    da = dt * (-jnp.exp(alog_ref[...]))
    cum = _cumsum_rows(da)
    if cl < LANES:
        cum_t = jnp.concatenate([cum, jnp.zeros((LANES - cl, LANES), F32)], axis=0).T[:, 0:cl]
    else:
        cum_t = cum.T
    cum_last = cum[cl - 1:cl, :]
    ecum = jnp.exp(cum)
    tail = jnp.exp(cum_last - cum)
    elast = jnp.exp(cum_last)

    ri = lax.broadcasted_iota(jnp.int32, (cl, cl), 0)
    ci = lax.broadcasted_iota(jnp.int32, (cl, cl), 1)
    causal = ci <= ri
    lane = lax.broadcasted_iota(jnp.int32, (cl, LANES), 1)
    lo = lane < SSM_HEAD_DIM
    row_lo = lax.broadcasted_iota(jnp.int32, (LANES, 1), 0) < SSM_HEAD_DIM
    heads_per_group = SSM_HEADS // SSM_GROUPS
    gmat = [_dot_nt(cm[g], bm[g]) for g in range(SSM_GROUPS)]

    def pick(cols, h0):
        return jnp.where(lo, cols[:, h0:h0 + 1], cols[:, h0 + 1:h0 + 2])

    ys = []
    for p in range(SSM_HEADS // 2):
        h0 = 2 * p
        g = h0 // heads_per_group
        xs_p = xs[:, p * LANES:(p + 1) * LANES]
        xdt = xs_p * pick(dt, h0)
        xdt_b = xdt.astype(BF16)
        parts = []
        for h in (h0, h0 + 1):
            seg = cum[:, h:h + 1] - cum_t[h:h + 1, :]
            dec = jnp.where(causal, jnp.exp(jnp.minimum(seg, 0.0)), 0.0)
            parts.append(_dot((gmat[g] * dec).astype(BF16), xdt_b))
        hp = h_ref[bb, p * LANES:(p + 1) * LANES, :]
        y_p = jnp.where(lo, parts[0], parts[1])
        y_p = y_p + _dot_nt(cm[g], hp.astype(BF16)) * pick(ecum, h0)
        y_p = y_p + dsk_ref[:, p * LANES:(p + 1) * LANES] * xs_p
        xw = (xdt * pick(tail, h0)).astype(BF16)
        decay_rows = jnp.where(row_lo, elast[:, h0:h0 + 1], elast[:, h0 + 1:h0 + 2])
        h_ref[bb, p * LANES:(p + 1) * LANES, :] = hp * decay_rows + _dot_tn(xw, bm[g])
        ys.append(y_p)

    y = jnp.concatenate(ys, axis=1)
    zz = z_ref[bb]
    yz = y * (zz * _sigmoid(zz))
    gwid = SSM_INNER // SSM_GROUPS
    normed = []
    for g in range(SSM_GROUPS):
        part = yz[:, g * gwid:(g + 1) * gwid]
        normed.append(part * lax.rsqrt(jnp.mean(part * part, axis=-1, keepdims=True) + EPS))
    y_ref[bb] = (jnp.concatenate(normed, axis=1) * gn_ref[...]).astype(y_ref.dtype)


def _ssd(xbc, dtr, z, conv0, h0, cw, cb, dtb, alog, dsk, gn, cl, valid, out_dtype):
    b, t, _ = xbc.shape
    nc = t // cl
    nbat = math.gcd(b, SSD_SEQS_PER_STEP)
    rows = lambda w: pl.BlockSpec((nbat, cl, w), lambda bi, c: (bi, c, 0))
    per_b = lambda shape: pl.BlockSpec((nbat,) + shape, lambda bi, c: (bi, 0, 0))
    hshape = (SSM_HEADS * SSM_HEAD_DIM, SSM_STATE)
    return pl.pallas_call(
        functools.partial(_ssd_kernel, cl=cl, valid=valid, nbat=nbat),
        grid=(b // nbat, nc),
        in_specs=[rows(CONV_DIM), rows(LANES), rows(SSM_INNER), per_b((CONV_PAD, CONV_DIM)),
                  per_b(hshape), _const(cw.shape), _const(cb.shape), _const(dtb.shape),
                  _const(alog.shape), _const(dsk.shape), _const(gn.shape)],
        out_specs=[rows(SSM_INNER), per_b((CONV_W - 1, CONV_DIM)), per_b(hshape)],
        out_shape=[jax.ShapeDtypeStruct((b, t, SSM_INNER), out_dtype),
                   jax.ShapeDtypeStruct((b, CONV_W - 1, CONV_DIM), F32),
                   jax.ShapeDtypeStruct((b,) + hshape, F32)],
        scratch_shapes=[pltpu.VMEM((nbat, CONV_PAD + cl, CONV_DIM), F32), pltpu.VMEM((nbat,) + hshape, F32)],
        compiler_params=_params("parallel", "arbitrary"),
        name="ssd",
    )(xbc, dtr, z, conv0, h0, cw, cb, dtb, alog, dsk, gn)


def _head_slice(h):
    return slice(h * X_HEAD_DIM, (h + 1) * X_HEAD_DIM)


def _outproj_body(x, att, y, wa_ref, wy_ref, g_ref, wq_ref, gq_ref):
    x1 = x + _dot(att.astype(BF16), wa_ref[...]) + _dot(y.astype(BF16), wy_ref[...])
    qx = _dot(_rms(x1, g_ref[...]).astype(BF16), wq_ref[...])
    gq = gq_ref[...]
    return x1, [_rms(qx[:, _head_slice(h)], gq) for h in range(X_HEADS)]


def _outproj_kernel(x_ref, att_ref, y_ref, wa_ref, wy_ref, g_ref, wq_ref, gq_ref, x1_ref, qx_ref):
    x1, q_heads = _outproj_body(x_ref[...], att_ref[...], y_ref[...], wa_ref, wy_ref, g_ref, wq_ref, gq_ref)
    x1_ref[...] = x1
    for h in range(X_HEADS):
        qx_ref[:, _head_slice(h)] = q_heads[h].astype(qx_ref.dtype)


def _outproj(x, att, y, w_att, w_y, g, wq, gq, tm, act_dtype):
    n = x.shape[0]
    row = lambda w: pl.BlockSpec((tm, w), lambda i: (i, 0))
    return pl.pallas_call(
        _outproj_kernel,
        grid=(n // tm,),
        in_specs=[row(D_MODEL), row(ATT_WIDTH), row(SSM_INNER), _const(w_att.shape), _const(w_y.shape),
                  _const(g.shape), _const(wq.shape), _const(gq.shape)],
        out_specs=[row(D_MODEL), row(X_WIDTH)],
        out_shape=[jax.ShapeDtypeStruct((n, D_MODEL), F32), jax.ShapeDtypeStruct((n, X_WIDTH), act_dtype)],
        compiler_params=_params("parallel"),
        name="outproj",
    )(x, att, y, w_att, w_y, g, wq, gq)


def _xattn_head(q, k, v):
    s = _dot_nt((q.astype(F32) * (X_HEAD_DIM ** -0.5)).astype(BF16), k.astype(BF16))
    p = jnp.exp(s - jnp.max(s, axis=-1, keepdims=True))
    return _dot(p.astype(BF16), v.astype(BF16)) / jnp.sum(p, axis=-1, keepdims=True)


def _xattn_rows_kernel(q_ref, k_ref, v_ref, o_ref):
    gb, t, _ = q_ref.shape
    flat = k_ref.shape[1]
    rowh = lax.broadcasted_iota(jnp.int32, (X_HEADS * t, flat), 0) // t
    colh = lax.broadcasted_iota(jnp.int32, (X_HEADS * t, flat), 1) % X_HEADS
    hmask = jnp.where(rowh == colh, 0.0, NEG)
    for g in range(gb):
        q = q_ref[g].astype(F32)
        q_rows = jnp.concatenate([q[:, _head_slice(h)] for h in range(X_HEADS)], axis=0)
        s = _dot_nt((q_rows * (X_HEAD_DIM ** -0.5)).astype(BF16), k_ref[g].astype(BF16)) + hmask
        p = jnp.exp(s - jnp.max(s, axis=-1, keepdims=True))
        o_rows = _dot(p.astype(BF16), v_ref[g].astype(BF16)) / jnp.sum(p, axis=-1, keepdims=True)
        for h in range(X_HEADS):
            o_ref[g, :, _head_slice(h)] = o_rows[h * t:(h + 1) * t].astype(o_ref.dtype)


def _xattn_rows(q, mk, mv, act_dtype):
    b, t, _ = q.shape
    flat = mk.shape[1]
    gb = math.gcd(b, XATTN_SEQS_PER_STEP)
    qspec = pl.BlockSpec((gb, t, X_WIDTH), lambda bi: (bi, 0, 0))
    kvspec = pl.BlockSpec((gb, flat, X_HEAD_DIM), lambda bi: (bi, 0, 0))
    return pl.pallas_call(
        _xattn_rows_kernel,
        grid=(b // gb,),
        in_specs=[qspec, kvspec, kvspec],
        out_specs=qspec,
        out_shape=jax.ShapeDtypeStruct((b, t, X_WIDTH), act_dtype),
        compiler_params=_params("parallel"),
        name="xattn",
    )(q, mk, mv)


def _mlp_body(x1, o, wo_ref, g_ref, wu_ref, wd_ref):
    x2 = x1 + _dot(o.astype(BF16), wo_ref[...])
    hm = _rms(x2, g_ref[...]).astype(BF16)
    acc = x2
    for cidx in range(D_FF // FF_CHUNK):
        sl = slice(cidx * FF_CHUNK, (cidx + 1) * FF_CHUNK)
        up = jnp.maximum(_dot(hm, wu_ref[:, sl]), 0.0)
        acc = acc + _dot((up * up).astype(BF16), wd_ref[sl, :])
    return acc


def _mlp_kernel(x1_ref, o_ref, wo_ref, g_ref, wu_ref, wd_ref, out_ref):
    out_ref[...] = _mlp_body(x1_ref[...], o_ref[...], wo_ref, g_ref, wu_ref, wd_ref)


def _mlp(x1, o, wo, g, wu, wd, tm):
    n = x1.shape[0]
    row = lambda w: pl.BlockSpec((tm, w), lambda i: (i, 0))
    return pl.pallas_call(
        _mlp_kernel,
        grid=(n // tm,),
        in_specs=[row(D_MODEL), row(X_WIDTH), _const_single(wo.shape), _const(g.shape),
                  _const_single(wu.shape), _const_single(wd.shape)],
        out_specs=row(D_MODEL),
        out_shape=jax.ShapeDtypeStruct((n, D_MODEL), F32),
        compiler_params=_params("parallel"),
        name="mlp",
    )(x1, o, wo, g, wu, wd)


def _post_kernel(x_ref, att_ref, y_ref, mk_ref, mv_ref, wa_ref, wy_ref, gx_ref, wq_ref, gq_ref,
                 wo_ref, gm_ref, wu_ref, wd_ref, out_ref):
    x1, q_heads = _outproj_body(x_ref[...], att_ref[...], y_ref[...], wa_ref, wy_ref, gx_ref, wq_ref, gq_ref)
    o = jnp.concatenate([_xattn_head(q_heads[h], mk_ref[0, :, _head_slice(h)], mv_ref[0, :, _head_slice(h)])
                         for h in range(X_HEADS)], axis=1)
    out_ref[...] = _mlp_body(x1, o, wo_ref, gm_ref, wu_ref, wd_ref)


def _post(x, att, y, mk, mv, wts, tm, t):
    n = x.shape[0]
    tiles_per_seq = t // tm
    m = mk.shape[1]
    row = lambda w: pl.BlockSpec((tm, w), lambda i: (i, 0))
    mem = pl.BlockSpec((1, m, X_WIDTH), lambda i: (i // tiles_per_seq, 0, 0))
    names = ("w_out_att", "w_out_ssm", "ln_x_g", "wq_x", "qx_norm_g", "wo_x", "ln_mlp_g", "w_up", "w_down")
    big = ("w_up", "w_down")
    weights = [wts[k] for k in names]
    return pl.pallas_call(
        _post_kernel,
        grid=(n // tm,),
        in_specs=[row(D_MODEL), row(ATT_WIDTH), row(SSM_INNER), mem, mem]
                 + [(_const_single if k in big else _const)(wts[k].shape) for k in names],
        out_specs=row(D_MODEL),
        out_shape=jax.ShapeDtypeStruct((n, D_MODEL), F32),
        compiler_params=_params("parallel"),
        name="post",
    )(x, att, y, mk, mv, *weights)


def _memkv_kernel(m_ref, g_ref, wk_ref, wv_ref, gk_ref, mk_ref, mv_ref):
    mn = _rms(m_ref[...], g_ref[...]).astype(BF16)
    mk = _dot(mn, wk_ref[...])
    gk = gk_ref[...]
    for h in range(X_HEADS):
        sl = slice(h * X_HEAD_DIM, (h + 1) * X_HEAD_DIM)
        mk_ref[:, sl] = _rms(mk[:, sl], gk)
    mv_ref[...] = _dot(mn, wv_ref[...])


def _memkv(mem, g, wk, wv, gk, tm):
    n = mem.shape[0]
    row = lambda w: pl.BlockSpec((tm, w), lambda i: (i, 0))
    return pl.pallas_call(
        _memkv_kernel,
        grid=(n // tm,),
        in_specs=[row(D_MODEL), _const(g.shape), _const(wk.shape), _const(wv.shape), _const(gk.shape)],
        out_specs=[row(X_WIDTH), row(X_WIDTH)],
        out_shape=[jax.ShapeDtypeStruct((n, X_WIDTH), F32)] * 2,
        compiler_params=_params("parallel"),
        name="memkv",
    )(mem, g, wk, wv, gk)


def _rope_tables(pos):
    half = HEAD_DIM // 2
    inv_freq = ROPE_THETA ** (-jnp.arange(half, dtype=F32) / half)
    ang = pos.astype(F32)[:, None] * inv_freq[None, :]
    return jnp.cos(ang).T, jnp.sin(ang).T


def _row_tile(n, want):
    tm = min(n, want)
    assert n % tm == 0
    return tm


def _layer(x, nb, cos_t, sin_t, conv0, h0, mem_k, mem_v, wts, attend, tm, act_dtype):
    b, t, _ = x.shape
    n = b * t
    xf = x.reshape(n, D_MODEL)
    qt, kt, vt, z, xbc, dtr = _inproj(xf, nb, wts["ln_mix_g"], wts["w_qkv_t"], wts["w_rest_t"], wts["w_dt_t"],
                                      wts["q_norm_g"], wts["k_norm_g"], cos_t, sin_t, tm)
    att, extras = attend(qt, kt, vt)

    cl = SSD_CHUNK if t >= SSD_CHUNK else -(-t // SSD_MIN_CHUNK) * SSD_MIN_CHUNK
    tp = -(-t // cl) * cl
    pad3 = lambda a: jnp.pad(a.reshape(b, t, -1), ((0, 0), (0, tp - t), (0, 0)))
    y, new_conv, new_h = _ssd(pad3(xbc), pad3(dtr), pad3(z), conv0, h0, wts["conv_w"], wts["conv_b"],
                              wts["dt_bias"], wts["a_log"], wts["d_skip"], wts["ssm_norm_g"],
                              cl, min(t, cl) if tp != t else cl, act_dtype)
    y = y[:, :t].reshape(n, SSM_INNER)

    if t % tm == 0:
        out = _post(xf, att, y, mem_k, mem_v, wts, tm, t)
    else:
        x1, qx = _outproj(xf, att, y, wts["w_out_att"], wts["w_out_ssm"], wts["ln_x_g"],
                          wts["wq_x"], wts["qx_norm_g"], tm, act_dtype)
        o = _xattn_rows(qx.reshape(b, t, X_WIDTH), mem_k, mem_v, act_dtype)
        out = _mlp(x1, o.reshape(n, X_WIDTH), wts["wo_x"], wts["ln_mlp_g"], wts["w_up"], wts["w_down"], tm)
    return out.reshape(b, t, D_MODEL), extras, new_conv, new_h


def kernel(x_prompt, x_sample, mem_prompt, cache_k, cache_v, page_table, state_conv, state_ssm,
           cache_mem_k, cache_mem_v, ln_mix_g, w_in, q_norm_g, k_norm_g, conv_w, conv_b, dt_bias,
           a_log, d_skip, ssm_norm_g, w_out, ln_x_g, ln_mem_g, wq_x, wk_x, wv_x, qx_norm_g,
           kx_norm_g, wo_x, ln_mlp_g, w_up, w_down):
    depth = w_in.shape[0]
    assert depth == 1, "single-layer step"
    bp, tp, _ = x_prompt.shape
    bs, ts, _ = x_sample.shape
    page = cache_k.shape[2]
    past_len = page_table.shape[1] * page
    assert tp % MOBA_BLOCK == 0 and past_len % MOBA_BLOCK == 0 and MOBA_BLOCK % page == 0
    assert ts <= MOBA_BLOCK and ts <= SSD_CHUNK and ts >= CONV_W - 1
    l = 0
    qkv_cols = 3 * ATT_WIDTH
    main_cols = qkv_cols + SSM_INNER + CONV_DIM

    def lanes(vec, width=LANES):
        return jnp.pad(vec.astype(F32), (0, width - vec.shape[0])).reshape(1, width)

    def dim_gain(vec):
        return jnp.broadcast_to(vec.astype(F32)[:, None], (HEAD_DIM, LANES))

    wts = {
        "ln_mix_g": ln_mix_g[l].reshape(1, D_MODEL),
        "w_qkv_t": w_in[l][:, :qkv_cols].T.astype(BF16),
        "w_rest_t": w_in[l][:, qkv_cols:main_cols].T.astype(BF16),
        "w_dt_t": jnp.pad(w_in[l][:, main_cols:].T, ((0, LANES - SSM_HEADS), (0, 0))).astype(BF16),
        "q_norm_g": dim_gain(q_norm_g[l]),
        "k_norm_g": dim_gain(k_norm_g[l]),
        "conv_w": conv_w[l],
        "conv_b": conv_b[l].reshape(1, CONV_DIM),
        "dt_bias": lanes(dt_bias[l]),
        "a_log": lanes(a_log[l]),
        "d_skip": jnp.repeat(d_skip[l].astype(F32), SSM_HEAD_DIM).reshape(1, SSM_INNER),
        "ssm_norm_g": ssm_norm_g[l].reshape(1, SSM_INNER),
        "w_out_att": w_out[l][:ATT_WIDTH].astype(BF16),
        "w_out_ssm": w_out[l][ATT_WIDTH:].astype(BF16),
        "ln_x_g": ln_x_g[l].reshape(1, D_MODEL),
        "wq_x": wq_x[l].astype(BF16),
        "qx_norm_g": qx_norm_g[l].reshape(1, X_HEAD_DIM),
        "wo_x": wo_x[l].astype(BF16),
        "ln_mlp_g": ln_mlp_g[l].reshape(1, D_MODEL),
        "w_up": w_up[l].astype(BF16),
        "w_down": w_down[l].astype(BF16),
    }

    mem_rows = mem_prompt.reshape(-1, D_MODEL)
    mk_p, mv_p = _memkv(mem_rows, ln_mem_g[l].reshape(1, D_MODEL), wk_x[l].astype(BF16),
                        wv_x[l].astype(BF16), kx_norm_g[l].reshape(1, X_HEAD_DIM),
                        _row_tile(mem_rows.shape[0], 512))
    mem_len = mem_prompt.shape[1]
    mk_p = mk_p.reshape(bp, mem_len, X_WIDTH)
    mv_p = mv_p.reshape(bp, mem_len, X_WIDTH)
    cos_p, sin_p = _rope_tables(jnp.arange(tp, dtype=jnp.int32))
    conv0_p = jnp.zeros((bp, CONV_PAD, CONV_DIM), F32)
    h0_p = jnp.zeros((bp, SSM_HEADS * SSM_HEAD_DIM, SSM_STATE), F32)

    def attend_prompt(qt, kt, vt):
        heads_t = lambda a: a.reshape(1, bp, ATT_HEADS, HEAD_DIM, tp).transpose(0, 1, 4, 2, 3)
        return _moba_prompt(qt, kt, vt).reshape(bp * tp, ATT_WIDTH), (heads_t(kt), heads_t(vt))

    y_p, (k_p, v_p), conv_p, ssm_p = _layer(x_prompt, bp, cos_p, sin_p, conv0_p, h0_p, mk_p, mv_p, wts,
                                            attend_prompt, _row_tile(tp, 512), BF16)

    cos_s, sin_s = _rope_tables(past_len + jnp.arange(ts, dtype=jnp.int32))
    cos_s = jnp.tile(cos_s, (1, bs))
    sin_s = jnp.tile(sin_s, (1, bs))
    conv0_s = jnp.pad(state_conv[l], ((0, 0), (CONV_PAD - (CONV_W - 1), 0), (0, 0)))
    h0_s = state_ssm[l].reshape(bs, SSM_HEADS * SSM_HEAD_DIM, SSM_STATE)
    cache_kt = cache_k.transpose(0, 1, 3, 4, 2)
    cache_vt = cache_v.transpose(0, 1, 3, 4, 2)

    def attend_sample(qt, kt, vt):
        rows = lambda a: a[0].T.reshape(bs, ts, ATT_WIDTH)
        k_new, v_new = rows(kt), rows(vt)
        o = _sample_moba(page_table, rows(qt), k_new, v_new, cache_kt, cache_vt)
        heads = lambda a: a.reshape(1, bs, ts, ATT_HEADS, HEAD_DIM)
        return o.reshape(bs * ts, ATT_WIDTH), (heads(k_new), heads(v_new))

    y_s, (k_s, v_s), conv_s, ssm_s = _layer(
        x_sample, 1, cos_s, sin_s, conv0_s, h0_s, cache_mem_k[l].reshape(bs, -1, X_HEAD_DIM),
        cache_mem_v[l].reshape(bs, -1, X_HEAD_DIM), wts, attend_sample, bs * ts, F32)

    state = lambda a, b: a.reshape(1, b, SSM_HEADS, SSM_HEAD_DIM, SSM_STATE)
    return (y_p, y_s, k_p, v_p, conv_p[None], state(ssm_p, bp),
            mk_p.reshape(1, bp, mem_len, X_HEADS, X_HEAD_DIM), mv_p.reshape(1, bp, mem_len, X_HEADS, X_HEAD_DIM),
            k_s, v_s, conv_s[None], state(ssm_s, bs))
```
